```python
import math
import jax, jax.numpy as jnp
from jax import lax
import numpy as np

D_MODEL = 1024
BATCH = 8
SEQ = 8192
DEPTH = 4
DEC_BATCH = 8
DEC_SEQ = 64
PAST_LEN = 2048

CHUNK = 64
EPS = 1e-6
GLA_HEADS = 4
GLA_DK = 32
GLA_DV = 64
GLA_RANK = 16
GLA_TAU = 16.0
RET_HEADS = 6
RET_D = 64
ML_HEADS = 6
ML_D = 64
CONV_W = 4
ROPE_BASE = 10000.0
GLA_W = GLA_HEADS * GLA_DV
RET_W = RET_HEADS * RET_D
ML_W = ML_HEADS * ML_D
MIX_W = GLA_W + RET_W + ML_W
D_FF = ((8 * D_MODEL // 3 + 255) // 256) * 256
SPLIT_SIZES = (GLA_HEADS * GLA_DK, GLA_HEADS * GLA_DK, GLA_W, GLA_RANK, GLA_W,
               RET_W, RET_W, RET_W, RET_W, ML_W, ML_W, ML_W, ML_HEADS, ML_HEADS)
IN_W = sum(SPLIT_SIZES)

kernel_name = 'hybrid_gla_retnet_mlstm_stream_step'


def _rmsnorm(x, g):
    xf = x.astype(jnp.float32)
    r = lax.rsqrt(jnp.mean(xf * xf, axis=-1, keepdims=True) + EPS)
    return (xf * r * g.astype(jnp.float32)).astype(x.dtype)


def _head_rmsnorm(h, g):
    r = lax.rsqrt(jnp.mean(h * h, axis=-1, keepdims=True) + EPS)
    B, L, H, d = h.shape
    return (h * r).reshape(B, L, H * d) * g.astype(jnp.float32)


def _head_layernorm(h, g):
    mu = jnp.mean(h, axis=-1, keepdims=True)
    hc = h - mu
    r = lax.rsqrt(jnp.mean(hc * hc, axis=-1, keepdims=True) + EPS)
    B, L, H, d = h.shape
    return (hc * r).reshape(B, L, H * d) * g.astype(jnp.float32)


def _rotary(x, pos):
    d = x.shape[-1]
    inv = ROPE_BASE ** (-jnp.arange(0, d, 2, dtype=jnp.float32) / d)
    ang = pos.astype(jnp.float32)[:, None] * inv[None, :]
    cos = jnp.cos(ang)[None, :, None, :]
    sin = jnp.sin(ang)[None, :, None, :]
    x1, x2 = x[..., : d // 2], x[..., d // 2:]
    return jnp.concatenate([x1 * cos - x2 * sin, x1 * sin + x2 * cos], axis=-1)


def _gla_step(S, inp):
    q, k, v, la = inp
    L = q.shape[1]
    b = jnp.cumsum(la, axis=1)
    causal = jnp.tril(jnp.ones((L, L), bool))[None, :, :, None, None]
    decay = jnp.exp(jnp.where(causal, b[:, :, None] - b[:, None, :], -jnp.inf))
    a = jnp.einsum('bthk,bshk,btshk->bths', q, k, decay)
    o = jnp.einsum('bthk,bhkv->bthv', q * jnp.exp(b), S) + jnp.einsum('bths,bshv->bthv', a, v)
    b_l = b[:, -1]
    S_new = jnp.exp(b_l)[..., None] * S + jnp.einsum('bshk,bshv->bhkv', k * jnp.exp(b_l[:, None] - b), v)
    return S_new, o


def _ret_step(S, inp):
    q, k, v = inp
    L = q.shape[1]
    lg = jnp.log1p(-jnp.exp2(-5.0 - jnp.arange(RET_HEADS, dtype=jnp.float32)))
    t = jnp.arange(L, dtype=jnp.float32)
    diff = t[:, None] - t[None, :]
    decay = jnp.where((diff >= 0)[..., None], jnp.exp(jnp.maximum(diff, 0.0)[..., None] * lg), 0.0)
    a = jnp.einsum('bthd,bshd->btsh', q, k) * decay[None]
    inter = jnp.exp((t[:, None] + 1.0) * lg)[None, :, :, None]
    o = jnp.einsum('btsh,bshe->bthe', a, v) + inter * jnp.einsum('bthd,bhde->bthe', q, S)
    w_end = jnp.exp((L - 1.0 - t)[:, None] * lg)
    S_new = jnp.exp(L * lg)[None, :, None, None] * S + jnp.einsum('sh,bshd,bshe->bhde', w_end, k, v)
    return S_new, o


def _mlstm_step(carry, inp):
    C, n, m = carry
    q, k, v, ig, lf = inp
    L = q.shape[1]
    b = jnp.cumsum(lf, axis=1)
    causal = jnp.tril(jnp.ones((L, L), bool))[None, :, :, None]
    log_d = jnp.where(causal, b[:, :, None, :] - b[:, None, :, :] + ig[:, None, :, :], -jnp.inf)
    log_0 = b + m[:, None, :]
    m_t = jnp.maximum(log_0, jnp.max(log_d, axis=2))
    w_d = jnp.exp(log_d - m_t[:, :, None, :])
    w_0 = jnp.exp(log_0 - m_t)
    a = jnp.einsum('bthd,bshd->btsh', q, k) * w_d
    num = w_0[..., None] * jnp.einsum('bthd,bhde->bthe', q, C) + jnp.einsum('btsh,bshe->bthe', a, v)
    den = w_0 * jnp.einsum('bthd,bhd->bth', q, n) + jnp.sum(a, axis=2)
    h = num / jnp.maximum(jnp.abs(den), jnp.exp(-m_t))[..., None]
    w_l = w_d[:, -1]
    w_0l = w_0[:, -1]
    C_new = w_0l[..., None, None] * C + jnp.einsum('bsh,bshd,bshe->bhde', w_l, k, v)
    n_new = w_0l[..., None] * n + jnp.einsum('bsh,bshd->bhd', w_l, k)
    return (C_new, n_new, m_t[:, -1]), h


def _run_chunked(step, carry, xs):
    B, L = xs[0].shape[0], xs[0].shape[1]
    if L <= CHUNK:
        return step(carry, xs)
    nc = L // CHUNK
    ch = tuple(jnp.moveaxis(a.reshape((B, nc, CHUNK) + a.shape[2:]), 1, 0) for a in xs)
    carry, ys = lax.scan(step, carry, ch)
    ys = jnp.moveaxis(ys, 0, 1)
    return carry, ys.reshape((B, L) + ys.shape[3:])


def _layer(x, pos, s_gla, s_ret, s_c, s_n, s_m, conv_buf,
           norm_mix, w_in, gla_w_a2, gla_b_a, gla_norm, ret_norm, ml_conv_w, ml_conv_b,
           ml_wq, ml_wk, ml_b_if, ml_norm, w_out, norm_ffn, w_gate, w_up, w_down):
    f32 = jnp.float32
    B, L, _ = x.shape
    h = _rmsnorm(x, norm_mix)
    z = h @ w_in
    idx = np.cumsum(SPLIT_SIZES)[:-1].tolist()
    gq, gk, gv, ga, gg, rq, rk, rv, rg, mu, mv, mo, mi, mf = jnp.split(z, idx, axis=-1)

    q = gq.reshape(B, L, GLA_HEADS, GLA_DK).astype(f32) * (GLA_DK ** -0.5)
    k = gk.reshape(B, L, GLA_HEADS, GLA_DK).astype(f32)
    v = gv.reshape(B, L, GLA_HEADS, GLA_DV).astype(f32)
    la = jax.nn.log_sigmoid(ga.astype(f32) @ gla_w_a2.astype(f32) + gla_b_a.astype(f32)) / GLA_TAU
    la = la.reshape(B, L, GLA_HEADS, GLA_DK)
    s_gla_new, o = _run_chunked(_gla_step, s_gla.astype(f32), (q, k, v, la))
    gla_out = _head_rmsnorm(o, gla_norm) * jax.nn.silu(gg.astype(f32))

    q = _rotary(rq.reshape(B, L, RET_HEADS, RET_D).astype(f32), pos)
    k = _rotary(rk.reshape(B, L, RET_HEADS, RET_D).astype(f32), pos) * (RET_D ** -0.5)
    v = rv.reshape(B, L, RET_HEADS, RET_D).astype(f32)
    s_ret_new, o = _run_chunked(_ret_step, s_ret.astype(f32), (q, k, v))
    ret_out = _head_layernorm(o, ret_norm) * jax.nn.silu(rg.astype(f32))

    xp = jnp.concatenate([conv_buf.astype(mu.dtype), mu], axis=1)
    c = ml_conv_b.astype(f32) + sum(xp[:, j:j + L].astype(f32) * ml_conv_w[j].astype(f32) for j in range(CONV_W))
    c = jax.nn.silu(c).reshape(B, L, ML_HEADS, ML_D)
    new_buf = xp[:, xp.shape[1] - (CONV_W - 1):]
    q = jnp.einsum('blhd,hde->blhe', c, ml_wq.astype(f32))
    k = jnp.einsum('blhd,hde->blhe', c, ml_wk.astype(f32)) * (ML_D ** -0.5)
    v = mv.reshape(B, L, ML_HEADS, ML_D).astype(f32)
    gif = jnp.concatenate([mi, mf], axis=-1).astype(f32) + ml_b_if.astype(f32)
    ig = gif[..., :ML_HEADS]
    lf = jax.nn.log_sigmoid(gif[..., ML_HEADS:])
    carry0 = (s_c.astype(f32), s_n.astype(f32), s_m.astype(f32))
    (c_new, n_new, m_new), o = _run_chunked(_mlstm_step, carry0, (q, k, v, ig, lf))
    o = o * jax.nn.sigmoid(mo.reshape(B, L, ML_HEADS, ML_D).astype(f32))
    ml_out = _head_layernorm(o, ml_norm)

    mixed = jnp.concatenate([gla_out, ret_out, ml_out], axis=-1).astype(x.dtype) @ w_out
    x = x + mixed
    hf = _rmsnorm(x, norm_ffn)
    x = x + (jax.nn.silu(hf @ w_gate) * (hf @ w_up)) @ w_down
    return x, (s_gla_new, s_ret_new, c_new, n_new, m_new, new_buf)


def setup_inputs(seed: int = 0) -> dict:
    key = jax.random.key(seed)
    ks = jax.random.split(key, 32)
    nrm = lambda k, s, sc: jax.random.normal(k, s, jnp.float32) * sc
    f_bias = jnp.linspace(3.0, 6.0, ML_HEADS, dtype=jnp.float32)
    b_if = jnp.concatenate([nrm(ks[20], (DEPTH, ML_HEADS), 0.1),
                            f_bias[None, :] + nrm(ks[21], (DEPTH, ML_HEADS), 0.1)], axis=-1)
    return {
        'x_prompt': nrm(ks[0], (BATCH, SEQ, D_MODEL), 1.0),
        'x_sample': nrm(ks[1], (DEC_BATCH, DEC_SEQ, D_MODEL), 1.0),
        'state_gla': nrm(ks[2], (DEPTH, DEC_BATCH, GLA_HEADS, GLA_DK, GLA_DV), 1.0),
        'state_ret': nrm(ks[3], (DEPTH, DEC_BATCH, RET_HEADS, RET_D, RET_D), 1.0),
        'state_mlstm_c': nrm(ks[4], (DEPTH, DEC_BATCH, ML_HEADS, ML_D, ML_D), 0.3),
        'state_mlstm_n': nrm(ks[5], (DEPTH, DEC_BATCH, ML_HEADS, ML_D), 0.3),
        'state_mlstm_m': nrm(ks[6], (DEPTH, DEC_BATCH, ML_HEADS), 1.0),
        'cache_mlstm_conv': nrm(ks[7], (DEPTH, DEC_BATCH, CONV_W - 1, ML_W), 1.0),
        'norm_mix': 1.0 + nrm(ks[8], (DEPTH, D_MODEL), 0.01),
        'w_in': nrm(ks[9], (DEPTH, D_MODEL, IN_W), D_MODEL ** -0.5),
        'gla_w_a2': nrm(ks[10], (DEPTH, GLA_RANK, GLA_HEADS * GLA_DK), GLA_RANK ** -0.5),
        'gla_b_a': nrm(ks[11], (DEPTH, GLA_HEADS * GLA_DK), 0.1),
        'gla_norm': 1.0 + nrm(ks[12], (DEPTH, GLA_W), 0.01),
        'ret_norm': 1.0 + nrm(ks[13], (DEPTH, RET_W), 0.01),
        'ml_conv_w': nrm(ks[14], (DEPTH, CONV_W, ML_W), CONV_W ** -0.5),
        'ml_conv_b': nrm(ks[15], (DEPTH, ML_W), 0.01),
        'ml_wq': nrm(ks[16], (DEPTH, ML_HEADS, ML_D, ML_D), ML_D ** -0.5),
        'ml_wk': nrm(ks[17], (DEPTH, ML_HEADS, ML_D, ML_D), ML_D ** -0.5),
        'ml_b_if': b_if,
        'ml_norm': 1.0 + nrm(ks[18], (DEPTH, ML_W), 0.01),
        'w_out': nrm(ks[19], (DEPTH, MIX_W, D_MODEL), MIX_W ** -0.5),
        'norm_ffn': 1.0 + nrm(ks[22], (DEPTH, D_MODEL), 0.01),
        'w_gate': nrm(ks[23], (DEPTH, D_MODEL, D_FF), D_MODEL ** -0.5),
        'w_up': nrm(ks[24], (DEPTH, D_MODEL, D_FF), D_MODEL ** -0.5),
        'w_down': nrm(ks[25], (DEPTH, D_FF, D_MODEL), D_FF ** -0.5),
        'norm_final': 1.0 + nrm(ks[26], (D_MODEL,), 0.01),
    }


def reference(x_prompt, x_sample, state_gla, state_ret, state_mlstm_c, state_mlstm_n, state_mlstm_m,
              cache_mlstm_conv, norm_mix, w_in, gla_w_a2, gla_b_a, gla_norm, ret_norm, ml_conv_w,
              ml_conv_b, ml_wq, ml_wk, ml_b_if, ml_norm, w_out, norm_ffn, w_gate, w_up, w_down, norm_final):
    f32 = jnp.float32
    B, S = x_prompt.shape[0], x_prompt.shape[1]
    pos_p = jnp.arange(S, dtype=jnp.int32)
    pos_s = PAST_LEN + jnp.arange(x_sample.shape[1], dtype=jnp.int32)
    xp, xs = x_prompt, x_sample
    new_p = [[] for _ in range(6)]
    new_s = [[] for _ in range(6)]
    for i in range(DEPTH):
        w = (norm_mix[i], w_in[i], gla_w_a2[i], gla_b_a[i], gla_norm[i], ret_norm[i], ml_conv_w[i],
             ml_conv_b[i], ml_wq[i], ml_wk[i], ml_b_if[i], ml_norm[i], w_out[i], norm_ffn[i],
             w_gate[i], w_up[i], w_down[i])
        xp, st_p = _layer(xp, pos_p,
                          jnp.zeros((B, GLA_HEADS, GLA_DK, GLA_DV), f32),
                          jnp.zeros((B, RET_HEADS, RET_D, RET_D), f32),
                          jnp.zeros((B, ML_HEADS, ML_D, ML_D), f32),
                          jnp.zeros((B, ML_HEADS, ML_D), f32),
                          jnp.zeros((B, ML_HEADS), f32),
                          jnp.zeros((B, CONV_W - 1, ML_W), xp.dtype), *w)
        xs, st_s = _layer(xs, pos_s, state_gla[i], state_ret[i], state_mlstm_c[i], state_mlstm_n[i],
                          state_mlstm_m[i], cache_mlstm_conv[i], *w)
        for j in range(6):
            new_p[j].append(st_p[j])
            new_s[j].append(st_s[j])
    y_prompt = _rmsnorm(xp, norm_final)
    y_sample = _rmsnorm(xs, norm_final)
    return (y_prompt, y_sample,
            jnp.stack(new_p[0]), jnp.stack(new_s[0]),
            jnp.stack(new_p[1]), jnp.stack(new_s[1]),
            jnp.stack(new_p[2]), jnp.stack(new_s[2]),
            jnp.stack(new_p[3]), jnp.stack(new_s[3]),
            jnp.stack(new_p[4]), jnp.stack(new_s[4]),
            jnp.stack(new_p[5]), jnp.stack(new_s[5]))
```

```python
import functools

import jax
import jax.numpy as jnp
from jax import lax
from jax.experimental import pallas as pl
from jax.experimental.pallas import tpu as pltpu

F32 = jnp.float32
BF16 = jnp.bfloat16

D_MODEL = 1024
EPS = 1e-6
CHUNK = 64
GLA_HEADS, GLA_DK, GLA_DV, GLA_RANK, GLA_TAU = 4, 32, 64, 16, 16.0
RET_HEADS, RET_D = 6, 64
ML_HEADS, ML_D = 6, 64
CONV_W = 4
ROPE_BASE = 10000.0
PAST_LEN = 2048
GLA_KW = GLA_HEADS * GLA_DK
GLA_W = GLA_HEADS * GLA_DV
RET_W = RET_HEADS * RET_D
ML_W = ML_HEADS * ML_D
D_FF = 2816
SPLIT_SIZES = (GLA_KW, GLA_KW, GLA_W, GLA_RANK, GLA_W, RET_W, RET_W, RET_W, RET_W, ML_W, ML_W, ML_W,
               ML_HEADS, ML_HEADS)

LANES = 128
Z_GQ, Z_GK, Z_GV, Z_GG = 0, 128, 256, 512
Z_RQ, Z_RK, Z_RV, Z_RG = 768, 1152, 1536, 1920
Z_MU, Z_MV, Z_MO = 2304, 2688, 3072
Z_SM = 3456
Z_W = 3584
SM_IG, SM_FG = GLA_RANK, GLA_RANK + ML_HEADS

TOKEN_BLOCK = 512
MIXER_BLOCK = 256
FF_CHUNK = 1408
VMEM_LIMIT = 56 * 1024 * 1024


def _dot(a, b):
    return jnp.dot(a.astype(BF16), b.astype(BF16), preferred_element_type=F32)


def _dot_nt(a, b):
    return lax.dot_general(a.astype(BF16), b.astype(BF16), (((1,), (1,)), ((), ())),
                           preferred_element_type=F32)


def _dot_tn(a, b):
    return lax.dot_general(a.astype(BF16), b.astype(BF16), (((0,), (0,)), ((), ())),
                           preferred_element_type=F32)


def _dot_f32(a, b):
    return jnp.dot(a, b, preferred_element_type=F32, precision=lax.Precision.HIGHEST)


def _log_sigmoid(x):
    return jnp.minimum(x, 0.0) - jnp.log1p(jnp.exp(-jnp.abs(x)))


def _silu(x):
    return x * jax.nn.sigmoid(x)


def _iota(shape, dim):
    return lax.broadcasted_iota(jnp.int32, shape, dim)


def _inproj_kernel(x_ref, g_ref, w_ref, z_ref):
    x = x_ref[...]
    r = lax.rsqrt(jnp.mean(x * x, axis=-1, keepdims=True) + EPS)
    h = (x * r * g_ref[...]).astype(BF16)
    z_ref[...] = jnp.dot(h, w_ref[...], preferred_element_type=F32)


def _inproj(x2d, norm3, w_in_p, layer):
    n = x2d.shape[0]
    tm = min(TOKEN_BLOCK, n)
    return pl.pallas_call(
        _inproj_kernel,
        grid=(n // tm,),
        in_specs=[
            pl.BlockSpec((tm, D_MODEL), lambda i: (i, 0)),
            pl.BlockSpec((None, 1, D_MODEL), lambda i: (layer, 0, 0)),
            pl.BlockSpec((None, D_MODEL, Z_W), lambda i: (layer, 0, 0), pipeline_mode=pl.Buffered(1)),
        ],
        out_specs=pl.BlockSpec((tm, Z_W), lambda i: (i, 0)),
        out_shape=jax.ShapeDtypeStruct((n, Z_W), F32),
        compiler_params=pltpu.CompilerParams(dimension_semantics=("parallel",),
                                             vmem_limit_bytes=VMEM_LIMIT),
        name="inproj",
    )(x2d, norm3, w_in_p)


def _mixer_kernel(*refs, block, has_init):
    it = iter(refs)
    z_ref, cos_ref, sin_ref, rtab_ref, rgam_ref = (next(it) for _ in range(5))
    (wa2_ref, ba_ref, bif_ref, gnorm_ref, rnorm_ref, mnorm_ref, cw_ref, cb_ref,
     wqk_ref) = (next(it) for _ in range(9))
    if has_init:
        g0_ref, r0_ref, c0_ref, n0_ref, m0_ref, cv0_ref = (next(it) for _ in range(6))
    mix_ref, gout_ref, rout_ref, cout_ref, nout_ref, mout_ref, cvout_ref = (next(it) for _ in range(7))
    gst_ref, sst_ref, cst_ref, nst_ref, mst_ref, xp_ref, cs_ref = (next(it) for _ in range(7))

    step = pl.program_id(1)
    last = pl.num_programs(1) - 1
    conv_lo = 8 - (CONV_W - 1)

    @pl.when(step == 0)
    def _init():
        gst_ref[...] = jnp.zeros_like(gst_ref)
        sst_ref[...] = jnp.zeros_like(sst_ref)
        cst_ref[...] = jnp.zeros_like(cst_ref)
        if has_init:
            for h in range(GLA_HEADS):
                gst_ref[h * GLA_DV:(h + 1) * GLA_DV, h * GLA_DK:(h + 1) * GLA_DK] = g0_ref[h]
            for h in range(RET_HEADS):
                sst_ref[h * RET_D:(h + 1) * RET_D, h * RET_D:(h + 1) * RET_D] = r0_ref[h]
                cst_ref[h * ML_D:(h + 1) * ML_D, h * ML_D:(h + 1) * ML_D] = c0_ref[h]
            nst_ref[...] = n0_ref[...]
            mst_ref[...] = m0_ref[...]
            xp_ref[conv_lo:8, :] = cv0_ref[...]
        else:
            nst_ref[...] = jnp.zeros_like(nst_ref)
            mst_ref[...] = jnp.zeros_like(mst_ref)
            xp_ref[conv_lo:8, :] = jnp.zeros((CONV_W - 1, ML_W), F32)

    xp_ref[8:8 + block, :] = z_ref[:, Z_MU:Z_MU + ML_W]
    cw = cw_ref[...]
    conv = cb_ref[...]
    for j in range(CONV_W):
        conv = conv + xp_ref[conv_lo + j:conv_lo + j + block, :] * cw[j:j + 1, :]
    cs_ref[...] = _silu(conv)
    conv_tail = xp_ref[block + conv_lo:block + 8, :]
    xp_ref[conv_lo:8, :] = conv_tail

    c64 = CHUNK
    tril = jnp.where(_iota((c64, c64), 1) <= _iota((c64, c64), 0), 1.0, 0.0).astype(F32)
    bm_gk = jnp.where((_iota((GLA_W, GLA_KW), 0) >> 6) == (_iota((GLA_W, GLA_KW), 1) >> 5), 1.0, 0.0)
    bm_gv = jnp.where((_iota((GLA_W, GLA_W), 0) >> 6) == (_iota((GLA_W, GLA_W), 1) >> 6), 1.0, 0.0)
    bm6 = jnp.where((_iota((RET_W, RET_W), 0) >> 6) == (_iota((RET_W, RET_W), 1) >> 6), 1.0, 0.0)
    bm_gk_b, bm_gv_b, bm6_b = bm_gk.astype(BF16), bm_gv.astype(BF16), bm6.astype(BF16)
    causal4 = (_iota((c64, GLA_W), 1) & 63) <= _iota((c64, GLA_W), 0)
    causal6 = (_iota((c64, ML_W), 1) & 63) <= _iota((c64, ML_W), 0)
    eye6 = (_iota((c64, ML_W), 1) & 63) == _iota((c64, ML_W), 0)
    ind_f = jnp.where(_iota((LANES, ML_W), 0) == (_iota((LANES, ML_W), 1) >> 6) + SM_FG, 1.0, 0.0).astype(F32)
    ind_i = jnp.where(_iota((LANES, ML_W), 0) == (_iota((LANES, ML_W), 1) >> 6) + SM_IG, 1.0, 0.0).astype(F32)
    lane_lo64 = (_iota((c64, LANES), 1) & 64) == 0
    lane_lo32 = (_iota((c64, LANES), 1) & 32) == 0
    neg_inf = jnp.float32(-jnp.inf)

    def rope(x, cos, sin):
        outs = []
        for p in range(RET_W // LANES):
            t = x[:, p * LANES:(p + 1) * LANES]
            sw = jnp.where(lane_lo32, pltpu.roll(t, LANES - 32, 1), pltpu.roll(t, 32, 1))
            outs.append(t * cos[:, p * LANES:(p + 1) * LANES] + sw * sin[:, p * LANES:(p + 1) * LANES])
        return jnp.concatenate(outs, axis=1)

    def tile_rows(x, n):
        return jnp.concatenate([x] * n, axis=0)

    def head_layernorm(o, gain):
        mu = _dot(o, bm6_b) * (1.0 / RET_D)
        oc = o - mu
        var = _dot(oc * oc, bm6_b) * (1.0 / RET_D)
        return oc * lax.rsqrt(var + EPS) * gain

    def chunk(ci, carry):
        rows = pl.ds(pl.multiple_of(ci * c64, c64), c64)
        small = z_ref[rows, Z_SM:Z_SM + LANES]

        xa = _dot(small, wa2_ref[...]) + ba_ref[...]
        la = _log_sigmoid(xa) * (1.0 / GLA_TAU)
        b = _dot_f32(tril, la)
        bl = b[c64 - 1:c64, :]
        kraw = z_ref[rows, Z_GK:Z_GK + GLA_KW]
        q1 = z_ref[rows, Z_GQ:Z_GQ + GLA_KW] * (GLA_DK ** -0.5) * jnp.exp(b)
        k1 = kraw * jnp.exp(-b)
        kl = kraw * jnp.exp(bl - b)
        gv = z_ref[rows, Z_GV:Z_GV + GLA_W]
        k_bd = tile_rows(k1.astype(BF16), GLA_HEADS) * bm_gk_b
        v_bd = tile_rows(gv.astype(BF16), GLA_HEADS) * bm_gv_b
        a = jnp.where(causal4, _dot_nt(q1, k_bd), 0.0)
        o = _dot(a, v_bd) + _dot_nt(q1, gst_ref[...])
        gst_ref[...] = gst_ref[...] * jnp.exp(bl) + bm_gk * _dot_tn(gv, kl)
        ms = _dot(o * o, bm_gv_b) * (1.0 / GLA_DV)
        mix_ref[rows, 0:GLA_W] = (o * lax.rsqrt(ms + EPS) * gnorm_ref[...]
                                  * _silu(z_ref[rows, Z_GG:Z_GG + GLA_W]))

        cos = cos_ref[rows, :]
        sin = sin_ref[rows, :]
        qr = rope(z_ref[rows, Z_RQ:Z_RQ + RET_W], cos, sin)
        kr = rope(z_ref[rows, Z_RK:Z_RK + RET_W], cos, sin) * (RET_D ** -0.5)
        rv = z_ref[rows, Z_RV:Z_RV + RET_W]
        k_bd = tile_rows(kr.astype(BF16), RET_HEADS) * bm6_b
        v_bd = tile_rows(rv.astype(BF16), RET_HEADS) * bm6_b
        a = _dot_nt(qr, k_bd) * rtab_ref[0]
        o = _dot(a, v_bd) + rtab_ref[1] * _dot(qr, sst_ref[...])
        sst_ref[...] = sst_ref[...] * rgam_ref[...] + bm6 * _dot_tn(kr * rtab_ref[2], rv)
        mix_ref[rows, GLA_W:GLA_W + RET_W] = (head_layernorm(o, rnorm_ref[...])
                                              * _silu(z_ref[rows, Z_RG:Z_RG + RET_W]))

        qk = _dot(cs_ref[rows, :], wqk_ref[...])
        q = qk[:, :ML_W]
        k = qk[:, ML_W:] * (ML_D ** -0.5)
        v = z_ref[rows, Z_MV:Z_MV + ML_W]
        gates = small + bif_ref[...]
        bt = _dot_f32(tril, _log_sigmoid(gates))
        bx = _dot_f32(bt, ind_f)
        ix = _dot_f32(gates, ind_i)
        row = jnp.sum(jnp.where(eye6, ix - bx, 0.0), axis=0, keepdims=True)
        logd = jnp.where(causal6, bx + row, neg_inf)
        log0 = bx + mst_ref[...]
        mparts = []
        for p in range(ML_W // LANES):
            t = logd[:, p * LANES:(p + 1) * LANES]
            lo = jnp.max(jnp.where(lane_lo64, t, neg_inf), axis=1, keepdims=True)
            hi = jnp.max(jnp.where(lane_lo64, neg_inf, t), axis=1, keepdims=True)
            mparts.append(jnp.where(lane_lo64, lo, hi))
        mx = jnp.maximum(log0, jnp.concatenate(mparts, axis=1))
        wd = jnp.exp(logd - mx)
        w0 = jnp.exp(log0 - mx)
        k_bd = tile_rows(k.astype(BF16), ML_HEADS) * bm6_b
        v_bd = tile_rows(v.astype(BF16), ML_HEADS) * bm6_b
        a = _dot_nt(q, k_bd) * wd
        num = w0 * _dot(q, cst_ref[...]) + _dot(a, v_bd)
        den = w0 * _dot(q * nst_ref[...], bm6_b) + _dot(a, bm6_b)
        hh = num / jnp.maximum(jnp.abs(den), jnp.exp(-mx))
        hh = hh * jax.nn.sigmoid(z_ref[rows, Z_MO:Z_MO + ML_W])
        mix_ref[rows, GLA_W + RET_W:D_MODEL] = head_layernorm(hh, mnorm_ref[...])
        m_last = mx[c64 - 1:c64, :]
        wl = jnp.exp(bx[c64 - 1:c64, :] - bx + ix - m_last)
        kw = k * wl
        w0l = w0[c64 - 1:c64, :]
        cst_ref[...] = cst_ref[...] * w0l + bm6 * _dot_tn(kw, v)
        nst_ref[...] = nst_ref[...] * w0l + jnp.sum(kw, axis=0, keepdims=True)
        mst_ref[...] = m_last
        return carry

    lax.fori_loop(0, block // c64, chunk, 0)

    @pl.when(step == last)
    def _finish():
        for h in range(GLA_HEADS):
            gout_ref[h] = gst_ref[h * GLA_DV:(h + 1) * GLA_DV, h * GLA_DK:(h + 1) * GLA_DK]
        for h in range(RET_HEADS):
            rout_ref[h] = sst_ref[h * RET_D:(h + 1) * RET_D, h * RET_D:(h + 1) * RET_D]
            cout_ref[h] = cst_ref[h * ML_D:(h + 1) * ML_D, h * ML_D:(h + 1) * ML_D]
        nout_ref[...] = nst_ref[...]
        mout_ref[...] = mst_ref[...]
        cvout_ref[...] = conv_tail


def _mixer(z3, tabs, wts, init, layer):
    bsz, seq, _ = z3.shape
    block = min(MIXER_BLOCK, seq)
    has_init = init is not None
    cos, sin, rtab, rgam = tabs

    def per_layer(shape):
        nd = len(shape)
        return pl.BlockSpec((None,) + shape, lambda b, s: (layer,) + (0,) * nd)

    def per_batch(shape, with_layer):
        nd = len(shape)
        if with_layer:
            return pl.BlockSpec((None, None) + shape, lambda b, s: (layer, b) + (0,) * nd)
        return pl.BlockSpec((None,) + shape, lambda b, s: (b,) + (0,) * nd)

    state_shapes = [(GLA_HEADS, GLA_DV, GLA_DK), (RET_HEADS, RET_D, RET_D), (ML_HEADS, ML_D, ML_D),
                    (1, ML_W), (1, ML_W), (CONV_W - 1, ML_W)]
    in_specs = [
        pl.BlockSpec((None, block, Z_W), lambda b, s: (b, s, 0)),
        pl.BlockSpec((block, RET_W), lambda b, s: (s, 0)),
        pl.BlockSpec((block, RET_W), lambda b, s: (s, 0)),
        pl.BlockSpec((3, CHUNK, RET_W), lambda b, s: (0, 0, 0)),
        pl.BlockSpec((1, RET_W), lambda b, s: (0, 0)),
        per_layer((LANES, GLA_KW)), per_layer((1, GLA_KW)), per_layer((1, LANES)),
        per_layer((1, GLA_W)), per_layer((1, RET_W)), per_layer((1, ML_W)),
        per_layer((CONV_W, ML_W)), per_layer((1, ML_W)), per_layer((ML_W, 2 * ML_W)),
    ]
    args = [z3, cos, sin, rtab, rgam] + list(wts)
    if has_init:
        in_specs += [per_batch(s, True) for s in state_shapes]
        args += list(init)
    out_specs = [pl.BlockSpec((None, block, D_MODEL), lambda b, s: (b, s, 0))]
    out_specs += [per_batch(s, False) for s in state_shapes]
    out_shape = [jax.ShapeDtypeStruct((bsz, seq, D_MODEL), F32)]
    out_shape += [jax.ShapeDtypeStruct((bsz,) + s, F32) for s in state_shapes]
    scratch = [
        pltpu.VMEM((GLA_W, GLA_KW), F32), pltpu.VMEM((RET_W, RET_W), F32), pltpu.VMEM((ML_W, ML_W), F32),
        pltpu.VMEM((1, ML_W), F32), pltpu.VMEM((1, ML_W), F32),
        pltpu.VMEM((block + 8, ML_W), F32), pltpu.VMEM((block, ML_W), F32),
    ]
    return pl.pallas_call(
        functools.partial(_mixer_kernel, block=block, has_init=has_init),
        grid=(bsz, seq // block),
        in_specs=in_specs,
        out_specs=out_specs,
        out_shape=out_shape,
        scratch_shapes=scratch,
        compiler_params=pltpu.CompilerParams(dimension_semantics=("parallel", "arbitrary"),
                                             vmem_limit_bytes=VMEM_LIMIT),
        name="mixer",
    )(*args)


def _ffn_kernel(*refs, final):
    if final:
        x_ref, mix_ref, wo_ref, g_ref, wg_ref, wu_ref, wd_ref, gf_ref, out_ref, y_ref = refs
    else:
        x_ref, mix_ref, wo_ref, g_ref, wg_ref, wu_ref, wd_ref, out_ref = refs
    x1 = x_ref[...] + jnp.dot(mix_ref[...].astype(BF16), wo_ref[...], preferred_element_type=F32)
    r = lax.rsqrt(jnp.mean(x1 * x1, axis=-1, keepdims=True) + EPS)
    hf = (x1 * r * g_ref[...]).astype(BF16)
    acc = x1
    for c in range(D_FF // FF_CHUNK):
        cols = slice(c * FF_CHUNK, (c + 1) * FF_CHUNK)
        gate = jnp.dot(hf, wg_ref[:, cols], preferred_element_type=F32)
        up = jnp.dot(hf, wu_ref[:, cols], preferred_element_type=F32)
        act = (_silu(gate) * up).astype(BF16)
        acc = acc + jnp.dot(act, wd_ref[cols, :], preferred_element_type=F32)
    out_ref[...] = acc
    if final:
        rf = lax.rsqrt(jnp.mean(acc * acc, axis=-1, keepdims=True) + EPS)
        y_ref[...] = acc * rf * gf_ref[...]


def _ffn(x2d, mix2d, wo, norm3, wg, wu, wd, layer, norm_final2):
    n = x2d.shape[0]
    tm = min(TOKEN_BLOCK, n)
    final = norm_final2 is not None

    def weight(shape):
        return pl.BlockSpec((None,) + shape, lambda i: (layer, 0, 0), pipeline_mode=pl.Buffered(1))

    row_spec = pl.BlockSpec((tm, D_MODEL), lambda i: (i, 0))
    in_specs = [row_spec, row_spec, weight((D_MODEL, D_MODEL)),
                pl.BlockSpec((None, 1, D_MODEL), lambda i: (layer, 0, 0)),
                weight((D_MODEL, D_FF)), weight((D_MODEL, D_FF)), weight((D_FF, D_MODEL))]
    args = [x2d, mix2d, wo, norm3, wg, wu, wd]
    out_specs = [row_spec]
    out_shape = [jax.ShapeDtypeStruct((n, D_MODEL), F32)]
    if final:
        in_specs.append(pl.BlockSpec((1, D_MODEL), lambda i: (0, 0)))
        args.append(norm_final2)
        out_specs.append(row_spec)
        out_shape.append(jax.ShapeDtypeStruct((n, D_MODEL), F32))
    return pl.pallas_call(
        functools.partial(_ffn_kernel, final=final),
        grid=(n // tm,),
        in_specs=in_specs,
        out_specs=out_specs,
        out_shape=out_shape,
        compiler_params=pltpu.CompilerParams(dimension_semantics=("parallel",),
                                             vmem_limit_bytes=VMEM_LIMIT),
        name="ffn",
    )(*args)


def _reorder_w_in(w_in):
    offs = [0]
    for sz in SPLIT_SIZES:
        offs.append(offs[-1] + sz)
    seg = [w_in[..., offs[i]:offs[i + 1]] for i in range(len(SPLIT_SIZES))]
    gq, gk, gv, ga, gg, rq, rk, rv, rg, mu, mv, mo, mi, mf = seg
    pad = jnp.zeros(w_in.shape[:-1] + (LANES - GLA_RANK - 2 * ML_HEADS,), w_in.dtype)
    return jnp.concatenate([gq, gk, gv, gg, rq, rk, rv, rg, mu, mv, mo, ga, mi, mf, pad],
                           axis=-1).astype(BF16)


def _block_diag_heads(w):
    depth, h, d, e = w.shape
    eye = jnp.eye(h, dtype=w.dtype)
    return jnp.einsum('lhde,hg->lhdge', w, eye).reshape(depth, h * d, h * e)


def _rope_tables(pos):
    inv = ROPE_BASE ** (-jnp.arange(0, RET_D, 2, dtype=F32) / RET_D)
    ang = pos.astype(F32)[:, None] * inv[None, :]
    cos, sin = jnp.cos(ang), jnp.sin(ang)
    cos_h = jnp.concatenate([cos, cos], axis=-1)
    sin_h = jnp.concatenate([-sin, sin], axis=-1)
    return jnp.tile(cos_h, (1, RET_HEADS)), jnp.tile(sin_h, (1, RET_HEADS))


def _retention_tables():
    lg = jnp.log1p(-jnp.exp2(-5.0 - jnp.arange(RET_HEADS, dtype=F32)))
    t = jnp.arange(CHUNK, dtype=F32)
    diff = t[:, None] - t[None, :]
    decay = jnp.where((diff >= 0)[..., None], jnp.exp(jnp.maximum(diff, 0.0)[..., None] * lg), 0.0)
    decay = jnp.transpose(decay, (0, 2, 1)).reshape(CHUNK, RET_W)
    inter = jnp.repeat(jnp.exp((t[:, None] + 1.0) * lg), RET_D, axis=1)
    w_end = jnp.repeat(jnp.exp((CHUNK - 1.0 - t)[:, None] * lg), RET_D, axis=1)
    gam = jnp.repeat(jnp.exp(CHUNK * lg), RET_D)[None, :]
    return jnp.stack([decay, inter, w_end]), gam


def kernel(x_prompt, x_sample, state_gla, state_ret, state_mlstm_c, state_mlstm_n, state_mlstm_m,
           cache_mlstm_conv, norm_mix, w_in, gla_w_a2, gla_b_a, gla_norm, ret_norm, ml_conv_w,
           ml_conv_b, ml_wq, ml_wk, ml_b_if, ml_norm, w_out, norm_ffn, w_gate, w_up, w_down, norm_final):
    depth = w_in.shape[0]
    assert x_prompt.shape[1] % CHUNK == 0 and x_sample.shape[1] % CHUNK == 0

    w_in_p = _reorder_w_in(w_in)
    wo, wg, wu, wd = (w.astype(BF16) for w in (w_out, w_gate, w_up, w_down))
    wa2 = jnp.zeros((depth, LANES, GLA_KW), F32).at[:, :GLA_RANK, :].set(gla_w_a2).astype(BF16)
    bif = jnp.zeros((depth, 1, LANES), F32).at[:, 0, SM_IG:SM_IG + 2 * ML_HEADS].set(ml_b_if)
    wqk = jnp.concatenate([_block_diag_heads(ml_wq), _block_diag_heads(ml_wk)], axis=-1).astype(BF16)
    row = lambda a: a[:, None, :]
    wts = (wa2, row(gla_b_a), bif, row(gla_norm), row(ret_norm), row(ml_norm), ml_conv_w, row(ml_conv_b), wqk)
    norm_mix3, norm_ffn3, norm_final2 = row(norm_mix), row(norm_ffn), norm_final[None, :]
    rtab, rgam = _retention_tables()

    init_sample = (jnp.swapaxes(state_gla, -1, -2), state_ret, state_mlstm_c,
                   state_mlstm_n.reshape(state_mlstm_n.shape[:2] + (1, ML_W)),
                   jnp.repeat(state_mlstm_m, ML_D, axis=-1)[:, :, None, :], cache_mlstm_conv)

    def run(x, pos, init):
        bsz, seq, _ = x.shape
        tabs = _rope_tables(pos) + (rtab, rgam)
        x2d = x.reshape(bsz * seq, D_MODEL)
        states = []
        y2d = None
        for i in range(depth):
            z = _inproj(x2d, norm_mix3, w_in_p, i)
            mixed, *st = _mixer(z.reshape(bsz, seq, Z_W), tabs, wts, init, i)
            outs = _ffn(x2d, mixed.reshape(bsz * seq, D_MODEL), wo, norm_ffn3, wg, wu, wd, i,
                        norm_final2 if i == depth - 1 else None)
            x2d = outs[0]
            if i == depth - 1:
                y2d = outs[1]
            states.append(st)
        g, r, c, n, m, cv = (jnp.stack([st[j] for st in states]) for j in range(6))
        return (y2d.reshape(bsz, seq, D_MODEL), jnp.swapaxes(g, -1, -2), r, c,
                n.reshape(n.shape[:2] + (ML_HEADS, ML_D)), m[:, :, 0, ::ML_D], cv)

    pos_p = jnp.arange(x_prompt.shape[1], dtype=jnp.int32)
    pos_s = PAST_LEN + jnp.arange(x_sample.shape[1], dtype=jnp.int32)
    yp, gp, rp, cp, np_, mp, cvp = run(x_prompt, pos_p, None)
    ys, gs, rs, cs, ns, ms, cvs = run(x_sample, pos_s, init_sample)
    return (yp, ys, gp, gs, rp, rs, cp, cs, np_, ns, mp, ms, cvp, cvs)
```

```python
import functools

import numpy as np
import jax
import jax.numpy as jnp
from jax import lax
from jax.experimental import pallas as pl
from jax.experimental.pallas import tpu as pltpu

F32 = jnp.float32
BF16 = jnp.bfloat16

D_MODEL = 1024
EPS = 1e-6
GLA_HEADS, GLA_DK, GLA_DV, GLA_RANK, GLA_TAU = 4, 32, 64, 16, 16.0
RET_HEADS, RET_D = 6, 64
ML_HEADS, ML_D = 6, 64
CONV_W = 4
ROPE_BASE = 10000.0
PAST_LEN = 2048
GLA_KW = GLA_HEADS * GLA_DK
GLA_W = GLA_HEADS * GLA_DV
RET_W = RET_HEADS * RET_D
ML_W = ML_HEADS * ML_D
D_FF = 2816
SPLIT_SIZES = (GLA_KW, GLA_KW, GLA_W, GLA_RANK, GLA_W, RET_W, RET_W, RET_W, RET_W, ML_W, ML_W, ML_W,
               ML_HEADS, ML_HEADS)

LANES = 128
PAIRS = RET_W // LANES
Z_GQ, Z_GK, Z_GV, Z_GG = 0, 128, 256, 512
Z_RQ, Z_RK, Z_RV, Z_RG = 768, 1152, 1536, 1920
Z_MU, Z_MV, Z_MO = 2304, 2688, 3072
Z_SM = 3456
Z_W = 3584
SM_IG, SM_FG = GLA_RANK, GLA_RANK + ML_HEADS
MIX_RET, MIX_ML = GLA_W, GLA_W + RET_W

TOKEN_BLOCK = 512
MIXER_BLOCK = 256
MIXER_CHUNK = 128
FF_CHUNK = 1408
VMEM_LIMIT = 56 * 1024 * 1024

_NT = (((1,), (1,)), ((), ()))


def _dot(a, b):
    return jnp.dot(a.astype(BF16), b.astype(BF16), preferred_element_type=F32)


def _dot_nt(a, b):
    return lax.dot_general(a.astype(BF16), b.astype(BF16), _NT, preferred_element_type=F32)


def _split3(x):
    p1 = x.astype(BF16)
    r1 = x - p1.astype(F32)
    p2 = r1.astype(BF16)
    p3 = (r1 - p2.astype(F32)).astype(BF16)
    return [p1, p2, p3]


def _sum3(x):
    w = x.shape[1] // 3
    return x[:, :w] + x[:, w:2 * w] + x[:, 2 * w:]


def _log_sigmoid(x):
    return jnp.minimum(x, 0.0) - jnp.log1p(jnp.exp(-jnp.abs(x)))


def _silu(x):
    return x * jax.nn.sigmoid(x)


def _iota(shape, dim):
    return lax.broadcasted_iota(jnp.int32, shape, dim)


def _inproj_kernel(x_ref, g_ref, w_ref, z_ref):
    x = x_ref[...]
    r = lax.rsqrt(jnp.mean(x * x, axis=-1, keepdims=True) + EPS)
    h = (x * r * g_ref[...]).astype(BF16)
    z_ref[...] = jnp.dot(h, w_ref[...], preferred_element_type=F32)


def _inproj(x2d, norm3, w_in_p, layer):
    n = x2d.shape[0]
    tm = min(TOKEN_BLOCK, n)
    return pl.pallas_call(
        _inproj_kernel,
        grid=(n // tm,),
        in_specs=[
            pl.BlockSpec((tm, D_MODEL), lambda i: (i, 0)),
            pl.BlockSpec((None, 1, D_MODEL), lambda i: (layer, 0, 0)),
            pl.BlockSpec((None, D_MODEL, Z_W), lambda i: (layer, 0, 0), pipeline_mode=pl.Buffered(1)),
        ],
        out_specs=pl.BlockSpec((tm, Z_W), lambda i: (i, 0)),
        out_shape=jax.ShapeDtypeStruct((n, Z_W), F32),
        compiler_params=pltpu.CompilerParams(dimension_semantics=("parallel",),
                                             vmem_limit_bytes=VMEM_LIMIT),
        name="inproj",
    )(x2d, norm3, w_in_p)


def _mixer_kernel(*refs, block, chunk, has_init):
    it = iter(refs)
    (z_ref, cos_ref, sin_ref, rdec_ref, rint_ref, rwend_ref, rgam_ref, tril_ref, ind_ref,
     on256_ref, on384_ref) = (next(it) for _ in range(11))
    (wa2_ref, ba_ref, bif_ref, gnorm_ref, rnorm_ref, mnorm_ref, cw_ref, cb_ref,
     wqk_ref) = (next(it) for _ in range(9))
    if has_init:
        g0_ref, r0_ref, c0_ref, n0_ref, m0_ref, cv0_ref = (next(it) for _ in range(6))
    mix_ref, gout_ref, rout_ref, cout_ref, nout_ref, mout_ref, cvout_ref = (next(it) for _ in range(7))
    (gst_ref, sst_ref, cst_ref, mst_ref, xp_ref, qm_ref, km_ref, qr_ref, kr_ref, bg_ref, bx_ref,
     dg_ref) = (next(it) for _ in range(12))

    step = pl.program_id(1)
    last = pl.num_programs(1) - 1
    conv_lo = 8 - (CONV_W - 1)
    c = chunk
    hd = RET_D

    @pl.when(step == 0)
    def _init():
        gst_ref[...] = jnp.zeros_like(gst_ref)
        sst_ref[...] = jnp.zeros_like(sst_ref)
        cst_ref[...] = jnp.zeros_like(cst_ref)
        if has_init:
            for h in range(GLA_HEADS):
                gst_ref[h * GLA_DV:(h + 1) * GLA_DV, h * GLA_DK:(h + 1) * GLA_DK] = g0_ref[h]
            for h in range(RET_HEADS):
                p, a = divmod(h, 2)
                sst_ref[p, a * hd:(a + 1) * hd, a * hd:(a + 1) * hd] = r0_ref[h]
                cst_ref[p, a * hd:(a + 1) * hd, a * hd:(a + 1) * hd] = c0_ref[h]
                cst_ref[p, a * hd:(a + 1) * hd, LANES + a * hd:LANES + (a + 1) * hd] = n0_ref[h]
            mst_ref[...] = m0_ref[...]
            xp_ref[conv_lo:8, :] = cv0_ref[...]
        else:
            mst_ref[...] = jnp.zeros_like(mst_ref)
            xp_ref[conv_lo:8, :] = jnp.zeros((CONV_W - 1, ML_W), F32)

    lane_c = _iota((c, LANES), 1)
    lane_lo = (lane_c & hd) == 0
    lane_lo32 = (lane_c & 32) == 0
    pair_mask = [jnp.where(lane_lo, 1.0, 0.0).astype(BF16), jnp.where(lane_lo, 0.0, 1.0).astype(BF16)]
    gla_mask = [jnp.where((lane_c >> 5) == h, 1.0, 0.0).astype(BF16) for h in range(GLA_HEADS)]
    ones_b = jnp.ones((c, LANES), BF16)
    neg_inf = jnp.float32(-jnp.inf)

    def rope(x, cos, sin):
        lo32 = (_iota((block, LANES), 1) & 32) == 0
        outs = []
        for p in range(PAIRS):
            t = x[:, p * LANES:(p + 1) * LANES]
            sw = jnp.where(lo32, pltpu.roll(t, LANES - 32, 1), pltpu.roll(t, 32, 1))
            outs.append(t * cos[:, p * LANES:(p + 1) * LANES] + sw * sin[:, p * LANES:(p + 1) * LANES])
        return jnp.concatenate(outs, axis=1)

    xp_ref[8:8 + block, :] = z_ref[:, Z_MU:Z_MU + ML_W]
    cw = cw_ref[...]
    conv = cb_ref[...]
    for j in range(CONV_W):
        conv = conv + xp_ref[conv_lo + j:conv_lo + j + block, :] * cw[j:j + 1, :]
    conv_tail = xp_ref[block + conv_lo:block + 8, :]
    xp_ref[conv_lo:8, :] = conv_tail
    qk = _dot(_silu(conv), wqk_ref[...])
    qm_ref[...] = qk[:, :ML_W]
    km_ref[...] = qk[:, ML_W:] * (ML_D ** -0.5)

    small = z_ref[:, Z_SM:Z_SM + LANES]
    la = _log_sigmoid(_dot(small, wa2_ref[...]) + ba_ref[...]) * (1.0 / GLA_TAU)
    gates = small + bif_ref[...]
    lf = _log_sigmoid(gates)
    cum = jnp.dot(tril_ref[...], jnp.concatenate(_split3(la) + _split3(lf), axis=1),
                  preferred_element_type=F32)
    bg_ref[...] = _sum3(cum[:, :3 * LANES])
    bt = _sum3(cum[:, 3 * LANES:])
    bx_ref[...] = jnp.dot(jnp.concatenate(_split3(bt), axis=1), ind_ref[...],
                          preferred_element_type=F32)
    dg_ref[...] = gates - pltpu.roll(bt, LANES - ML_HEADS, 1)

    cos = cos_ref[...]
    sin = sin_ref[...]
    qr_ref[...] = rope(z_ref[:, Z_RQ:Z_RQ + RET_W], cos, sin)
    kr_ref[...] = rope(z_ref[:, Z_RK:Z_RK + RET_W], cos, sin) * (RET_D ** -0.5)

    causal = _iota((c, c), 1) <= _iota((c, c), 0)
    causal4 = _iota((GLA_HEADS * c, c), 1) <= (_iota((GLA_HEADS * c, c), 0) & (c - 1))
    bm_gk = jnp.where((_iota((GLA_W, GLA_KW), 0) >> 6) == (_iota((GLA_W, GLA_KW), 1) >> 5), 1.0, 0.0)
    bm_pair = jnp.where((_iota((LANES, LANES), 0) >> 6) == (_iota((LANES, LANES), 1) >> 6), 1.0, 0.0)
    bm_aug = jnp.concatenate([bm_pair, bm_pair], axis=1)
    sub_lo = _iota((LANES, c), 0) < hd

    def stack_pair(xb):
        return jnp.concatenate([xb * pair_mask[0], xb * pair_mask[1]], axis=0)

    def chunk_body(ci, carry):
        rows = pl.ds(pl.multiple_of(ci * c, c), c)

        b = bg_ref[rows, :]
        bmid = b[c // 2 - 1:c // 2, :]
        bl = b[c - 1:c, :]
        gq = z_ref[rows, Z_GQ:Z_GQ + GLA_KW] * (GLA_DK ** -0.5)
        gk = z_ref[rows, Z_GK:Z_GK + GLA_KW]
        qh = (gq * jnp.exp(b - bmid)).astype(BF16)
        kh = (gk * jnp.exp(bmid - b)).astype(BF16)
        qe = gq * jnp.exp(b)
        kl = gk * jnp.exp(bl - b)
        lhs = jnp.concatenate([qh * gla_mask[h] for h in range(GLA_HEADS)], axis=0)
        a = lax.dot_general(lhs, kh, _NT, preferred_element_type=F32)
        a = jnp.where(causal4, a, 0.0).astype(BF16)
        gv = z_ref[rows, Z_GV:Z_GV + GLA_W]
        for vp in range(GLA_W // LANES):
            gvp = gv[:, vp * LANES:(vp + 1) * LANES].astype(BF16)
            o = (_dot(a[(2 * vp) * c:(2 * vp + 1) * c], gvp * pair_mask[0])
                 + _dot(a[(2 * vp + 1) * c:(2 * vp + 2) * c], gvp * pair_mask[1])
                 + _dot_nt(qe, gst_ref[vp * LANES:(vp + 1) * LANES, :]))
            mix_ref[rows, vp * LANES:(vp + 1) * LANES] = o
        gv_t = jnp.concatenate([gv[:, :LANES].T, gv[:, LANES:].T], axis=0)
        gst_ref[...] = gst_ref[...] * jnp.exp(bl) + bm_gk * _dot(gv_t, kl)

        for p in range(PAIRS):
            lanes = slice(p * LANES, (p + 1) * LANES)
            qp = qr_ref[rows, lanes]
            kp = kr_ref[rows, lanes]
            vb = z_ref[rows, Z_RV + p * LANES:Z_RV + (p + 1) * LANES].astype(BF16)
            s2 = lax.dot_general(stack_pair(qp.astype(BF16)), kp.astype(BF16), _NT,
                                 preferred_element_type=F32)
            pm = (s2 * rdec_ref[p]).astype(BF16)
            o = (_dot(pm[:c], vb * pair_mask[0]) + _dot(pm[c:], vb * pair_mask[1])
                 + _dot(qp * rint_ref[:, lanes], sst_ref[p]))
            mix_ref[rows, MIX_RET + p * LANES:MIX_RET + (p + 1) * LANES] = o
            sst_ref[p] = sst_ref[p] * rgam_ref[:, lanes] + bm_pair * _dot(kp.T * rwend_ref[p], vb)

        dg_t = dg_ref[rows, :].T
        for p in range(PAIRS):
            lanes = slice(p * LANES, (p + 1) * LANES)
            wd, w0, mt = [], [], []
            for a_ in range(2):
                h = 2 * p + a_
                colr = bx_ref[rows, h * LANES:(h + 1) * LANES]
                logd = jnp.where(causal, colr[:, :c] + dg_t[SM_IG + h:SM_IG + h + 1, :], neg_inf)
                log0 = colr + mst_ref[:, h * LANES:(h + 1) * LANES]
                m_t = jnp.maximum(log0, jnp.max(logd, axis=1, keepdims=True))
                wd.append(jnp.exp(logd - m_t[:, :c]))
                w0.append(jnp.exp(log0 - m_t))
                mt.append(m_t)
                mst_ref[:, h * LANES:(h + 1) * LANES] = m_t[c - 1:c, :]
            qp = qm_ref[rows, lanes]
            kp = km_ref[rows, lanes]
            vb = z_ref[rows, Z_MV + p * LANES:Z_MV + (p + 1) * LANES].astype(BF16)
            s2 = lax.dot_general(stack_pair(qp.astype(BF16)), kp.astype(BF16), _NT,
                                 preferred_element_type=F32)
            a0 = (s2[:c] * wd[0]).astype(BF16)
            a1 = (s2[c:] * wd[1]).astype(BF16)
            w0p = jnp.where(lane_lo, w0[0], w0[1])
            mtp = jnp.where(lane_lo, mt[0], mt[1])
            nd = (_dot(a0, jnp.concatenate([vb * pair_mask[0], pair_mask[0]], axis=1))
                  + _dot(a1, jnp.concatenate([vb * pair_mask[1], pair_mask[1]], axis=1))
                  + _dot(qp * w0p, cst_ref[p]))
            hh = nd[:, :LANES] / jnp.maximum(jnp.abs(nd[:, LANES:]), jnp.exp(-mtp))
            hh = hh * jax.nn.sigmoid(z_ref[rows, Z_MO + p * LANES:Z_MO + (p + 1) * LANES])
            mix_ref[rows, MIX_ML + p * LANES:MIX_ML + (p + 1) * LANES] = hh
            wl = jnp.where(sub_lo, wd[0][c - 1:c, :], wd[1][c - 1:c, :])
            upd = _dot(kp.T * wl, jnp.concatenate([vb, ones_b], axis=1))
            w0l = w0p[c - 1:c, :]
            cst_ref[p] = cst_ref[p] * jnp.concatenate([w0l, w0l], axis=1) + bm_aug * upd
        return carry

    lax.fori_loop(0, block // c, chunk_body, 0)

    def head_layernorm(o, ones_bd, gain):
        mu = _dot(o, ones_bd) * (1.0 / RET_D)
        oc = o - mu
        var = _dot(oc * oc, ones_bd) * (1.0 / RET_D)
        return oc * lax.rsqrt(var + EPS) * gain

    o = mix_ref[:, 0:GLA_W]
    ms = _dot(o * o, on256_ref[...]) * (1.0 / GLA_DV)
    mix_ref[:, 0:GLA_W] = o * lax.rsqrt(ms + EPS) * gnorm_ref[...] * _silu(z_ref[:, Z_GG:Z_GG + GLA_W])
    mix_ref[:, MIX_RET:MIX_ML] = (head_layernorm(mix_ref[:, MIX_RET:MIX_ML], on384_ref[...], rnorm_ref[...])
                                  * _silu(z_ref[:, Z_RG:Z_RG + RET_W]))
    mix_ref[:, MIX_ML:D_MODEL] = head_layernorm(mix_ref[:, MIX_ML:D_MODEL], on384_ref[...], mnorm_ref[...])

    @pl.when(step == last)
    def _finish():
        for h in range(GLA_HEADS):
            gout_ref[h] = gst_ref[h * GLA_DV:(h + 1) * GLA_DV, h * GLA_DK:(h + 1) * GLA_DK]
        for h in range(RET_HEADS):
            p, a = divmod(h, 2)
            rout_ref[h] = sst_ref[p, a * hd:(a + 1) * hd, a * hd:(a + 1) * hd]
            cout_ref[h] = cst_ref[p, a * hd:(a + 1) * hd, a * hd:(a + 1) * hd]
            nout_ref[h] = cst_ref[p, a * hd:(a + 1) * hd, LANES + a * hd:LANES + (a + 1) * hd]
        mout_ref[...] = mst_ref[...]
        cvout_ref[...] = conv_tail


def _mixer_constants(block, chunk):
    t = np.arange(block)
    tril = ((t[:, None] // chunk == t[None, :] // chunk) & (t[None, :] <= t[:, None])).astype(np.float32)
    r = np.arange(3 * LANES)[:, None] % LANES
    col = np.arange(ML_HEADS * LANES)[None, :] // LANES
    ind = (r == col + SM_FG).astype(np.float32)
    i256, i384 = np.arange(GLA_W) // GLA_DV, np.arange(RET_W) // RET_D
    on256 = (i256[:, None] == i256[None, :]).astype(np.float32)
    on384 = (i384[:, None] == i384[None, :]).astype(np.float32)
    return tuple(jnp.asarray(m, BF16) for m in (tril, ind, on256, on384))


def _mixer(z3, tabs, wts, init, layer):
    bsz, seq, _ = z3.shape
    block = min(MIXER_BLOCK, seq)
    chunk = min(MIXER_CHUNK, block)
    has_init = init is not None
    cos, sin, rdec, rint, rwend, rgam = tabs
    consts = _mixer_constants(block, chunk)

    def const(shape):
        nd = len(shape)
        return pl.BlockSpec(shape, lambda b, s: (0,) * nd)

    def per_layer(shape):
        nd = len(shape)
        return pl.BlockSpec((None,) + shape, lambda b, s: (layer,) + (0,) * nd)

    def per_batch(shape, with_layer):
        nd = len(shape)
        if with_layer:
            return pl.BlockSpec((None, None) + shape, lambda b, s: (layer, b) + (0,) * nd)
        return pl.BlockSpec((None,) + shape, lambda b, s: (b,) + (0,) * nd)

    state_shapes = [(GLA_HEADS, GLA_DV, GLA_DK), (RET_HEADS, RET_D, RET_D), (ML_HEADS, ML_D, ML_D),
                    (ML_HEADS, ML_D, ML_D), (1, ML_HEADS * LANES), (CONV_W - 1, ML_W)]
    in_specs = [
        pl.BlockSpec((None, block, Z_W), lambda b, s: (b, s, 0)),
        pl.BlockSpec((block, RET_W), lambda b, s: (s, 0)),
        pl.BlockSpec((block, RET_W), lambda b, s: (s, 0)),
        const(rdec.shape), const(rint.shape), const(rwend.shape), const(rgam.shape),
    ] + [const(m.shape) for m in consts] + [
        per_layer((LANES, GLA_KW)), per_layer((1, GLA_KW)), per_layer((1, LANES)),
        per_layer((1, GLA_W)), per_layer((1, RET_W)), per_layer((1, ML_W)),
        per_layer((CONV_W, ML_W)), per_layer((1, ML_W)), per_layer((ML_W, 2 * ML_W)),
    ]
    args = [z3, cos, sin, rdec, rint, rwend, rgam] + list(consts) + list(wts)
    if has_init:
        in_specs += [per_batch(s, True) for s in state_shapes]
        args += list(init)
    out_specs = [pl.BlockSpec((None, block, D_MODEL), lambda b, s: (b, s, 0))]
    out_specs += [per_batch(s, False) for s in state_shapes]
    out_shape = [jax.ShapeDtypeStruct((bsz, seq, D_MODEL), F32)]
    out_shape += [jax.ShapeDtypeStruct((bsz,) + s, F32) for s in state_shapes]
    scratch = [
        pltpu.VMEM((GLA_W, GLA_KW), F32), pltpu.VMEM((PAIRS, LANES, LANES), F32),
        pltpu.VMEM((PAIRS, LANES, 2 * LANES), F32), pltpu.VMEM((1, ML_HEADS * LANES), F32),
        pltpu.VMEM((block + 8, ML_W), F32),
        pltpu.VMEM((block, ML_W), F32), pltpu.VMEM((block, ML_W), F32),
        pltpu.VMEM((block, RET_W), F32), pltpu.VMEM((block, RET_W), F32),
        pltpu.VMEM((block, LANES), F32), pltpu.VMEM((block, ML_HEADS * LANES), F32),
        pltpu.VMEM((block, LANES), F32),
    ]
    return pl.pallas_call(
        functools.partial(_mixer_kernel, block=block, chunk=chunk, has_init=has_init),
        grid=(bsz, seq // block),
        in_specs=in_specs,
        out_specs=out_specs,
        out_shape=out_shape,
        scratch_shapes=scratch,
        compiler_params=pltpu.CompilerParams(dimension_semantics=("parallel", "arbitrary"),
                                             vmem_limit_bytes=VMEM_LIMIT),
        name="mixer",
    )(*args)


def _ffn_kernel(*refs, final):
    if final:
        x_ref, mix_ref, wo_ref, g_ref, wg_ref, wu_ref, wd_ref, gf_ref, out_ref, y_ref = refs
    else:
        x_ref, mix_ref, wo_ref, g_ref, wg_ref, wu_ref, wd_ref, out_ref = refs
    x1 = x_ref[...] + jnp.dot(mix_ref[...].astype(BF16), wo_ref[...], preferred_element_type=F32)
    r = lax.rsqrt(jnp.mean(x1 * x1, axis=-1, keepdims=True) + EPS)
    hf = (x1 * r * g_ref[...]).astype(BF16)
    acc = x1
    for c in range(D_FF // FF_CHUNK):
        cols = slice(c * FF_CHUNK, (c + 1) * FF_CHUNK)
        gate = jnp.dot(hf, wg_ref[:, cols], preferred_element_type=F32)
        up = jnp.dot(hf, wu_ref[:, cols], preferred_element_type=F32)
        act = (_silu(gate) * up).astype(BF16)
        acc = acc + jnp.dot(act, wd_ref[cols, :], preferred_element_type=F32)
    out_ref[...] = acc
    if final:
        rf = lax.rsqrt(jnp.mean(acc * acc, axis=-1, keepdims=True) + EPS)
        y_ref[...] = acc * rf * gf_ref[...]


def _ffn(x2d, mix2d, wo, norm3, wg, wu, wd, layer, norm_final2):
    n = x2d.shape[0]
    tm = min(TOKEN_BLOCK, n)
    final = norm_final2 is not None

    def weight(shape):
        return pl.BlockSpec((None,) + shape, lambda i: (layer, 0, 0), pipeline_mode=pl.Buffered(1))

    row_spec = pl.BlockSpec((tm, D_MODEL), lambda i: (i, 0))
    in_specs = [row_spec, row_spec, weight((D_MODEL, D_MODEL)),
                pl.BlockSpec((None, 1, D_MODEL), lambda i: (layer, 0, 0)),
                weight((D_MODEL, D_FF)), weight((D_MODEL, D_FF)), weight((D_FF, D_MODEL))]
    args = [x2d, mix2d, wo, norm3, wg, wu, wd]
    out_specs = [row_spec]
    out_shape = [jax.ShapeDtypeStruct((n, D_MODEL), F32)]
    if final:
        in_specs.append(pl.BlockSpec((1, D_MODEL), lambda i: (0, 0)))
        args.append(norm_final2)
        out_specs.append(row_spec)
        out_shape.append(jax.ShapeDtypeStruct((n, D_MODEL), F32))
    return pl.pallas_call(
        functools.partial(_ffn_kernel, final=final),
        grid=(n // tm,),
        in_specs=in_specs,
        out_specs=out_specs,
        out_shape=out_shape,
        compiler_params=pltpu.CompilerParams(dimension_semantics=("parallel",),
                                             vmem_limit_bytes=VMEM_LIMIT),
        name="ffn",
    )(*args)


def _reorder_w_in(w_in):
    offs = [0]
    for sz in SPLIT_SIZES:
        offs.append(offs[-1] + sz)
    seg = [w_in[..., offs[i]:offs[i + 1]] for i in range(len(SPLIT_SIZES))]
    gq, gk, gv, ga, gg, rq, rk, rv, rg, mu, mv, mo, mi, mf = seg
    pad = jnp.zeros(w_in.shape[:-1] + (LANES - GLA_RANK - 2 * ML_HEADS,), w_in.dtype)
    return jnp.concatenate([gq, gk, gv, gg, rq, rk, rv, rg, mu, mv, mo, ga, mi, mf, pad],
                           axis=-1).astype(BF16)


def _block_diag_heads(w):
    depth, h, d, e = w.shape
    eye = jnp.eye(h, dtype=w.dtype)
    return jnp.einsum('lhde,hg->lhdge', w, eye).reshape(depth, h * d, h * e)


def _rope_tables(pos):
    inv = ROPE_BASE ** (-jnp.arange(0, RET_D, 2, dtype=F32) / RET_D)
    ang = pos.astype(F32)[:, None] * inv[None, :]
    cos, sin = jnp.cos(ang), jnp.sin(ang)
    cos_h = jnp.concatenate([cos, cos], axis=-1)
    sin_h = jnp.concatenate([-sin, sin], axis=-1)
    return jnp.tile(cos_h, (1, RET_HEADS)), jnp.tile(sin_h, (1, RET_HEADS))


def _retention_tables(chunk):
    lg = jnp.log1p(-jnp.exp2(-5.0 - jnp.arange(RET_HEADS, dtype=F32)))
    t = jnp.arange(chunk, dtype=F32)
    diff = t[:, None] - t[None, :]
    decay = jnp.where((diff >= 0)[..., None], jnp.exp(jnp.maximum(diff, 0.0)[..., None] * lg), 0.0)
    decay = jnp.transpose(decay, (2, 0, 1)).reshape(PAIRS, 2 * chunk, chunk)
    inter = jnp.repeat(jnp.exp((t[:, None] + 1.0) * lg), RET_D, axis=1)
    w_end = jnp.exp((chunk - 1.0 - t)[None, :] * lg[:, None])
    w_end = jnp.repeat(w_end, RET_D, axis=0).reshape(PAIRS, LANES, chunk)
    gam = jnp.repeat(jnp.exp(chunk * lg), RET_D)[None, :]
    return decay, inter, w_end, gam


def kernel(x_prompt, x_sample, state_gla, state_ret, state_mlstm_c, state_mlstm_n, state_mlstm_m,
           cache_mlstm_conv, norm_mix, w_in, gla_w_a2, gla_b_a, gla_norm, ret_norm, ml_conv_w,
           ml_conv_b, ml_wq, ml_wk, ml_b_if, ml_norm, w_out, norm_ffn, w_gate, w_up, w_down, norm_final):
    depth = w_in.shape[0]

    w_in_p = _reorder_w_in(w_in)
    wo, wg, wu, wd = (w.astype(BF16) for w in (w_out, w_gate, w_up, w_down))
    wa2 = jnp.zeros((depth, LANES, GLA_KW), F32).at[:, :GLA_RANK, :].set(gla_w_a2).astype(BF16)
    bif = jnp.zeros((depth, 1, LANES), F32).at[:, 0, SM_IG:SM_IG + 2 * ML_HEADS].set(ml_b_if)
    wqk = jnp.concatenate([_block_diag_heads(ml_wq), _block_diag_heads(ml_wk)], axis=-1).astype(BF16)
    row = lambda a: a[:, None, :]
    wts = (wa2, row(gla_b_a), bif, row(gla_norm), row(ret_norm), row(ml_norm), ml_conv_w, row(ml_conv_b), wqk)
    norm_mix3, norm_ffn3, norm_final2 = row(norm_mix), row(norm_ffn), norm_final[None, :]

    init_sample = (jnp.swapaxes(state_gla, -1, -2), state_ret, state_mlstm_c,
                   jnp.broadcast_to(state_mlstm_n[..., None], state_mlstm_n.shape + (ML_D,)),
                   jnp.repeat(state_mlstm_m, LANES, axis=-1)[:, :, None, :], cache_mlstm_conv)

    def run(x, pos, init):
        bsz, seq, _ = x.shape
        chunk = min(MIXER_CHUNK, MIXER_BLOCK, seq)
        assert seq % chunk == 0 and chunk & (chunk - 1) == 0 and chunk >= 2 * 8
        tabs = _rope_tables(pos) + _retention_tables(chunk)
        x2d = x.reshape(bsz * seq, D_MODEL)
        states = []
        y2d = None
        for i in range(depth):
            z = _inproj(x2d, norm_mix3, w_in_p, i)
            mixed, *st = _mixer(z.reshape(bsz, seq, Z_W), tabs, wts, init, i)
            outs = _ffn(x2d, mixed.reshape(bsz * seq, D_MODEL), wo, norm_ffn3, wg, wu, wd, i,
                        norm_final2 if i == depth - 1 else None)
            x2d = outs[0]
            if i == depth - 1:
                y2d = outs[1]
            states.append(st)
        g, r, c, n, m, cv = (jnp.stack([st[j] for st in states]) for j in range(6))
        return (y2d.reshape(bsz, seq, D_MODEL), jnp.swapaxes(g, -1, -2), r, c,
                n[..., 0], m[:, :, 0, ::LANES], cv)

    pos_p = jnp.arange(x_prompt.shape[1], dtype=jnp.int32)
    pos_s = PAST_LEN + jnp.arange(x_sample.shape[1], dtype=jnp.int32)
    yp, gp, rp, cp, np_, mp, cvp = run(x_prompt, pos_p, None)
    ys, gs, rs, cs, ns, ms, cvs = run(x_sample, pos_s, init_sample)
    return (yp, ys, gp, gs, rp, rs, cp, cs, np_, ns, mp, ms, cvp, cvs)
```

```python
import functools

import numpy as np
import jax
import jax.numpy as jnp
from jax import lax
from jax.experimental import pallas as pl
from jax.experimental.pallas import tpu as pltpu

F32 = jnp.float32
BF16 = jnp.bfloat16

D_MODEL = 1024
EPS = 1e-6
GLA_HEADS, GLA_DK, GLA_DV, GLA_RANK, GLA_TAU = 4, 32, 64, 16, 16.0
RET_HEADS, RET_D = 6, 64
ML_HEADS, ML_D = 6, 64
CONV_W = 4
ROPE_BASE = 10000.0
PAST_LEN = 2048
GLA_KW = GLA_HEADS * GLA_DK
GLA_W = GLA_HEADS * GLA_DV
RET_W = RET_HEADS * RET_D
ML_W = ML_HEADS * ML_D
D_FF = 2816
SPLIT_SIZES = (GLA_KW, GLA_KW, GLA_W, GLA_RANK, GLA_W, RET_W, RET_W, RET_W, RET_W, ML_W, ML_W, ML_W,
               ML_HEADS, ML_HEADS)

LANES = 128
PAIRS = RET_W // LANES
Z_GQ, Z_GK, Z_GV, Z_GG = 0, 128, 256, 512
Z_RQ, Z_RK, Z_RV, Z_RG = 768, 1152, 1536, 1920
Z_MU, Z_MV, Z_MO = 2304, 2688, 3072
Z_SM = 3456
Z_W = 3584
SM_IG, SM_FG = GLA_RANK, GLA_RANK + ML_HEADS
MIX_RET, MIX_ML = GLA_W, GLA_W + RET_W

TOKEN_BLOCK = 512
MIXER_BLOCK = 512
MIXER_CHUNK = 128
FF_CHUNK = 1408
VMEM_LIMIT = 56 * 1024 * 1024

_NT = (((1,), (1,)), ((), ()))


def _dot(a, b):
    return jnp.dot(a.astype(BF16), b.astype(BF16), preferred_element_type=F32)


def _dot_nt(a, b):
    return lax.dot_general(a.astype(BF16), b.astype(BF16), _NT, preferred_element_type=F32)


def _split2(x):
    hi = x.astype(BF16)
    return [hi, (x - hi.astype(F32)).astype(BF16)]


def _log_sigmoid(x):
    return jnp.minimum(x, 0.0) - jnp.log(1.0 + jnp.exp(-jnp.abs(x)))


def _sigmoid(x):
    return 0.5 + 0.5 * jnp.tanh(0.5 * x)


def _silu(x):
    return x * _sigmoid(x)


def _iota(shape, dim):
    return lax.broadcasted_iota(jnp.int32, shape, dim)


def _inproj_kernel(x_ref, g_ref, w_ref, z_ref):
    x = x_ref[...]
    r = lax.rsqrt(jnp.mean(x * x, axis=-1, keepdims=True) + EPS)
    h = (x * r * g_ref[...]).astype(BF16)
    z_ref[...] = jnp.dot(h, w_ref[...], preferred_element_type=F32)


def _inproj(x2d, norm3, w_in_p, layer):
    n = x2d.shape[0]
    tm = min(TOKEN_BLOCK, n)
    return pl.pallas_call(
        _inproj_kernel,
        grid=(n // tm,),
        in_specs=[
            pl.BlockSpec((tm, D_MODEL), lambda i: (i, 0)),
            pl.BlockSpec((None, 1, D_MODEL), lambda i: (layer, 0, 0)),
            pl.BlockSpec((None, D_MODEL, Z_W), lambda i: (layer, 0, 0), pipeline_mode=pl.Buffered(1)),
        ],
        out_specs=pl.BlockSpec((tm, Z_W), lambda i: (i, 0)),
        out_shape=jax.ShapeDtypeStruct((n, Z_W), F32),
        compiler_params=pltpu.CompilerParams(dimension_semantics=("parallel",),
                                             vmem_limit_bytes=VMEM_LIMIT),
        name="inproj",
    )(x2d, norm3, w_in_p)


def _mixer_kernel(*refs, block, chunk, has_init):
    it = iter(refs)
    (z_ref, cos_ref, sin_ref, rdec_ref, rint_ref, rwend_ref, rgam_ref, tril_ref, ind_ref,
     on256_ref) = (next(it) for _ in range(10))
    (wa2_ref, ba_ref, bif_ref, gnorm_ref, rnorm_ref, mnorm_ref, cw_ref, cb_ref,
     wqk_ref) = (next(it) for _ in range(9))
    if has_init:
        g0_ref, r0_ref, c0_ref, n0_ref, m0_ref, cv0_ref = (next(it) for _ in range(6))
    mix_ref, gout_ref, rout_ref, cout_ref, nout_ref, mout_ref, cvout_ref = (next(it) for _ in range(7))
    (gst_ref, sst_ref, cst_ref, mst_ref, xp_ref, qm_ref, km_ref, qr_ref, kr_ref, bg_ref, bx_ref,
     dg_ref) = (next(it) for _ in range(12))

    step = pl.program_id(1)
    last = pl.num_programs(1) - 1
    conv_lo = 8 - (CONV_W - 1)
    c = chunk
    hd = RET_D

    @pl.when(step == 0)
    def _init():
        gst_ref[...] = jnp.zeros_like(gst_ref)
        sst_ref[...] = jnp.zeros_like(sst_ref)
        cst_ref[...] = jnp.zeros_like(cst_ref)
        if has_init:
            for h in range(GLA_HEADS):
                gst_ref[h * GLA_DK:(h + 1) * GLA_DK, h * GLA_DV:(h + 1) * GLA_DV] = g0_ref[h]
            for h in range(RET_HEADS):
                p, a = divmod(h, 2)
                sst_ref[p, a * hd:(a + 1) * hd, a * hd:(a + 1) * hd] = r0_ref[h]
                cst_ref[p, a * hd:(a + 1) * hd, a * hd:(a + 1) * hd] = c0_ref[h]
                cst_ref[p, a * hd:(a + 1) * hd, LANES + a * hd:LANES + (a + 1) * hd] = n0_ref[h]
            mst_ref[...] = m0_ref[...]
            xp_ref[conv_lo:8, :] = cv0_ref[...]
        else:
            mst_ref[...] = jnp.zeros_like(mst_ref)
            xp_ref[conv_lo:8, :] = jnp.zeros((CONV_W - 1, ML_W), F32)

    lane_c = _iota((c, LANES), 1)
    lane_lo = (lane_c & hd) == 0
    pair_mask = [jnp.where(lane_lo, 1.0, 0.0).astype(BF16), jnp.where(lane_lo, 0.0, 1.0).astype(BF16)]
    gla_mask = [jnp.where((lane_c >> 5) == h, 1.0, 0.0).astype(BF16) for h in range(GLA_HEADS)]
    gv_mask = [jnp.where((_iota((c, GLA_W), 1) >> 6) == h, 1.0, 0.0).astype(BF16) for h in range(GLA_HEADS)]
    ones_b = jnp.ones((c, LANES), BF16)
    neg_inf = jnp.float32(-jnp.inf)
    causal = _iota((c, c), 1) <= _iota((c, c), 0)
    causal4 = _iota((GLA_HEADS * c, c), 1) <= (_iota((GLA_HEADS * c, c), 0) & (c - 1))
    bm_gk = jnp.where((_iota((GLA_KW, GLA_W), 0) >> 5) == (_iota((GLA_KW, GLA_W), 1) >> 6), 1.0, 0.0)
    bm_pair = jnp.where((_iota((LANES, LANES), 0) >> 6) == (_iota((LANES, LANES), 1) >> 6), 1.0, 0.0)
    bm_aug = jnp.concatenate([bm_pair, bm_pair], axis=1)
    sub_lo = _iota((LANES, c), 0) < hd

    def stack_pair(xb):
        return jnp.concatenate([xb * pair_mask[0], xb * pair_mask[1]], axis=0)

    xp_ref[8:8 + block, :] = z_ref[:, Z_MU:Z_MU + ML_W]
    cw = cw_ref[...]
    conv = cb_ref[...]
    for j in range(CONV_W):
        conv = conv + xp_ref[conv_lo + j:conv_lo + j + block, :] * cw[j:j + 1, :]
    qk = _dot(_silu(conv), wqk_ref[...])
    qm_ref[...] = qk[:, :ML_W]
    km_ref[...] = qk[:, ML_W:] * (ML_D ** -0.5)

    small = z_ref[:, Z_SM:Z_SM + LANES]
    la = _log_sigmoid(_dot(small, wa2_ref[...]) + ba_ref[...]) * (1.0 / GLA_TAU)
    gates = small + bif_ref[...]
    lf = _log_sigmoid(gates)
    cum = jnp.dot(tril_ref[...], jnp.concatenate(_split2(la) + _split2(lf), axis=1),
                  preferred_element_type=F32)
    bg_ref[...] = cum[:, :LANES] + cum[:, LANES:2 * LANES]
    bt = cum[:, 2 * LANES:3 * LANES] + cum[:, 3 * LANES:]
    bxe = jnp.dot(jnp.concatenate(_split2(bt), axis=1), ind_ref[...], preferred_element_type=F32)
    bx_ref[...] = bxe[:, :ML_HEADS * LANES]
    dg_ref[...] = gates - bxe[:, ML_HEADS * LANES:]

    lo32 = (_iota((block, LANES), 1) & 32) == 0
    for p in range(PAIRS):
        lanes = slice(p * LANES, (p + 1) * LANES)
        cos = cos_ref[:, lanes]
        sin = sin_ref[:, lanes]
        for src, dst, scale in ((Z_RQ, qr_ref, 1.0), (Z_RK, kr_ref, RET_D ** -0.5)):
            t = z_ref[:, src + p * LANES:src + (p + 1) * LANES]
            sw = jnp.where(lo32, pltpu.roll(t, LANES - 32, 1), pltpu.roll(t, 32, 1))
            dst[:, lanes] = (t * cos + sw * sin) * scale

    for ci in range(block // c):
        rows = slice(ci * c, (ci + 1) * c)

        b = bg_ref[rows, :]
        bmid = b[c // 2 - 1:c // 2, :]
        bl = b[c - 1:c, :]
        gq = z_ref[rows, Z_GQ:Z_GQ + GLA_KW] * (GLA_DK ** -0.5)
        gk = z_ref[rows, Z_GK:Z_GK + GLA_KW]
        qh = (gq * jnp.exp(b - bmid)).astype(BF16)
        kh = (gk * jnp.exp(bmid - b)).astype(BF16)
        qe = (gq * jnp.exp(b)).astype(BF16)
        kl_t = (gk * jnp.exp(bl - b)).T
        lhs = jnp.concatenate([qh * gla_mask[h] for h in range(GLA_HEADS)], axis=0)
        a = lax.dot_general(lhs, kh, _NT, preferred_element_type=F32)
        a = jnp.where(causal4, a, 0.0).astype(BF16)
        gvb = z_ref[rows, Z_GV:Z_GV + GLA_W].astype(BF16)
        o = jnp.dot(jnp.concatenate([a[h * c:(h + 1) * c] for h in range(GLA_HEADS)] + [qe], axis=1),
                    jnp.concatenate([gvb * gv_mask[h] for h in range(GLA_HEADS)]
                                    + [gst_ref[...].astype(BF16)], axis=0),
                    preferred_element_type=F32)
        mix_ref[rows, 0:GLA_W] = o
        bl_col = b[c - 8:c, :].T[:, 7:8]
        gst_ref[...] = gst_ref[...] * jnp.exp(bl_col) + bm_gk * _dot(kl_t, gvb)

        rvb = z_ref[rows, Z_RV:Z_RV + RET_W].astype(BF16)
        kr = kr_ref[rows, :]
        for p in range(PAIRS):
            lanes = slice(p * LANES, (p + 1) * LANES)
            qp = qr_ref[rows, lanes]
            vb = rvb[:, lanes]
            s2 = lax.dot_general(stack_pair(qp.astype(BF16)), kr[:, lanes].astype(BF16), _NT,
                                 preferred_element_type=F32)
            pm = (s2 * rdec_ref[p]).astype(BF16)
            mix_ref[rows, MIX_RET + p * LANES:MIX_RET + (p + 1) * LANES] = jnp.dot(
                jnp.concatenate([pm[:c], pm[c:], (qp * rint_ref[:, lanes]).astype(BF16)], axis=1),
                jnp.concatenate([vb * pair_mask[0], vb * pair_mask[1], sst_ref[p].astype(BF16)], axis=0),
                preferred_element_type=F32)
        upd = _dot(kr.T * rwend_ref[...], rvb)
        for p in range(PAIRS):
            lanes = slice(p * LANES, (p + 1) * LANES)
            sst_ref[p] = sst_ref[p] * rgam_ref[:, lanes] + bm_pair * upd[lanes, lanes]

        dg_t = dg_ref[rows, :].T
        mvb = z_ref[rows, Z_MV:Z_MV + ML_W].astype(BF16)
        km = km_ref[rows, :]
        wls, w0ls = [], []
        for p in range(PAIRS):
            lanes = slice(p * LANES, (p + 1) * LANES)
            wd, w0, mt = [], [], []
            for a_ in range(2):
                h = 2 * p + a_
                colr = bx_ref[rows, h * LANES:(h + 1) * LANES]
                logd = jnp.where(causal, colr[:, :c] + dg_t[SM_IG + h:SM_IG + h + 1, :], neg_inf)
                log0 = colr + mst_ref[:, h * LANES:(h + 1) * LANES]
                m_t = jnp.maximum(log0, jnp.max(logd, axis=1, keepdims=True))
                wd.append(jnp.exp(logd - m_t[:, :c]))
                w0.append(jnp.exp(log0 - m_t))
                mt.append(m_t)
                mst_ref[:, h * LANES:(h + 1) * LANES] = m_t[c - 1:c, :]
            qp = qm_ref[rows, lanes]
            vb = mvb[:, lanes]
            s2 = lax.dot_general(stack_pair(qp.astype(BF16)), km[:, lanes].astype(BF16), _NT,
                                 preferred_element_type=F32)
            a0 = (s2[:c] * wd[0]).astype(BF16)
            a1 = (s2[c:] * wd[1]).astype(BF16)
            w0p = jnp.where(lane_lo, w0[0], w0[1])
            mtp = jnp.where(lane_lo, mt[0], mt[1])
            nd = jnp.dot(
                jnp.concatenate([a0, a1, (qp * w0p).astype(BF16)], axis=1),
                jnp.concatenate([jnp.concatenate([vb * pair_mask[0], pair_mask[0]], axis=1),
                                 jnp.concatenate([vb * pair_mask[1], pair_mask[1]], axis=1),
                                 cst_ref[p].astype(BF16)], axis=0),
                preferred_element_type=F32)
            hh = nd[:, :LANES] / jnp.maximum(jnp.abs(nd[:, LANES:]), jnp.exp(-mtp))
            mix_ref[rows, MIX_ML + p * LANES:MIX_ML + (p + 1) * LANES] = (
                hh * _sigmoid(z_ref[rows, Z_MO + p * LANES:Z_MO + (p + 1) * LANES]))
            wls.append(jnp.where(sub_lo, wd[0][c - 1:c, :], wd[1][c - 1:c, :]))
            w0l = w0p[c - 1:c, :]
            w0ls.append(jnp.concatenate([w0l, w0l], axis=1))
        upd = _dot(km.T * jnp.concatenate(wls, axis=0), jnp.concatenate([mvb, ones_b], axis=1))
        for p in range(PAIRS):
            lanes = slice(p * LANES, (p + 1) * LANES)
            cst_ref[p] = cst_ref[p] * w0ls[p] + bm_aug * jnp.concatenate(
                [upd[lanes, lanes], upd[lanes, ML_W:]], axis=1)

    conv_tail = xp_ref[block + conv_lo:block + 8, :]
    xp_ref[conv_lo:8, :] = conv_tail

    on256 = on256_ref[...]
    o = mix_ref[:, 0:GLA_W]
    ms = _dot(o * o, on256) * (1.0 / GLA_DV)
    mix_ref[:, 0:GLA_W] = o * lax.rsqrt(ms + EPS) * gnorm_ref[...] * _silu(z_ref[:, Z_GG:Z_GG + GLA_W])
    for g in range((RET_W + ML_W) // GLA_W):
        cols = slice(MIX_RET + g * GLA_W, MIX_RET + (g + 1) * GLA_W)
        o = mix_ref[:, cols]
        oc = o - _dot(o, on256) * (1.0 / RET_D)
        on = oc * lax.rsqrt(_dot(oc * oc, on256) * (1.0 / RET_D) + EPS)
        for j in range(GLA_W // LANES):
            t = (GLA_W // LANES) * g + j
            tile = on[:, j * LANES:(j + 1) * LANES]
            if t < PAIRS:
                tile = (tile * rnorm_ref[:, t * LANES:(t + 1) * LANES]
                        * _silu(z_ref[:, Z_RG + t * LANES:Z_RG + (t + 1) * LANES]))
            else:
                tile = tile * mnorm_ref[:, (t - PAIRS) * LANES:(t - PAIRS + 1) * LANES]
            mix_ref[:, MIX_RET + t * LANES:MIX_RET + (t + 1) * LANES] = tile

    @pl.when(step == last)
    def _finish():
        for h in range(GLA_HEADS):
            gout_ref[h] = gst_ref[h * GLA_DK:(h + 1) * GLA_DK, h * GLA_DV:(h + 1) * GLA_DV]
        for h in range(RET_HEADS):
            p, a = divmod(h, 2)
            rout_ref[h] = sst_ref[p, a * hd:(a + 1) * hd, a * hd:(a + 1) * hd]
            cout_ref[h] = cst_ref[p, a * hd:(a + 1) * hd, a * hd:(a + 1) * hd]
            nout_ref[h] = cst_ref[p, a * hd:(a + 1) * hd, LANES + a * hd:LANES + (a + 1) * hd]
        mout_ref[...] = mst_ref[...]
        cvout_ref[...] = conv_tail


def _mixer_constants(block, chunk):
    t = np.arange(block)
    tril = ((t[:, None] // chunk == t[None, :] // chunk) & (t[None, :] <= t[:, None])).astype(np.float32)
    r = np.arange(2 * LANES)[:, None] % LANES
    col = np.arange(ML_HEADS * LANES)[None, :] // LANES
    lane = np.arange(LANES)[None, :]
    ind = np.concatenate([r == col + SM_FG, (r == lane + ML_HEADS) & (lane >= SM_IG) & (lane < SM_FG)],
                         axis=1).astype(np.float32)
    i256 = np.arange(GLA_W) // GLA_DV
    on256 = (i256[:, None] == i256[None, :]).astype(np.float32)
    return tuple(jnp.asarray(m, BF16) for m in (tril, ind, on256))


def _mixer(z3, tabs, wts, init, layer):
    bsz, seq, _ = z3.shape
    block = min(MIXER_BLOCK, seq)
    chunk = min(MIXER_CHUNK, block)
    has_init = init is not None
    cos, sin, rdec, rint, rwend, rgam = tabs
    consts = _mixer_constants(block, chunk)

    def const(shape):
        nd = len(shape)
        return pl.BlockSpec(shape, lambda b, s: (0,) * nd)

    def per_layer(shape):
        nd = len(shape)
        return pl.BlockSpec((None,) + shape, lambda b, s: (layer,) + (0,) * nd)

    def per_batch(shape, with_layer):
        nd = len(shape)
        if with_layer:
            return pl.BlockSpec((None, None) + shape, lambda b, s: (layer, b) + (0,) * nd)
        return pl.BlockSpec((None,) + shape, lambda b, s: (b,) + (0,) * nd)

    state_shapes = [(GLA_HEADS, GLA_DK, GLA_DV), (RET_HEADS, RET_D, RET_D), (ML_HEADS, ML_D, ML_D),
                    (ML_HEADS, ML_D, ML_D), (1, ML_HEADS * LANES), (CONV_W - 1, ML_W)]
    in_specs = [
        pl.BlockSpec((None, block, Z_W), lambda b, s: (b, s, 0)),
        pl.BlockSpec((block, RET_W), lambda b, s: (s, 0)),
        pl.BlockSpec((block, RET_W), lambda b, s: (s, 0)),
        const(rdec.shape), const(rint.shape), const(rwend.shape), const(rgam.shape),
    ] + [const(m.shape) for m in consts] + [
        per_layer((LANES, GLA_KW)), per_layer((1, GLA_KW)), per_layer((1, LANES)),
        per_layer((1, GLA_W)), per_layer((1, RET_W)), per_layer((1, ML_W)),
        per_layer((CONV_W, ML_W)), per_layer((1, ML_W)), per_layer((ML_W, 2 * ML_W)),
    ]
    args = [z3, cos, sin, rdec, rint, rwend, rgam] + list(consts) + list(wts)
    if has_init:
        in_specs += [per_batch(s, True) for s in state_shapes]
        args += list(init)
    out_specs = [pl.BlockSpec((None, block, D_MODEL), lambda b, s: (b, s, 0))]
    out_specs += [per_batch(s, False) for s in state_shapes]
    out_shape = [jax.ShapeDtypeStruct((bsz, seq, D_MODEL), F32)]
    out_shape += [jax.ShapeDtypeStruct((bsz,) + s, F32) for s in state_shapes]
    scratch = [
        pltpu.VMEM((GLA_KW, GLA_W), F32), pltpu.VMEM((PAIRS, LANES, LANES), F32),
        pltpu.VMEM((PAIRS, LANES, 2 * LANES), F32), pltpu.VMEM((1, ML_HEADS * LANES), F32),
        pltpu.VMEM((block + 8, ML_W), F32),
        pltpu.VMEM((block, ML_W), F32), pltpu.VMEM((block, ML_W), F32),
        pltpu.VMEM((block, RET_W), F32), pltpu.VMEM((block, RET_W), F32),
        pltpu.VMEM((block, LANES), F32), pltpu.VMEM((block, ML_HEADS * LANES), F32),
        pltpu.VMEM((block, LANES), F32),
    ]
    return pl.pallas_call(
        functools.partial(_mixer_kernel, block=block, chunk=chunk, has_init=has_init),
        grid=(bsz, seq // block),
        in_specs=in_specs,
        out_specs=out_specs,
        out_shape=out_shape,
        scratch_shapes=scratch,
        compiler_params=pltpu.CompilerParams(dimension_semantics=("parallel", "arbitrary"),
                                             vmem_limit_bytes=VMEM_LIMIT),
        name="mixer",
    )(*args)


def _ffn_kernel(*refs, final):
    if final:
        x_ref, mix_ref, wo_ref, g_ref, wg_ref, wu_ref, wd_ref, gf_ref, out_ref, y_ref = refs
    else:
        x_ref, mix_ref, wo_ref, g_ref, wg_ref, wu_ref, wd_ref, out_ref = refs
    x1 = x_ref[...] + jnp.dot(mix_ref[...].astype(BF16), wo_ref[...], preferred_element_type=F32)
    r = lax.rsqrt(jnp.mean(x1 * x1, axis=-1, keepdims=True) + EPS)
    hf = (x1 * r * g_ref[...]).astype(BF16)
    acc = x1
    for c in range(D_FF // FF_CHUNK):
        cols = slice(c * FF_CHUNK, (c + 1) * FF_CHUNK)
        gate = jnp.dot(hf, wg_ref[:, cols], preferred_element_type=F32)
        up = jnp.dot(hf, wu_ref[:, cols], preferred_element_type=F32)
        act = (_silu(gate) * up).astype(BF16)
        acc = acc + jnp.dot(act, wd_ref[cols, :], preferred_element_type=F32)
    out_ref[...] = acc
    if final:
        rf = lax.rsqrt(jnp.mean(acc * acc, axis=-1, keepdims=True) + EPS)
        y_ref[...] = acc * rf * gf_ref[...]


def _ffn(x2d, mix2d, wo, norm3, wg, wu, wd, layer, norm_final2):
    n = x2d.shape[0]
    tm = min(TOKEN_BLOCK, n)
    final = norm_final2 is not None

    def weight(shape):
        return pl.BlockSpec((None,) + shape, lambda i: (layer, 0, 0), pipeline_mode=pl.Buffered(1))

    row_spec = pl.BlockSpec((tm, D_MODEL), lambda i: (i, 0))
    in_specs = [row_spec, row_spec, weight((D_MODEL, D_MODEL)),
                pl.BlockSpec((None, 1, D_MODEL), lambda i: (layer, 0, 0)),
                weight((D_MODEL, D_FF)), weight((D_MODEL, D_FF)), weight((D_FF, D_MODEL))]
    args = [x2d, mix2d, wo, norm3, wg, wu, wd]
    out_specs = [row_spec]
    out_shape = [jax.ShapeDtypeStruct((n, D_MODEL), F32)]
    if final:
        in_specs.append(pl.BlockSpec((1, D_MODEL), lambda i: (0, 0)))
        args.append(norm_final2)
        out_specs.append(row_spec)
        out_shape.append(jax.ShapeDtypeStruct((n, D_MODEL), F32))
    return pl.pallas_call(
        functools.partial(_ffn_kernel, final=final),
        grid=(n // tm,),
        in_specs=in_specs,
        out_specs=out_specs,
        out_shape=out_shape,
        compiler_params=pltpu.CompilerParams(dimension_semantics=("parallel",),
                                             vmem_limit_bytes=VMEM_LIMIT),
        name="ffn",
    )(*args)


def _reorder_w_in(w_in):
    offs = [0]
    for sz in SPLIT_SIZES:
        offs.append(offs[-1] + sz)
    seg = [w_in[..., offs[i]:offs[i + 1]] for i in range(len(SPLIT_SIZES))]
    gq, gk, gv, ga, gg, rq, rk, rv, rg, mu, mv, mo, mi, mf = seg
    pad = jnp.zeros(w_in.shape[:-1] + (LANES - GLA_RANK - 2 * ML_HEADS,), w_in.dtype)
    return jnp.concatenate([gq, gk, gv, gg, rq, rk, rv, rg, mu, mv, mo, ga, mi, mf, pad],
                           axis=-1).astype(BF16)


def _block_diag_heads(w):
    depth, h, d, e = w.shape
    eye = jnp.eye(h, dtype=w.dtype)
    return jnp.einsum('lhde,hg->lhdge', w, eye).reshape(depth, h * d, h * e)


def _rope_tables(pos):
    inv = ROPE_BASE ** (-jnp.arange(0, RET_D, 2, dtype=F32) / RET_D)
    ang = pos.astype(F32)[:, None] * inv[None, :]
    cos, sin = jnp.cos(ang), jnp.sin(ang)
    cos_h = jnp.concatenate([cos, cos], axis=-1)
    sin_h = jnp.concatenate([-sin, sin], axis=-1)
    return jnp.tile(cos_h, (1, RET_HEADS)), jnp.tile(sin_h, (1, RET_HEADS))


def _retention_tables(chunk):
    lg = jnp.log1p(-jnp.exp2(-5.0 - jnp.arange(RET_HEADS, dtype=F32)))
    t = jnp.arange(chunk, dtype=F32)
    diff = t[:, None] - t[None, :]
    decay = jnp.where((diff >= 0)[..., None], jnp.exp(jnp.maximum(diff, 0.0)[..., None] * lg), 0.0)
    decay = jnp.transpose(decay, (2, 0, 1)).reshape(PAIRS, 2 * chunk, chunk)
    inter = jnp.repeat(jnp.exp((t[:, None] + 1.0) * lg), RET_D, axis=1)
    w_end = jnp.exp((chunk - 1.0 - t)[None, :] * lg[:, None])
    w_end = jnp.repeat(w_end, RET_D, axis=0)
    gam = jnp.repeat(jnp.exp(chunk * lg), RET_D)[None, :]
    return decay, inter, w_end, gam


def kernel(x_prompt, x_sample, state_gla, state_ret, state_mlstm_c, state_mlstm_n, state_mlstm_m,
           cache_mlstm_conv, norm_mix, w_in, gla_w_a2, gla_b_a, gla_norm, ret_norm, ml_conv_w,
           ml_conv_b, ml_wq, ml_wk, ml_b_if, ml_norm, w_out, norm_ffn, w_gate, w_up, w_down, norm_final):
    depth = w_in.shape[0]

    w_in_p = _reorder_w_in(w_in)
    wo, wg, wu, wd = (w.astype(BF16) for w in (w_out, w_gate, w_up, w_down))
    wa2 = jnp.zeros((depth, LANES, GLA_KW), F32).at[:, :GLA_RANK, :].set(gla_w_a2).astype(BF16)
    bif = jnp.zeros((depth, 1, LANES), F32).at[:, 0, SM_IG:SM_IG + 2 * ML_HEADS].set(ml_b_if)
    wqk = jnp.concatenate([_block_diag_heads(ml_wq), _block_diag_heads(ml_wk)], axis=-1).astype(BF16)
    row = lambda a: a[:, None, :]
    wts = (wa2, row(gla_b_a), bif, row(gla_norm), row(ret_norm), row(ml_norm), ml_conv_w, row(ml_conv_b), wqk)
    norm_mix3, norm_ffn3, norm_final2 = row(norm_mix), row(norm_ffn), norm_final[None, :]

    init_sample = (state_gla, state_ret, state_mlstm_c,
                   jnp.broadcast_to(state_mlstm_n[..., None], state_mlstm_n.shape + (ML_D,)),
                   jnp.repeat(state_mlstm_m, LANES, axis=-1)[:, :, None, :], cache_mlstm_conv)

    def run(x, pos, init):
        bsz, seq, _ = x.shape
        chunk = min(MIXER_CHUNK, MIXER_BLOCK, seq)
        assert seq % chunk == 0 and chunk & (chunk - 1) == 0 and chunk >= 2 * 8
        tabs = _rope_tables(pos) + _retention_tables(chunk)
        x2d = x.reshape(bsz * seq, D_MODEL)
        states = []
        y2d = None
        for i in range(depth):
            z = _inproj(x2d, norm_mix3, w_in_p, i)
            mixed, *st = _mixer(z.reshape(bsz, seq, Z_W), tabs, wts, init, i)
            outs = _ffn(x2d, mixed.reshape(bsz * seq, D_MODEL), wo, norm_ffn3, wg, wu, wd, i,
                        norm_final2 if i == depth - 1 else None)
            x2d = outs[0]
            if i == depth - 1:
                y2d = outs[1]
            states.append(st)
        g, r, c, n, m, cv = (jnp.stack([st[j] for st in states]) for j in range(6))
        return (y2d.reshape(bsz, seq, D_MODEL), g, r, c,
                n[..., 0], m[:, :, 0, ::LANES], cv)

    pos_p = jnp.arange(x_prompt.shape[1], dtype=jnp.int32)
    pos_s = PAST_LEN + jnp.arange(x_sample.shape[1], dtype=jnp.int32)
    yp, gp, rp, cp, np_, mp, cvp = run(x_prompt, pos_p, None)
    ys, gs, rs, cs, ns, ms, cvs = run(x_sample, pos_s, init_sample)
    return (yp, ys, gp, gs, rp, rs, cp, cs, np_, ns, mp, ms, cvp, cvs)
```

```python
import functools

import numpy as np
import jax
import jax.numpy as jnp
from jax import lax
from jax.experimental import pallas as pl
from jax.experimental.pallas import tpu as pltpu

F32 = jnp.float32
BF16 = jnp.bfloat16

D_MODEL = 1024
EPS = 1e-6
GLA_HEADS, GLA_DK, GLA_DV, GLA_RANK, GLA_TAU = 4, 32, 64, 16, 16.0
RET_HEADS, RET_D = 6, 64
ML_HEADS, ML_D = 6, 64
CONV_W = 4
ROPE_BASE = 10000.0
PAST_LEN = 2048
GLA_KW = GLA_HEADS * GLA_DK
GLA_W = GLA_HEADS * GLA_DV
RET_W = RET_HEADS * RET_D
ML_W = ML_HEADS * ML_D
D_FF = 2816
SPLIT_SIZES = (GLA_KW, GLA_KW, GLA_W, GLA_RANK, GLA_W, RET_W, RET_W, RET_W, RET_W, ML_W, ML_W, ML_W,
               ML_HEADS, ML_HEADS)

LANES = 128
PAIRS = RET_W // LANES
Z_GQ, Z_GK, Z_GV, Z_GG = 0, 128, 256, 512
Z_RQ, Z_RK, Z_RV, Z_RG = 768, 1152, 1536, 1920
Z_MU, Z_MV, Z_MO = 2304, 2688, 3072
Z_SM = 3456
Z_W = 3584
SM_IG, SM_FG = GLA_RANK, GLA_RANK + ML_HEADS
MIX_RET, MIX_ML = GLA_W, GLA_W + RET_W

TOKEN_BLOCK = 512
MIXER_BLOCK = 512
MIXER_CHUNK = 128
FF_CHUNK = 1408
GLA_FACTOR_LIMIT = 1e18
VMEM_LIMIT = 56 * 1024 * 1024

_NT = (((1,), (1,)), ((), ()))


def _dot(a, b):
    return jnp.dot(a.astype(BF16), b.astype(BF16), preferred_element_type=F32)


def _dot_nt(a, b):
    return lax.dot_general(a.astype(BF16), b.astype(BF16), _NT, preferred_element_type=F32)


def _split2(x):
    hi = x.astype(BF16)
    return [hi, (x - hi.astype(F32)).astype(BF16)]


def _log_sigmoid(x):
    return jnp.minimum(x, 0.0) - jnp.log(1.0 + jnp.exp(-jnp.abs(x)))


def _sigmoid(x):
    return 0.5 + 0.5 * jnp.tanh(0.5 * x)


def _silu(x):
    return x * _sigmoid(x)


def _iota(shape, dim):
    return lax.broadcasted_iota(jnp.int32, shape, dim)


def _inproj_kernel(x_ref, g_ref, w_ref, z_ref):
    x = x_ref[...]
    r = lax.rsqrt(jnp.mean(x * x, axis=-1, keepdims=True) + EPS)
    h = (x * r * g_ref[...]).astype(BF16)
    z_ref[...] = jnp.dot(h, w_ref[...], preferred_element_type=F32)


def _inproj(x2d, norm3, w_in_p, layer):
    n = x2d.shape[0]
    tm = min(TOKEN_BLOCK, n)
    return pl.pallas_call(
        _inproj_kernel,
        grid=(n // tm,),
        in_specs=[
            pl.BlockSpec((tm, D_MODEL), lambda i: (i, 0)),
            pl.BlockSpec((None, 1, D_MODEL), lambda i: (layer, 0, 0)),
            pl.BlockSpec((None, D_MODEL, Z_W), lambda i: (layer, 0, 0), pipeline_mode=pl.Buffered(1)),
        ],
        out_specs=pl.BlockSpec((tm, Z_W), lambda i: (i, 0)),
        out_shape=jax.ShapeDtypeStruct((n, Z_W), F32),
        compiler_params=pltpu.CompilerParams(dimension_semantics=("parallel",),
                                             vmem_limit_bytes=VMEM_LIMIT),
        name="inproj",
    )(x2d, norm3, w_in_p)


def _mixer_kernel(*refs, block, chunk, has_init):
    it = iter(refs)
    (z_ref, cos_ref, sin_ref, rdec_ref, rint_ref, rwend_ref, rgam_ref, tril_ref, ind_ref,
     on256_ref) = (next(it) for _ in range(10))
    (wa2_ref, ba_ref, bif_ref, gnorm_ref, rnorm_ref, mnorm_ref, cw_ref, cb_ref,
     wqk_ref) = (next(it) for _ in range(9))
    if has_init:
        g0_ref, r0_ref, c0_ref, n0_ref, m0_ref, cv0_ref = (next(it) for _ in range(6))
    mix_ref, gout_ref, rout_ref, cout_ref, nout_ref, mout_ref, cvout_ref = (next(it) for _ in range(7))
    (gst_ref, sst_ref, cst_ref, mst_ref, xp_ref, qm_ref, km_ref, qr_ref, kr_ref, bg_ref, bx_ref,
     dg_ref, la_ref, gprev_ref) = (next(it) for _ in range(14))

    step = pl.program_id(1)
    last = pl.num_programs(1) - 1
    conv_lo = 8 - (CONV_W - 1)
    c = chunk
    hd = RET_D

    @pl.when(step == 0)
    def _init():
        gst_ref[...] = jnp.zeros_like(gst_ref)
        sst_ref[...] = jnp.zeros_like(sst_ref)
        cst_ref[...] = jnp.zeros_like(cst_ref)
        if has_init:
            for h in range(GLA_HEADS):
                gst_ref[h * GLA_DK:(h + 1) * GLA_DK, h * GLA_DV:(h + 1) * GLA_DV] = g0_ref[h]
            for h in range(RET_HEADS):
                p, a = divmod(h, 2)
                sst_ref[p, a * hd:(a + 1) * hd, a * hd:(a + 1) * hd] = r0_ref[h]
                cst_ref[p, a * hd:(a + 1) * hd, a * hd:(a + 1) * hd] = c0_ref[h]
                cst_ref[p, a * hd:(a + 1) * hd, LANES + a * hd:LANES + (a + 1) * hd] = n0_ref[h]
            mst_ref[...] = m0_ref[...]
            xp_ref[conv_lo:8, :] = cv0_ref[...]
        else:
            mst_ref[...] = jnp.zeros_like(mst_ref)
            xp_ref[conv_lo:8, :] = jnp.zeros((CONV_W - 1, ML_W), F32)

    lane_c = _iota((c, LANES), 1)
    lane_lo = (lane_c & hd) == 0
    pair_mask = [jnp.where(lane_lo, 1.0, 0.0).astype(BF16), jnp.where(lane_lo, 0.0, 1.0).astype(BF16)]
    gla_mask = [jnp.where((lane_c >> 5) == h, 1.0, 0.0).astype(BF16) for h in range(GLA_HEADS)]
    gv_mask = [jnp.where((_iota((c, GLA_W), 1) >> 6) == h, 1.0, 0.0).astype(BF16) for h in range(GLA_HEADS)]
    ones_b = jnp.ones((c, LANES), BF16)
    neg_inf = jnp.float32(-jnp.inf)
    causal = _iota((c, c), 1) <= _iota((c, c), 0)
    causal4 = _iota((GLA_HEADS * c, c), 1) <= (_iota((GLA_HEADS * c, c), 0) & (c - 1))
    bm_gk = jnp.where((_iota((GLA_KW, GLA_W), 0) >> 5) == (_iota((GLA_KW, GLA_W), 1) >> 6), 1.0, 0.0)
    bm_pair = jnp.where((_iota((LANES, LANES), 0) >> 6) == (_iota((LANES, LANES), 1) >> 6), 1.0, 0.0)
    bm_aug = jnp.concatenate([bm_pair, bm_pair], axis=1)
    sub_lo = _iota((LANES, c), 0) < hd

    def stack_pair(xb):
        return jnp.concatenate([xb * pair_mask[0], xb * pair_mask[1]], axis=0)

    xp_ref[8:8 + block, :] = z_ref[:, Z_MU:Z_MU + ML_W]
    cw = cw_ref[...]
    conv = cb_ref[...]
    for j in range(CONV_W):
        conv = conv + xp_ref[conv_lo + j:conv_lo + j + block, :] * cw[j:j + 1, :]
    qk = _dot(_silu(conv), wqk_ref[...])
    qm_ref[...] = qk[:, :ML_W]
    km_ref[...] = qk[:, ML_W:] * (ML_D ** -0.5)

    small = z_ref[:, Z_SM:Z_SM + LANES]
    la = _log_sigmoid(_dot(small, wa2_ref[...]) + ba_ref[...]) * (1.0 / GLA_TAU)
    gates = small + bif_ref[...]
    lf = _log_sigmoid(gates)
    cum = jnp.dot(tril_ref[...], jnp.concatenate(_split2(la) + _split2(lf), axis=1),
                  preferred_element_type=F32)
    bg_ref[...] = cum[:, :LANES] + cum[:, LANES:2 * LANES]
    la_ref[...] = la
    gprev_ref[...] = gst_ref[...]
    bt = cum[:, 2 * LANES:3 * LANES] + cum[:, 3 * LANES:]
    bxe = jnp.dot(jnp.concatenate(_split2(bt), axis=1), ind_ref[...], preferred_element_type=F32)
    bx_ref[...] = bxe[:, :ML_HEADS * LANES]
    dg_ref[...] = gates - bxe[:, ML_HEADS * LANES:]

    lo32 = (_iota((block, LANES), 1) & 32) == 0
    for p in range(PAIRS):
        lanes = slice(p * LANES, (p + 1) * LANES)
        cos = cos_ref[:, lanes]
        sin = sin_ref[:, lanes]
        for src, dst, scale in ((Z_RQ, qr_ref, 1.0), (Z_RK, kr_ref, RET_D ** -0.5)):
            t = z_ref[:, src + p * LANES:src + (p + 1) * LANES]
            sw = jnp.where(lo32, pltpu.roll(t, LANES - 32, 1), pltpu.roll(t, 32, 1))
            dst[:, lanes] = (t * cos + sw * sin) * scale

    fac_max = jnp.zeros((8, LANES), F32)
    for ci in range(block // c):
        rows = slice(ci * c, (ci + 1) * c)

        b = bg_ref[rows, :]
        bmid = b[c // 2 - 1:c // 2, :]
        bl = b[c - 1:c, :]
        gq = z_ref[rows, Z_GQ:Z_GQ + GLA_KW] * (GLA_DK ** -0.5)
        gk = z_ref[rows, Z_GK:Z_GK + GLA_KW]
        eq = jnp.exp(b - bmid)
        ek = jnp.exp(bmid - b)
        qh = gq * eq
        kh = gk * ek
        fac = jnp.maximum(jnp.maximum(jnp.abs(qh), jnp.abs(kh)), jnp.maximum(eq, ek))
        fac_max = functools.reduce(jnp.maximum, [fac_max] + [fac[i:i + 8] for i in range(0, c, 8)])
        qh = qh.astype(BF16)
        kh = kh.astype(BF16)
        qe = (gq * jnp.exp(b)).astype(BF16)
        kl_t = (gk * jnp.exp(bl - b)).T
        lhs = jnp.concatenate([qh * gla_mask[h] for h in range(GLA_HEADS)], axis=0)
        a = lax.dot_general(lhs, kh, _NT, preferred_element_type=F32)
        a = jnp.where(causal4, a, 0.0).astype(BF16)
        gvb = z_ref[rows, Z_GV:Z_GV + GLA_W].astype(BF16)
        o = jnp.dot(jnp.concatenate([a[h * c:(h + 1) * c] for h in range(GLA_HEADS)] + [qe], axis=1),
                    jnp.concatenate([gvb * gv_mask[h] for h in range(GLA_HEADS)]
                                    + [gst_ref[...].astype(BF16)], axis=0),
                    preferred_element_type=F32)
        mix_ref[rows, 0:GLA_W] = o
        bl_col = b[c - 8:c, :].T[:, 7:8]
        gst_ref[...] = gst_ref[...] * jnp.exp(bl_col) + bm_gk * _dot(kl_t, gvb)

        rvb = z_ref[rows, Z_RV:Z_RV + RET_W].astype(BF16)
        kr = kr_ref[rows, :]
        for p in range(PAIRS):
            lanes = slice(p * LANES, (p + 1) * LANES)
            qp = qr_ref[rows, lanes]
            vb = rvb[:, lanes]
            s2 = lax.dot_general(stack_pair(qp.astype(BF16)), kr[:, lanes].astype(BF16), _NT,
                                 preferred_element_type=F32)
            pm = (s2 * rdec_ref[p]).astype(BF16)
            mix_ref[rows, MIX_RET + p * LANES:MIX_RET + (p + 1) * LANES] = jnp.dot(
                jnp.concatenate([pm[:c], pm[c:], (qp * rint_ref[:, lanes]).astype(BF16)], axis=1),
                jnp.concatenate([vb * pair_mask[0], vb * pair_mask[1], sst_ref[p].astype(BF16)], axis=0),
                preferred_element_type=F32)
        upd = _dot(kr.T * rwend_ref[...], rvb)
        for p in range(PAIRS):
            lanes = slice(p * LANES, (p + 1) * LANES)
            sst_ref[p] = sst_ref[p] * rgam_ref[:, lanes] + bm_pair * upd[lanes, lanes]

        dg_t = dg_ref[rows, :].T
        mvb = z_ref[rows, Z_MV:Z_MV + ML_W].astype(BF16)
        km = km_ref[rows, :]
        wls, w0ls = [], []
        for p in range(PAIRS):
            lanes = slice(p * LANES, (p + 1) * LANES)
            wd, w0, mt = [], [], []
            for a_ in range(2):
                h = 2 * p + a_
                colr = bx_ref[rows, h * LANES:(h + 1) * LANES]
                logd = jnp.where(causal, colr[:, :c] + dg_t[SM_IG + h:SM_IG + h + 1, :], neg_inf)
                log0 = colr + mst_ref[:, h * LANES:(h + 1) * LANES]
                m_t = jnp.maximum(log0, jnp.max(logd, axis=1, keepdims=True))
                wd.append(jnp.exp(logd - m_t[:, :c]))
                w0.append(jnp.exp(log0 - m_t))
                mt.append(m_t)
                mst_ref[:, h * LANES:(h + 1) * LANES] = m_t[c - 1:c, :]
            qp = qm_ref[rows, lanes]
            vb = mvb[:, lanes]
            s2 = lax.dot_general(stack_pair(qp.astype(BF16)), km[:, lanes].astype(BF16), _NT,
                                 preferred_element_type=F32)
            a0 = (s2[:c] * wd[0]).astype(BF16)
            a1 = (s2[c:] * wd[1]).astype(BF16)
            w0p = jnp.where(lane_lo, w0[0], w0[1])
            mtp = jnp.where(lane_lo, mt[0], mt[1])
            nd = jnp.dot(
                jnp.concatenate([a0, a1, (qp * w0p).astype(BF16)], axis=1),
                jnp.concatenate([jnp.concatenate([vb * pair_mask[0], pair_mask[0]], axis=1),
                                 jnp.concatenate([vb * pair_mask[1], pair_mask[1]], axis=1),
                                 cst_ref[p].astype(BF16)], axis=0),
                preferred_element_type=F32)
            hh = nd[:, :LANES] / jnp.maximum(jnp.abs(nd[:, LANES:]), jnp.exp(-mtp))
            mix_ref[rows, MIX_ML + p * LANES:MIX_ML + (p + 1) * LANES] = (
                hh * _sigmoid(z_ref[rows, Z_MO + p * LANES:Z_MO + (p + 1) * LANES]))
            wls.append(jnp.where(sub_lo, wd[0][c - 1:c, :], wd[1][c - 1:c, :]))
            w0l = w0p[c - 1:c, :]
            w0ls.append(jnp.concatenate([w0l, w0l], axis=1))
        upd = _dot(km.T * jnp.concatenate(wls, axis=0), jnp.concatenate([mvb, ones_b], axis=1))
        for p in range(PAIRS):
            lanes = slice(p * LANES, (p + 1) * LANES)
            cst_ref[p] = cst_ref[p] * w0ls[p] + bm_aug * jnp.concatenate(
                [upd[lanes, lanes], upd[lanes, ML_W:]], axis=1)

    conv_tail = xp_ref[block + conv_lo:block + 8, :]
    xp_ref[conv_lo:8, :] = conv_tail

    @pl.when(jnp.logical_not(jnp.max(fac_max) < GLA_FACTOR_LIMIT))
    def _gla_per_token():
        gst_ref[...] = gprev_ref[...]

        def tokens8(g, carry):
            rows8 = pl.ds(pl.multiple_of(g * 8, 8), 8)
            alpha_t = jnp.exp(la_ref[rows8, :]).T
            k_t = z_ref[rows8, Z_GK:Z_GK + GLA_KW].T
            q_t = (z_ref[rows8, Z_GQ:Z_GQ + GLA_KW] * (GLA_DK ** -0.5)).T
            v8 = z_ref[rows8, Z_GV:Z_GV + GLA_W]
            s = gst_ref[...]
            outs = []
            for j in range(8):
                s = s * alpha_t[:, j:j + 1] + bm_gk * (k_t[:, j:j + 1] * v8[j:j + 1, :])
                outs.append(jnp.sum(q_t[:, j:j + 1] * s, axis=0, keepdims=True))
            gst_ref[...] = s
            mix_ref[rows8, 0:GLA_W] = jnp.concatenate(outs, axis=0)
            return carry

        lax.fori_loop(0, block // 8, tokens8, 0)

    on256 = on256_ref[...]
    o = mix_ref[:, 0:GLA_W]
    ms = _dot(o * o, on256) * (1.0 / GLA_DV)
    mix_ref[:, 0:GLA_W] = o * lax.rsqrt(ms + EPS) * gnorm_ref[...] * _silu(z_ref[:, Z_GG:Z_GG + GLA_W])
    for g in range((RET_W + ML_W) // GLA_W):
        cols = slice(MIX_RET + g * GLA_W, MIX_RET + (g + 1) * GLA_W)
        o = mix_ref[:, cols]
        oc = o - _dot(o, on256) * (1.0 / RET_D)
        on = oc * lax.rsqrt(_dot(oc * oc, on256) * (1.0 / RET_D) + EPS)
        for j in range(GLA_W // LANES):
            t = (GLA_W // LANES) * g + j
            tile = on[:, j * LANES:(j + 1) * LANES]
            if t < PAIRS:
                tile = (tile * rnorm_ref[:, t * LANES:(t + 1) * LANES]
                        * _silu(z_ref[:, Z_RG + t * LANES:Z_RG + (t + 1) * LANES]))
            else:
                tile = tile * mnorm_ref[:, (t - PAIRS) * LANES:(t - PAIRS + 1) * LANES]
            mix_ref[:, MIX_RET + t * LANES:MIX_RET + (t + 1) * LANES] = tile

    @pl.when(step == last)
    def _finish():
        for h in range(GLA_HEADS):
            gout_ref[h] = gst_ref[h * GLA_DK:(h + 1) * GLA_DK, h * GLA_DV:(h + 1) * GLA_DV]
        for h in range(RET_HEADS):
            p, a = divmod(h, 2)
            rout_ref[h] = sst_ref[p, a * hd:(a + 1) * hd, a * hd:(a + 1) * hd]
            cout_ref[h] = cst_ref[p, a * hd:(a + 1) * hd, a * hd:(a + 1) * hd]
            nout_ref[h] = cst_ref[p, a * hd:(a + 1) * hd, LANES + a * hd:LANES + (a + 1) * hd]
        mout_ref[...] = mst_ref[...]
        cvout_ref[...] = conv_tail


def _mixer_constants(block, chunk):
    t = np.arange(block)
    tril = ((t[:, None] // chunk == t[None, :] // chunk) & (t[None, :] <= t[:, None])).astype(np.float32)
    r = np.arange(2 * LANES)[:, None] % LANES
    col = np.arange(ML_HEADS * LANES)[None, :] // LANES
    lane = np.arange(LANES)[None, :]
    ind = np.concatenate([r == col + SM_FG, (r == lane + ML_HEADS) & (lane >= SM_IG) & (lane < SM_FG)],
                         axis=1).astype(np.float32)
    i256 = np.arange(GLA_W) // GLA_DV
    on256 = (i256[:, None] == i256[None, :]).astype(np.float32)
    return tuple(jnp.asarray(m, BF16) for m in (tril, ind, on256))


def _mixer(z3, tabs, wts, init, layer):
    bsz, seq, _ = z3.shape
    block = min(MIXER_BLOCK, seq)
    chunk = min(MIXER_CHUNK, block)
    has_init = init is not None
    cos, sin, rdec, rint, rwend, rgam = tabs
    consts = _mixer_constants(block, chunk)

    def const(shape):
        nd = len(shape)
        return pl.BlockSpec(shape, lambda b, s: (0,) * nd)

    def per_layer(shape):
        nd = len(shape)
        return pl.BlockSpec((None,) + shape, lambda b, s: (layer,) + (0,) * nd)

    def per_batch(shape, with_layer):
        nd = len(shape)
        if with_layer:
            return pl.BlockSpec((None, None) + shape, lambda b, s: (layer, b) + (0,) * nd)
        return pl.BlockSpec((None,) + shape, lambda b, s: (b,) + (0,) * nd)

    state_shapes = [(GLA_HEADS, GLA_DK, GLA_DV), (RET_HEADS, RET_D, RET_D), (ML_HEADS, ML_D, ML_D),
                    (ML_HEADS, ML_D, ML_D), (1, ML_HEADS * LANES), (CONV_W - 1, ML_W)]
    in_specs = [
        pl.BlockSpec((None, block, Z_W), lambda b, s: (b, s, 0)),
        pl.BlockSpec((block, RET_W), lambda b, s: (s, 0)),
        pl.BlockSpec((block, RET_W), lambda b, s: (s, 0)),
        const(rdec.shape), const(rint.shape), const(rwend.shape), const(rgam.shape),
    ] + [const(m.shape) for m in consts] + [
        per_layer((LANES, GLA_KW)), per_layer((1, GLA_KW)), per_layer((1, LANES)),
        per_layer((1, GLA_W)), per_layer((1, RET_W)), per_layer((1, ML_W)),
        per_layer((CONV_W, ML_W)), per_layer((1, ML_W)), per_layer((ML_W, 2 * ML_W)),
    ]
    args = [z3, cos, sin, rdec, rint, rwend, rgam] + list(consts) + list(wts)
    if has_init:
        in_specs += [per_batch(s, True) for s in state_shapes]
        args += list(init)
    out_specs = [pl.BlockSpec((None, block, D_MODEL), lambda b, s: (b, s, 0))]
    out_specs += [per_batch(s, False) for s in state_shapes]
    out_shape = [jax.ShapeDtypeStruct((bsz, seq, D_MODEL), F32)]
    out_shape += [jax.ShapeDtypeStruct((bsz,) + s, F32) for s in state_shapes]
    scratch = [
        pltpu.VMEM((GLA_KW, GLA_W), F32), pltpu.VMEM((PAIRS, LANES, LANES), F32),
        pltpu.VMEM((PAIRS, LANES, 2 * LANES), F32), pltpu.VMEM((1, ML_HEADS * LANES), F32),
        pltpu.VMEM((block + 8, ML_W), F32),
        pltpu.VMEM((block, ML_W), F32), pltpu.VMEM((block, ML_W), F32),
        pltpu.VMEM((block, RET_W), F32), pltpu.VMEM((block, RET_W), F32),
        pltpu.VMEM((block, LANES), F32), pltpu.VMEM((block, ML_HEADS * LANES), F32),
        pltpu.VMEM((block, LANES), F32), pltpu.VMEM((block, LANES), F32),
        pltpu.VMEM((GLA_KW, GLA_W), F32),
    ]
    return pl.pallas_call(
        functools.partial(_mixer_kernel, block=block, chunk=chunk, has_init=has_init),
        grid=(bsz, seq // block),
        in_specs=in_specs,
        out_specs=out_specs,
        out_shape=out_shape,
        scratch_shapes=scratch,
        compiler_params=pltpu.CompilerParams(dimension_semantics=("parallel", "arbitrary"),
                                             vmem_limit_bytes=VMEM_LIMIT),
        name="mixer",
    )(*args)


def _ffn_kernel(*refs, final):
    if final:
        x_ref, mix_ref, wo_ref, g_ref, wg_ref, wu_ref, wd_ref, gf_ref, out_ref, y_ref = refs
    else:
        x_ref, mix_ref, wo_ref, g_ref, wg_ref, wu_ref, wd_ref, out_ref = refs
    x1 = x_ref[...] + jnp.dot(mix_ref[...].astype(BF16), wo_ref[...], preferred_element_type=F32)
    r = lax.rsqrt(jnp.mean(x1 * x1, axis=-1, keepdims=True) + EPS)
    hf = (x1 * r * g_ref[...]).astype(BF16)
    acc = x1
    for c in range(D_FF // FF_CHUNK):
        cols = slice(c * FF_CHUNK, (c + 1) * FF_CHUNK)
        gate = jnp.dot(hf, wg_ref[:, cols], preferred_element_type=F32)
        up = jnp.dot(hf, wu_ref[:, cols], preferred_element_type=F32)
        act = (_silu(gate) * up).astype(BF16)
        acc = acc + jnp.dot(act, wd_ref[cols, :], preferred_element_type=F32)
    out_ref[...] = acc
    if final:
        rf = lax.rsqrt(jnp.mean(acc * acc, axis=-1, keepdims=True) + EPS)
        y_ref[...] = acc * rf * gf_ref[...]


def _ffn(x2d, mix2d, wo, norm3, wg, wu, wd, layer, norm_final2):
    n = x2d.shape[0]
    tm = min(TOKEN_BLOCK, n)
    final = norm_final2 is not None

    def weight(shape):
        return pl.BlockSpec((None,) + shape, lambda i: (layer, 0, 0), pipeline_mode=pl.Buffered(1))

    row_spec = pl.BlockSpec((tm, D_MODEL), lambda i: (i, 0))
    in_specs = [row_spec, row_spec, weight((D_MODEL, D_MODEL)),
                pl.BlockSpec((None, 1, D_MODEL), lambda i: (layer, 0, 0)),
                weight((D_MODEL, D_FF)), weight((D_MODEL, D_FF)), weight((D_FF, D_MODEL))]
    args = [x2d, mix2d, wo, norm3, wg, wu, wd]
    out_specs = [row_spec]
    out_shape = [jax.ShapeDtypeStruct((n, D_MODEL), F32)]
    if final:
        in_specs.append(pl.BlockSpec((1, D_MODEL), lambda i: (0, 0)))
        args.append(norm_final2)
        out_specs.append(row_spec)
        out_shape.append(jax.ShapeDtypeStruct((n, D_MODEL), F32))
    return pl.pallas_call(
        functools.partial(_ffn_kernel, final=final),
        grid=(n // tm,),
        in_specs=in_specs,
        out_specs=out_specs,
        out_shape=out_shape,
        compiler_params=pltpu.CompilerParams(dimension_semantics=("parallel",),
                                             vmem_limit_bytes=VMEM_LIMIT),
        name="ffn",
    )(*args)


def _reorder_w_in(w_in):
    offs = [0]
    for sz in SPLIT_SIZES:
        offs.append(offs[-1] + sz)
    seg = [w_in[..., offs[i]:offs[i + 1]] for i in range(len(SPLIT_SIZES))]
    gq, gk, gv, ga, gg, rq, rk, rv, rg, mu, mv, mo, mi, mf = seg
    pad = jnp.zeros(w_in.shape[:-1] + (LANES - GLA_RANK - 2 * ML_HEADS,), w_in.dtype)
    return jnp.concatenate([gq, gk, gv, gg, rq, rk, rv, rg, mu, mv, mo, ga, mi, mf, pad],
                           axis=-1).astype(BF16)


def _block_diag_heads(w):
    depth, h, d, e = w.shape
    eye = jnp.eye(h, dtype=w.dtype)
    return jnp.einsum('lhde,hg->lhdge', w, eye).reshape(depth, h * d, h * e)


def _rope_tables(pos):
    inv = ROPE_BASE ** (-jnp.arange(0, RET_D, 2, dtype=F32) / RET_D)
    ang = pos.astype(F32)[:, None] * inv[None, :]
    cos, sin = jnp.cos(ang), jnp.sin(ang)
    cos_h = jnp.concatenate([cos, cos], axis=-1)
    sin_h = jnp.concatenate([-sin, sin], axis=-1)
    return jnp.tile(cos_h, (1, RET_HEADS)), jnp.tile(sin_h, (1, RET_HEADS))


def _retention_tables(chunk):
    lg = jnp.log1p(-jnp.exp2(-5.0 - jnp.arange(RET_HEADS, dtype=F32)))
    t = jnp.arange(chunk, dtype=F32)
    diff = t[:, None] - t[None, :]
    decay = jnp.where((diff >= 0)[..., None], jnp.exp(jnp.maximum(diff, 0.0)[..., None] * lg), 0.0)
    decay = jnp.transpose(decay, (2, 0, 1)).reshape(PAIRS, 2 * chunk, chunk)
    inter = jnp.repeat(jnp.exp((t[:, None] + 1.0) * lg), RET_D, axis=1)
    w_end = jnp.exp((chunk - 1.0 - t)[None, :] * lg[:, None])
    w_end = jnp.repeat(w_end, RET_D, axis=0)
    gam = jnp.repeat(jnp.exp(chunk * lg), RET_D)[None, :]
    return decay, inter, w_end, gam


def kernel(x_prompt, x_sample, state_gla, state_ret, state_mlstm_c, state_mlstm_n, state_mlstm_m,
           cache_mlstm_conv, norm_mix, w_in, gla_w_a2, gla_b_a, gla_norm, ret_norm, ml_conv_w,
           ml_conv_b, ml_wq, ml_wk, ml_b_if, ml_norm, w_out, norm_ffn, w_gate, w_up, w_down, norm_final):
    depth = w_in.shape[0]

    w_in_p = _reorder_w_in(w_in)
    wo, wg, wu, wd = (w.astype(BF16) for w in (w_out, w_gate, w_up, w_down))
    wa2 = jnp.zeros((depth, LANES, GLA_KW), F32).at[:, :GLA_RANK, :].set(gla_w_a2).astype(BF16)
    bif = jnp.zeros((depth, 1, LANES), F32).at[:, 0, SM_IG:SM_IG + 2 * ML_HEADS].set(ml_b_if)
    wqk = jnp.concatenate([_block_diag_heads(ml_wq), _block_diag_heads(ml_wk)], axis=-1).astype(BF16)
    row = lambda a: a[:, None, :]
    wts = (wa2, row(gla_b_a), bif, row(gla_norm), row(ret_norm), row(ml_norm), ml_conv_w, row(ml_conv_b), wqk)
    norm_mix3, norm_ffn3, norm_final2 = row(norm_mix), row(norm_ffn), norm_final[None, :]

    init_sample = (state_gla, state_ret, state_mlstm_c,
                   jnp.broadcast_to(state_mlstm_n[..., None], state_mlstm_n.shape + (ML_D,)),
                   jnp.repeat(state_mlstm_m, LANES, axis=-1)[:, :, None, :], cache_mlstm_conv)

    def run(x, pos, init):
        bsz, seq, _ = x.shape
        chunk = min(MIXER_CHUNK, MIXER_BLOCK, seq)
        assert seq % chunk == 0 and chunk & (chunk - 1) == 0 and chunk >= 2 * 8
        tabs = _rope_tables(pos) + _retention_tables(chunk)
        x2d = x.reshape(bsz * seq, D_MODEL)
        states = []
        y2d = None
        for i in range(depth):
            z = _inproj(x2d, norm_mix3, w_in_p, i)
            mixed, *st = _mixer(z.reshape(bsz, seq, Z_W), tabs, wts, init, i)
            outs = _ffn(x2d, mixed.reshape(bsz * seq, D_MODEL), wo, norm_ffn3, wg, wu, wd, i,
                        norm_final2 if i == depth - 1 else None)
            x2d = outs[0]
            if i == depth - 1:
                y2d = outs[1]
            states.append(st)
        g, r, c, n, m, cv = (jnp.stack([st[j] for st in states]) for j in range(6))
        return (y2d.reshape(bsz, seq, D_MODEL), g, r, c,
                n[..., 0], m[:, :, 0, ::LANES], cv)

    pos_p = jnp.arange(x_prompt.shape[1], dtype=jnp.int32)
    pos_s = PAST_LEN + jnp.arange(x_sample.shape[1], dtype=jnp.int32)
    yp, gp, rp, cp, np_, mp, cvp = run(x_prompt, pos_p, None)
    ys, gs, rs, cs, ns, ms, cvs = run(x_sample, pos_s, init_sample)
    return (yp, ys, gp, gs, rp, rs, cp, cs, np_, ns, mp, ms, cvp, cvs)
```

```python
import functools

import numpy as np
import jax
import jax.numpy as jnp
from jax import lax
from jax.experimental import pallas as pl
from jax.experimental.pallas import tpu as pltpu

F32 = jnp.float32
BF16 = jnp.bfloat16

D_MODEL = 1024
EPS = 1e-6
GLA_HEADS, GLA_DK, GLA_DV, GLA_RANK, GLA_TAU = 4, 32, 64, 16, 16.0
RET_HEADS, RET_D = 6, 64
ML_HEADS, ML_D = 6, 64
CONV_W = 4
ROPE_BASE = 10000.0
PAST_LEN = 2048
GLA_KW = GLA_HEADS * GLA_DK
GLA_W = GLA_HEADS * GLA_DV
RET_W = RET_HEADS * RET_D
ML_W = ML_HEADS * ML_D
D_FF = 2816
SPLIT_SIZES = (GLA_KW, GLA_KW, GLA_W, GLA_RANK, GLA_W, RET_W, RET_W, RET_W, RET_W, ML_W, ML_W, ML_W,
               ML_HEADS, ML_HEADS)

LANES = 128
PAIRS = RET_W // LANES
Z_GQ, Z_GK, Z_GV, Z_GG = 0, 128, 256, 512
Z_RQ, Z_RK, Z_RV, Z_RG = 768, 1152, 1536, 1920
Z_MU, Z_MV, Z_MO = 2304, 2688, 3072
Z_SM = 3456
Z_W = 3584
SM_IG, SM_FG = GLA_RANK, GLA_RANK + ML_HEADS
MIX_RET, MIX_ML = GLA_W, GLA_W + RET_W

TOKEN_BLOCK = 512
MIXER_BLOCK = 512
MIXER_CHUNK = 128
MIXER_GROUP = 256
FF_CHUNK = 1408
GLA_FACTOR_LIMIT = 1e18
VMEM_LIMIT = 56 * 1024 * 1024

_NT = (((1,), (1,)), ((), ()))


def _dot(a, b):
    return jnp.dot(a.astype(BF16), b.astype(BF16), preferred_element_type=F32)


def _dot_nt(a, b):
    return lax.dot_general(a.astype(BF16), b.astype(BF16), _NT, preferred_element_type=F32)


def _split2(x):
    hi = x.astype(BF16)
    return [hi, (x - hi.astype(F32)).astype(BF16)]


def _log_sigmoid(x):
    return jnp.minimum(x, 0.0) - jnp.log(1.0 + jnp.exp(-jnp.abs(x)))


def _sigmoid(x):
    return 0.5 + 0.5 * jnp.tanh(0.5 * x)


def _silu(x):
    h = 0.5 * x
    return h + h * jnp.tanh(h)


def _iota(shape, dim):
    return lax.broadcasted_iota(jnp.int32, shape, dim)


def _inproj_kernel(x_ref, g_ref, w_ref, z_ref):
    x = x_ref[...]
    r = lax.rsqrt(jnp.mean(x * x, axis=-1, keepdims=True) + EPS)
    h = (x * r * g_ref[...]).astype(BF16)
    z_ref[...] = jnp.dot(h, w_ref[...], preferred_element_type=F32)


def _inproj(x2d, norm3, w_in_p, layer):
    n = x2d.shape[0]
    tm = min(TOKEN_BLOCK, n)
    return pl.pallas_call(
        _inproj_kernel,
        grid=(n // tm,),
        in_specs=[
            pl.BlockSpec((tm, D_MODEL), lambda i: (i, 0)),
            pl.BlockSpec((None, 1, D_MODEL), lambda i: (layer, 0, 0)),
            pl.BlockSpec((None, D_MODEL, Z_W), lambda i: (layer, 0, 0), pipeline_mode=pl.Buffered(1)),
        ],
        out_specs=pl.BlockSpec((tm, Z_W), lambda i: (i, 0)),
        out_shape=jax.ShapeDtypeStruct((n, Z_W), F32),
        compiler_params=pltpu.CompilerParams(dimension_semantics=("parallel",),
                                             vmem_limit_bytes=VMEM_LIMIT),
        name="inproj",
    )(x2d, norm3, w_in_p)


def _mixer_kernel(*refs, block, chunk, has_init):
    it = iter(refs)
    (z_ref, cos_ref, sin_ref, rdec_ref, rint_ref, rwend_ref, rgam_ref, tril_ref, ind_ref,
     on256_ref) = (next(it) for _ in range(10))
    (wa2_ref, ba_ref, bif_ref, gnorm_ref, rnorm_ref, mnorm_ref, cw_ref, cb_ref,
     wqk_ref) = (next(it) for _ in range(9))
    if has_init:
        g0_ref, r0_ref, c0_ref, n0_ref, m0_ref, cv0_ref = (next(it) for _ in range(6))
    mix_ref, gout_ref, rout_ref, cout_ref, nout_ref, mout_ref, cvout_ref = (next(it) for _ in range(7))
    gst_ref, sst_ref, cst_ref, mst_ref, xp_ref, la_ref, gprev_ref = (next(it) for _ in range(7))

    step = pl.program_id(1)
    last = pl.num_programs(1) - 1
    conv_lo = 8 - (CONV_W - 1)
    c = chunk
    hd = RET_D
    group = min(MIXER_GROUP, block)

    @pl.when(step == 0)
    def _init():
        gst_ref[...] = jnp.zeros_like(gst_ref)
        sst_ref[...] = jnp.zeros_like(sst_ref)
        cst_ref[...] = jnp.zeros_like(cst_ref)
        if has_init:
            for h in range(GLA_HEADS):
                gst_ref[h * GLA_DK:(h + 1) * GLA_DK, h * GLA_DV:(h + 1) * GLA_DV] = g0_ref[h]
            for h in range(RET_HEADS):
                p, a = divmod(h, 2)
                sst_ref[p, a * hd:(a + 1) * hd, a * hd:(a + 1) * hd] = r0_ref[h]
                cst_ref[p, a * hd:(a + 1) * hd, a * hd:(a + 1) * hd] = c0_ref[h]
                cst_ref[p, a * hd:(a + 1) * hd, LANES + a * hd:LANES + (a + 1) * hd] = n0_ref[h]
            mst_ref[...] = m0_ref[...]
            xp_ref[conv_lo:8, :] = cv0_ref[...]
        else:
            mst_ref[...] = jnp.zeros_like(mst_ref)
            xp_ref[conv_lo:8, :] = jnp.zeros((CONV_W - 1, ML_W), F32)

    lane_c = _iota((c, LANES), 1)
    lane_lo = (lane_c & hd) == 0
    pair_mask = [jnp.where(lane_lo, 1.0, 0.0).astype(BF16), jnp.where(lane_lo, 0.0, 1.0).astype(BF16)]
    gla_mask = [jnp.where((lane_c >> 5) == h, 1.0, 0.0).astype(BF16) for h in range(GLA_HEADS)]
    gv_mask = [jnp.where((_iota((c, GLA_W), 1) >> 6) == h, 1.0, 0.0).astype(BF16) for h in range(GLA_HEADS)]
    ones_b = jnp.ones((c, LANES), BF16)
    neg_inf = jnp.float32(-jnp.inf)
    causal = _iota((c, c), 1) <= _iota((c, c), 0)
    causal4 = _iota((GLA_HEADS * c, c), 1) <= (_iota((GLA_HEADS * c, c), 0) & (c - 1))
    bm_gk = jnp.where((_iota((GLA_KW, GLA_W), 0) >> 5) == (_iota((GLA_KW, GLA_W), 1) >> 6), 1.0, 0.0)
    bm_pair = jnp.where((_iota((LANES, LANES), 0) >> 6) == (_iota((LANES, LANES), 1) >> 6), 1.0, 0.0)
    bm_aug = jnp.concatenate([bm_pair, bm_pair], axis=1)
    sub_lo = _iota((LANES, c), 0) < hd

    def stack_pair(xb):
        return jnp.concatenate([xb * pair_mask[0], xb * pair_mask[1]], axis=0)

    xp_ref[8:8 + block, :] = z_ref[:, Z_MU:Z_MU + ML_W]
    gprev_ref[...] = gst_ref[...]
    cw = cw_ref[...]
    lo32 = (_iota((group, LANES), 1) & 32) == 0
    on256 = on256_ref[...]

    def prep(g):
        rs = slice(g * group, (g + 1) * group)
        conv = cb_ref[...]
        for j in range(CONV_W):
            lo = conv_lo + j + g * group
            conv = conv + xp_ref[lo:lo + group, :] * cw[j:j + 1, :]
        qk = _dot(_silu(conv), wqk_ref[...])
        out = {'qm': qk[:, :ML_W], 'km': qk[:, ML_W:]}
        small = z_ref[rs, Z_SM:Z_SM + LANES]
        la = _log_sigmoid(_dot(small, wa2_ref[...]) + ba_ref[...]) * (1.0 / GLA_TAU)
        gates = small + bif_ref[...]
        lf = _log_sigmoid(gates)
        cum = jnp.dot(tril_ref[...], jnp.concatenate(_split2(la) + _split2(lf), axis=1),
                      preferred_element_type=F32)
        out['bg'] = cum[:, :LANES] + cum[:, LANES:2 * LANES]
        la_ref[rs, :] = la
        bt = cum[:, 2 * LANES:3 * LANES] + cum[:, 3 * LANES:]
        bxe = jnp.dot(jnp.concatenate(_split2(bt), axis=1), ind_ref[...], preferred_element_type=F32)
        out['bx'] = bxe[:, :ML_HEADS * LANES]
        out['dg'] = gates - bxe[:, ML_HEADS * LANES:]
        out['qr'], out['kr'] = [], []
        for p in range(PAIRS):
            lanes = slice(p * LANES, (p + 1) * LANES)
            cos = cos_ref[rs, lanes]
            sin = sin_ref[rs, lanes]
            for src, dst in ((Z_RQ, 'qr'), (Z_RK, 'kr')):
                t = z_ref[rs, src + p * LANES:src + (p + 1) * LANES]
                sw = jnp.where(lo32, pltpu.roll(t, LANES - 32, 1), pltpu.roll(t, 32, 1))
                out[dst].append(t * cos + sw * sin)
        return out

    def finish(g, raw):
        rs = slice(g * group, (g + 1) * group)
        for k in range((RET_W + ML_W) // GLA_W):
            o = jnp.concatenate(raw[2 * k:2 * k + 2], axis=1)
            oc = o - _dot(o, on256)
            on = oc * lax.rsqrt(_dot(oc * oc, on256) + EPS)
            for j in range(GLA_W // LANES):
                t = (GLA_W // LANES) * k + j
                tile = on[:, j * LANES:(j + 1) * LANES]
                if t < PAIRS:
                    tile = (tile * rnorm_ref[:, t * LANES:(t + 1) * LANES]
                            * _silu(z_ref[rs, Z_RG + t * LANES:Z_RG + (t + 1) * LANES]))
                else:
                    tile = tile * mnorm_ref[:, (t - PAIRS) * LANES:(t - PAIRS + 1) * LANES]
                mix_ref[rs, MIX_RET + t * LANES:MIX_RET + (t + 1) * LANES] = tile

    per_group = group // c
    n_chunks = block // c

    def first_dots(ci, pre, lr):
        rows = slice(ci * c, (ci + 1) * c)
        d = {}
        b = pre['bg'][lr]
        bmid = b[c // 2 - 1:c // 2, :]
        bl = b[c - 1:c, :]
        gq = z_ref[rows, Z_GQ:Z_GQ + GLA_KW] * (GLA_DK ** -0.5)
        gk = z_ref[rows, Z_GK:Z_GK + GLA_KW]
        eq = jnp.exp(b - bmid)
        ek = jnp.exp(bmid - b)
        qh = gq * eq
        kh = gk * ek
        fac = jnp.maximum(jnp.maximum(jnp.abs(qh), jnp.abs(kh)), jnp.maximum(eq, ek))
        d['fac'] = functools.reduce(jnp.maximum, [fac[i:i + 8] for i in range(0, c, 8)])
        qh = qh.astype(BF16)
        d['qe'] = (gq * jnp.exp(b)).astype(BF16)
        kl_t = (gk * jnp.exp(bl - b)).T
        lhs = jnp.concatenate([qh * gla_mask[h] for h in range(GLA_HEADS)], axis=0)
        a = lax.dot_general(lhs, kh.astype(BF16), _NT, preferred_element_type=F32)
        d['a'] = jnp.where(causal4, a, 0.0).astype(BF16)
        d['gvb'] = z_ref[rows, Z_GV:Z_GV + GLA_W].astype(BF16)
        d['gupd'] = _dot(kl_t, d['gvb'])
        d['bl_col'] = b[c - 8:c, :].T[:, 7:8]
        d['rvb'] = z_ref[rows, Z_RV:Z_RV + RET_W].astype(BF16)
        kr = jnp.concatenate([t[lr] for t in pre['kr']], axis=1)
        d['rpm'] = [(lax.dot_general(stack_pair(pre['qr'][p][lr].astype(BF16)),
                                     kr[:, p * LANES:(p + 1) * LANES].astype(BF16), _NT,
                                     preferred_element_type=F32) * rdec_ref[p]).astype(BF16)
                    for p in range(PAIRS)]
        d['rupd'] = _dot(kr.T * rwend_ref[...], d['rvb'])
        d['mvb'] = z_ref[rows, Z_MV:Z_MV + ML_W].astype(BF16)
        km = pre['km'][lr]
        d['km_t'] = km.T
        d['ms2'] = [lax.dot_general(stack_pair(pre['qm'][lr, p * LANES:(p + 1) * LANES].astype(BF16)),
                                    km[:, p * LANES:(p + 1) * LANES].astype(BF16), _NT,
                                    preferred_element_type=F32) for p in range(PAIRS)]
        dg_t = pre['dg'][lr].T
        d['dg_t'] = dg_t
        d['mrow'] = [jnp.max(jnp.where(causal, pre['bx'][lr, h * LANES:h * LANES + c]
                                       + dg_t[SM_IG + h:SM_IG + h + 1, :], neg_inf), axis=1, keepdims=True)
                     for h in range(ML_HEADS)]
        return d

    def second_dots(ci, pre, lr, d):
        rows = slice(ci * c, (ci + 1) * c)
        raw = []
        a = d['a']
        o = jnp.dot(jnp.concatenate([a[h * c:(h + 1) * c] for h in range(GLA_HEADS)] + [d['qe']], axis=1),
                    jnp.concatenate([d['gvb'] * gv_mask[h] for h in range(GLA_HEADS)]
                                    + [gst_ref[...].astype(BF16)], axis=0),
                    preferred_element_type=F32)
        mix_ref[rows, 0:GLA_W] = o
        gst_ref[...] = gst_ref[...] * jnp.exp(d['bl_col']) + bm_gk * d['gupd']
        for p in range(PAIRS):
            lanes = slice(p * LANES, (p + 1) * LANES)
            vb = d['rvb'][:, lanes]
            pm = d['rpm'][p]
            raw.append(jnp.dot(
                jnp.concatenate([pm[:c], pm[c:], (pre['qr'][p][lr] * rint_ref[:, lanes]).astype(BF16)], axis=1),
                jnp.concatenate([vb * pair_mask[0], vb * pair_mask[1], sst_ref[p].astype(BF16)], axis=0),
                preferred_element_type=F32))
            sst_ref[p] = sst_ref[p] * rgam_ref[:, lanes] + bm_pair * d['rupd'][lanes, lanes]
        dg_t = d['dg_t']
        wls, w0ls = [], []
        for p in range(PAIRS):
            lanes = slice(p * LANES, (p + 1) * LANES)
            wd, w0, mt = [], [], []
            for a_ in range(2):
                h = 2 * p + a_
                colr = pre['bx'][lr, h * LANES:(h + 1) * LANES]
                logd = jnp.where(causal, colr[:, :c] + dg_t[SM_IG + h:SM_IG + h + 1, :], neg_inf)
                log0 = colr + mst_ref[:, h * LANES:(h + 1) * LANES]
                m_t = jnp.maximum(log0, d['mrow'][h])
                wd.append(jnp.exp(logd - m_t[:, :c]))
                w0.append(jnp.exp(log0 - m_t))
                mt.append(m_t)
                mst_ref[:, h * LANES:(h + 1) * LANES] = m_t[c - 1:c, :]
            qp = pre['qm'][lr, lanes]
            vb = d['mvb'][:, lanes]
            a0 = (d['ms2'][p][:c] * wd[0]).astype(BF16)
            a1 = (d['ms2'][p][c:] * wd[1]).astype(BF16)
            w0p = jnp.where(lane_lo, w0[0], w0[1])
            mtp = jnp.where(lane_lo, mt[0], mt[1])
            nd = jnp.dot(
                jnp.concatenate([a0, a1, (qp * w0p).astype(BF16)], axis=1),
                jnp.concatenate([jnp.concatenate([vb * pair_mask[0], pair_mask[0]], axis=1),
                                 jnp.concatenate([vb * pair_mask[1], pair_mask[1]], axis=1),
                                 cst_ref[p].astype(BF16)], axis=0),
                preferred_element_type=F32)
            hh = nd[:, :LANES] / jnp.maximum(jnp.abs(nd[:, LANES:]), jnp.exp(-mtp))
            raw.append(hh * _sigmoid(z_ref[rows, Z_MO + p * LANES:Z_MO + (p + 1) * LANES]))
            wls.append(jnp.where(sub_lo, wd[0][c - 1:c, :], wd[1][c - 1:c, :]))
            w0l = w0p[c - 1:c, :]
            w0ls.append(jnp.concatenate([w0l, w0l], axis=1))
        upd = _dot(d['km_t'] * jnp.concatenate(wls, axis=0), jnp.concatenate([d['mvb'], ones_b], axis=1))
        for p in range(PAIRS):
            lanes = slice(p * LANES, (p + 1) * LANES)
            cst_ref[p] = cst_ref[p] * w0ls[p] + bm_aug * jnp.concatenate(
                [upd[lanes, lanes], upd[lanes, ML_W:]], axis=1)
        return raw

    preps = {0: prep(0)}
    raws = {}
    fac_max = jnp.zeros((8, LANES), F32)
    place = lambda ci: (ci // per_group, slice((ci % per_group) * c, (ci % per_group + 1) * c))
    firsts = {0: first_dots(0, preps[0], place(0)[1])}
    for ci in range(n_chunks):
        g, lr = place(ci)
        if ci % per_group == 0 and g > 0:
            finish(g - 1, [jnp.concatenate(t, axis=0) for t in zip(*raws.pop(g - 1))])
        if ci + 1 < n_chunks:
            g1, lr1 = place(ci + 1)
            if g1 not in preps:
                preps[g1] = prep(g1)
            firsts[ci + 1] = first_dots(ci + 1, preps[g1], lr1)
        d = firsts.pop(ci)
        fac_max = jnp.maximum(fac_max, d['fac'])
        raws.setdefault(g, []).append(second_dots(ci, preps[g], lr, d))

    last_g = block // group - 1
    finish(last_g, [jnp.concatenate(t, axis=0) for t in zip(*raws.pop(last_g))])
    conv_tail = xp_ref[block + conv_lo:block + 8, :]
    xp_ref[conv_lo:8, :] = conv_tail

    @pl.when(jnp.logical_not(jnp.max(fac_max) < GLA_FACTOR_LIMIT))
    def _gla_per_token():
        gst_ref[...] = gprev_ref[...]

        def tokens8(g, carry):
            rows8 = pl.ds(pl.multiple_of(g * 8, 8), 8)
            alpha_t = jnp.exp(la_ref[rows8, :]).T
            k_t = z_ref[rows8, Z_GK:Z_GK + GLA_KW].T
            q_t = (z_ref[rows8, Z_GQ:Z_GQ + GLA_KW] * (GLA_DK ** -0.5)).T
            v8 = z_ref[rows8, Z_GV:Z_GV + GLA_W]
            s = gst_ref[...]
            outs = []
            for j in range(8):
                s = s * alpha_t[:, j:j + 1] + bm_gk * (k_t[:, j:j + 1] * v8[j:j + 1, :])
                outs.append(jnp.sum(q_t[:, j:j + 1] * s, axis=0, keepdims=True))
            gst_ref[...] = s
            mix_ref[rows8, 0:GLA_W] = jnp.concatenate(outs, axis=0)
            return carry

        lax.fori_loop(0, block // 8, tokens8, 0)

    o = mix_ref[:, 0:GLA_W]
    ms = _dot(o * o, on256)
    mix_ref[:, 0:GLA_W] = o * lax.rsqrt(ms + EPS) * gnorm_ref[...] * _silu(z_ref[:, Z_GG:Z_GG + GLA_W])

    @pl.when(step == last)
    def _finish():
        for h in range(GLA_HEADS):
            gout_ref[h] = gst_ref[h * GLA_DK:(h + 1) * GLA_DK, h * GLA_DV:(h + 1) * GLA_DV]
        for h in range(RET_HEADS):
            p, a = divmod(h, 2)
            rout_ref[h] = sst_ref[p, a * hd:(a + 1) * hd, a * hd:(a + 1) * hd]
            cout_ref[h] = cst_ref[p, a * hd:(a + 1) * hd, a * hd:(a + 1) * hd]
            nout_ref[h] = cst_ref[p, a * hd:(a + 1) * hd, LANES + a * hd:LANES + (a + 1) * hd]
        mout_ref[...] = mst_ref[...]
        cvout_ref[...] = conv_tail


def _mixer_constants(block, chunk):
    t = np.arange(block)
    tril = ((t[:, None] // chunk == t[None, :] // chunk) & (t[None, :] <= t[:, None])).astype(np.float32)
    r = np.arange(2 * LANES)[:, None] % LANES
    col = np.arange(ML_HEADS * LANES)[None, :] // LANES
    lane = np.arange(LANES)[None, :]
    ind = np.concatenate([r == col + SM_FG, (r == lane + ML_HEADS) & (lane >= SM_IG) & (lane < SM_FG)],
                         axis=1).astype(np.float32)
    i256 = np.arange(GLA_W) // GLA_DV
    on256 = (i256[:, None] == i256[None, :]).astype(np.float32) / GLA_DV
    return tuple(jnp.asarray(m, BF16) for m in (tril, ind, on256))


def _mixer(z3, tabs, wts, init, layer):
    bsz, seq, _ = z3.shape
    block = min(MIXER_BLOCK, seq)
    chunk = min(MIXER_CHUNK, block)
    has_init = init is not None
    cos, sin, rdec, rint, rwend, rgam = tabs
    consts = _mixer_constants(min(MIXER_GROUP, block), chunk)

    def const(shape):
        nd = len(shape)
        return pl.BlockSpec(shape, lambda b, s: (0,) * nd)

    def per_layer(shape):
        nd = len(shape)
        return pl.BlockSpec((None,) + shape, lambda b, s: (layer,) + (0,) * nd)

    def per_batch(shape, with_layer):
        nd = len(shape)
        if with_layer:
            return pl.BlockSpec((None, None) + shape, lambda b, s: (layer, b) + (0,) * nd)
        return pl.BlockSpec((None,) + shape, lambda b, s: (b,) + (0,) * nd)

    state_shapes = [(GLA_HEADS, GLA_DK, GLA_DV), (RET_HEADS, RET_D, RET_D), (ML_HEADS, ML_D, ML_D),
                    (ML_HEADS, ML_D, ML_D), (1, ML_HEADS * LANES), (CONV_W - 1, ML_W)]
    in_specs = [
        pl.BlockSpec((None, block, Z_W), lambda b, s: (b, s, 0)),
        pl.BlockSpec((block, RET_W), lambda b, s: (s, 0)),
        pl.BlockSpec((block, RET_W), lambda b, s: (s, 0)),
        const(rdec.shape), const(rint.shape), const(rwend.shape), const(rgam.shape),
    ] + [const(m.shape) for m in consts] + [
        per_layer((LANES, GLA_KW)), per_layer((1, GLA_KW)), per_layer((1, LANES)),
        per_layer((1, GLA_W)), per_layer((1, RET_W)), per_layer((1, ML_W)),
        per_layer((CONV_W, ML_W)), per_layer((1, ML_W)), per_layer((ML_W, 2 * ML_W)),
    ]
    args = [z3, cos, sin, rdec, rint, rwend, rgam] + list(consts) + list(wts)
    if has_init:
        in_specs += [per_batch(s, True) for s in state_shapes]
        args += list(init)
    out_specs = [pl.BlockSpec((None, block, D_MODEL), lambda b, s: (b, s, 0))]
    out_specs += [per_batch(s, False) for s in state_shapes]
    out_shape = [jax.ShapeDtypeStruct((bsz, seq, D_MODEL), F32)]
    out_shape += [jax.ShapeDtypeStruct((bsz,) + s, F32) for s in state_shapes]
    scratch = [
        pltpu.VMEM((GLA_KW, GLA_W), F32), pltpu.VMEM((PAIRS, LANES, LANES), F32),
        pltpu.VMEM((PAIRS, LANES, 2 * LANES), F32), pltpu.VMEM((1, ML_HEADS * LANES), F32),
        pltpu.VMEM((block + 8, ML_W), F32), pltpu.VMEM((block, LANES), F32),
        pltpu.VMEM((GLA_KW, GLA_W), F32),
    ]
    return pl.pallas_call(
        functools.partial(_mixer_kernel, block=block, chunk=chunk, has_init=has_init),
        grid=(bsz, seq // block),
        in_specs=in_specs,
        out_specs=out_specs,
        out_shape=out_shape,
        scratch_shapes=scratch,
        compiler_params=pltpu.CompilerParams(dimension_semantics=("parallel", "arbitrary"),
                                             vmem_limit_bytes=VMEM_LIMIT),
        name="mixer",
    )(*args)


def _ffn_kernel(*refs, final):
    if final:
        x_ref, mix_ref, wo_ref, g_ref, wg_ref, wu_ref, wd_ref, gf_ref, out_ref, y_ref = refs
    else:
        x_ref, mix_ref, wo_ref, g_ref, wg_ref, wu_ref, wd_ref, out_ref = refs
    x1 = x_ref[...] + jnp.dot(mix_ref[...].astype(BF16), wo_ref[...], preferred_element_type=F32)
    r = lax.rsqrt(jnp.mean(x1 * x1, axis=-1, keepdims=True) + EPS)
    hf = (x1 * r * g_ref[...]).astype(BF16)
    acc = x1
    for c in range(D_FF // FF_CHUNK):
        cols = slice(c * FF_CHUNK, (c + 1) * FF_CHUNK)
        gate = jnp.dot(hf, wg_ref[:, cols], preferred_element_type=F32)
        up = jnp.dot(hf, wu_ref[:, cols], preferred_element_type=F32)
        act = (_silu(gate) * up).astype(BF16)
        acc = acc + jnp.dot(act, wd_ref[cols, :], preferred_element_type=F32)
    out_ref[...] = acc
    if final:
        rf = lax.rsqrt(jnp.mean(acc * acc, axis=-1, keepdims=True) + EPS)
        y_ref[...] = acc * rf * gf_ref[...]


def _ffn(x2d, mix2d, wo, norm3, wg, wu, wd, layer, norm_final2):
    n = x2d.shape[0]
    tm = min(TOKEN_BLOCK, n)
    final = norm_final2 is not None

    def weight(shape):
        return pl.BlockSpec((None,) + shape, lambda i: (layer, 0, 0), pipeline_mode=pl.Buffered(1))

    row_spec = pl.BlockSpec((tm, D_MODEL), lambda i: (i, 0))
    in_specs = [row_spec, row_spec, weight((D_MODEL, D_MODEL)),
                pl.BlockSpec((None, 1, D_MODEL), lambda i: (layer, 0, 0)),
                weight((D_MODEL, D_FF)), weight((D_MODEL, D_FF)), weight((D_FF, D_MODEL))]
    args = [x2d, mix2d, wo, norm3, wg, wu, wd]
    out_specs = [row_spec]
    out_shape = [jax.ShapeDtypeStruct((n, D_MODEL), F32)]
    if final:
        in_specs.append(pl.BlockSpec((1, D_MODEL), lambda i: (0, 0)))
        args.append(norm_final2)
        out_specs.append(row_spec)
        out_shape.append(jax.ShapeDtypeStruct((n, D_MODEL), F32))
    return pl.pallas_call(
        functools.partial(_ffn_kernel, final=final),
        grid=(n // tm,),
        in_specs=in_specs,
        out_specs=out_specs,
        out_shape=out_shape,
        compiler_params=pltpu.CompilerParams(dimension_semantics=("parallel",),
                                             vmem_limit_bytes=VMEM_LIMIT),
        name="ffn",
    )(*args)


def _reorder_w_in(w_in):
    offs = [0]
    for sz in SPLIT_SIZES:
        offs.append(offs[-1] + sz)
    seg = [w_in[..., offs[i]:offs[i + 1]] for i in range(len(SPLIT_SIZES))]
    gq, gk, gv, ga, gg, rq, rk, rv, rg, mu, mv, mo, mi, mf = seg
    pad = jnp.zeros(w_in.shape[:-1] + (LANES - GLA_RANK - 2 * ML_HEADS,), w_in.dtype)
    rk = rk * (RET_D ** -0.5)
    return jnp.concatenate([gq, gk, gv, gg, rq, rk, rv, rg, mu, mv, mo, ga, mi, mf, pad],
                           axis=-1).astype(BF16)


def _block_diag_heads(w):
    depth, h, d, e = w.shape
    eye = jnp.eye(h, dtype=w.dtype)
    return jnp.einsum('lhde,hg->lhdge', w, eye).reshape(depth, h * d, h * e)


def _rope_tables(pos):
    inv = ROPE_BASE ** (-jnp.arange(0, RET_D, 2, dtype=F32) / RET_D)
    ang = pos.astype(F32)[:, None] * inv[None, :]
    cos, sin = jnp.cos(ang), jnp.sin(ang)
    cos_h = jnp.concatenate([cos, cos], axis=-1)
    sin_h = jnp.concatenate([-sin, sin], axis=-1)
    return jnp.tile(cos_h, (1, RET_HEADS)), jnp.tile(sin_h, (1, RET_HEADS))


def _retention_tables(chunk):
    lg = jnp.log1p(-jnp.exp2(-5.0 - jnp.arange(RET_HEADS, dtype=F32)))
    t = jnp.arange(chunk, dtype=F32)
    diff = t[:, None] - t[None, :]
    decay = jnp.where((diff >= 0)[..., None], jnp.exp(jnp.maximum(diff, 0.0)[..., None] * lg), 0.0)
    decay = jnp.transpose(decay, (2, 0, 1)).reshape(PAIRS, 2 * chunk, chunk)
    inter = jnp.repeat(jnp.exp((t[:, None] + 1.0) * lg), RET_D, axis=1)
    w_end = jnp.exp((chunk - 1.0 - t)[None, :] * lg[:, None])
    w_end = jnp.repeat(w_end, RET_D, axis=0)
    gam = jnp.repeat(jnp.exp(chunk * lg), RET_D)[None, :]
    return decay, inter, w_end, gam


def kernel(x_prompt, x_sample, state_gla, state_ret, state_mlstm_c, state_mlstm_n, state_mlstm_m,
           cache_mlstm_conv, norm_mix, w_in, gla_w_a2, gla_b_a, gla_norm, ret_norm, ml_conv_w,
           ml_conv_b, ml_wq, ml_wk, ml_b_if, ml_norm, w_out, norm_ffn, w_gate, w_up, w_down, norm_final):
    depth = w_in.shape[0]

    w_in_p = _reorder_w_in(w_in)
    wo, wg, wu, wd = (w.astype(BF16) for w in (w_out, w_gate, w_up, w_down))
    wa2 = jnp.zeros((depth, LANES, GLA_KW), F32).at[:, :GLA_RANK, :].set(gla_w_a2).astype(BF16)
    bif = jnp.zeros((depth, 1, LANES), F32).at[:, 0, SM_IG:SM_IG + 2 * ML_HEADS].set(ml_b_if)
    wqk = jnp.concatenate([_block_diag_heads(ml_wq), _block_diag_heads(ml_wk) * (ML_D ** -0.5)],
                          axis=-1).astype(BF16)
    row = lambda a: a[:, None, :]
    wts = (wa2, row(gla_b_a), bif, row(gla_norm), row(ret_norm), row(ml_norm), ml_conv_w, row(ml_conv_b), wqk)
    norm_mix3, norm_ffn3, norm_final2 = row(norm_mix), row(norm_ffn), norm_final[None, :]

    init_sample = (state_gla, state_ret, state_mlstm_c,
                   jnp.broadcast_to(state_mlstm_n[..., None], state_mlstm_n.shape + (ML_D,)),
                   jnp.repeat(state_mlstm_m, LANES, axis=-1)[:, :, None, :], cache_mlstm_conv)

    def run(x, pos, init):
        bsz, seq, _ = x.shape
        chunk = min(MIXER_CHUNK, MIXER_BLOCK, seq)
        assert seq % chunk == 0 and chunk & (chunk - 1) == 0 and chunk >= 2 * 8
        tabs = _rope_tables(pos) + _retention_tables(chunk)
        x2d = x.reshape(bsz * seq, D_MODEL)
        states = []
        y2d = None
        for i in range(depth):
            z = _inproj(x2d, norm_mix3, w_in_p, i)
            mixed, *st = _mixer(z.reshape(bsz, seq, Z_W), tabs, wts, init, i)
            outs = _ffn(x2d, mixed.reshape(bsz * seq, D_MODEL), wo, norm_ffn3, wg, wu, wd, i,
                        norm_final2 if i == depth - 1 else None)
            x2d = outs[0]
            if i == depth - 1:
                y2d = outs[1]
            states.append(st)
        g, r, c, n, m, cv = (jnp.stack([st[j] for st in states]) for j in range(6))
        return (y2d.reshape(bsz, seq, D_MODEL), g, r, c,
                n[..., 0], m[:, :, 0, ::LANES], cv)

    pos_p = jnp.arange(x_prompt.shape[1], dtype=jnp.int32)
    pos_s = PAST_LEN + jnp.arange(x_sample.shape[1], dtype=jnp.int32)
    yp, gp, rp, cp, np_, mp, cvp = run(x_prompt, pos_p, None)
    ys, gs, rs, cs, ns, ms, cvs = run(x_sample, pos_s, init_sample)
    return (yp, ys, gp, gs, rp, rs, cp, cs, np_, ns, mp, ms, cvp, cvs)
```

```python
import functools

import numpy as np
import jax
import jax.numpy as jnp
from jax import lax
from jax.experimental import pallas as pl
from jax.experimental.pallas import tpu as pltpu

F32 = jnp.float32
BF16 = jnp.bfloat16

D_MODEL = 1024
EPS = 1e-6
GLA_HEADS, GLA_DK, GLA_DV, GLA_RANK, GLA_TAU = 4, 32, 64, 16, 16.0
RET_HEADS, RET_D = 6, 64
ML_HEADS, ML_D = 6, 64
CONV_W = 4
ROPE_BASE = 10000.0
PAST_LEN = 2048
GLA_KW = GLA_HEADS * GLA_DK
GLA_W = GLA_HEADS * GLA_DV
RET_W = RET_HEADS * RET_D
ML_W = ML_HEADS * ML_D
D_FF = 2816
SPLIT_SIZES = (GLA_KW, GLA_KW, GLA_W, GLA_RANK, GLA_W, RET_W, RET_W, RET_W, RET_W, ML_W, ML_W, ML_W,
               ML_HEADS, ML_HEADS)

LANES = 128
PAIRS = RET_W // LANES
Z_GQ, Z_GK, Z_GV, Z_GG = 0, 128, 256, 512
Z_RQ, Z_RK, Z_RV, Z_RG = 768, 1152, 1536, 1920
Z_MU, Z_MV, Z_MO = 2304, 2688, 3072
Z_SM = 3456
Z_W = 3584
SM_IG, SM_FG = GLA_RANK, GLA_RANK + ML_HEADS
MIX_RET, MIX_ML = GLA_W, GLA_W + RET_W

TOKEN_BLOCK = 512
MIXER_BLOCK = 512
MIXER_CHUNK = 128
MIXER_GROUP = 256
FF_CHUNK = 1408
GLA_FACTOR_LIMIT = 1e18
VMEM_LIMIT = 56 * 1024 * 1024

_NT = (((1,), (1,)), ((), ()))


def _dot(a, b):
    return jnp.dot(a.astype(BF16), b.astype(BF16), preferred_element_type=F32)


def _dot_nt(a, b):
    return lax.dot_general(a.astype(BF16), b.astype(BF16), _NT, preferred_element_type=F32)


def _split2(x):
    hi = x.astype(BF16)
    return [hi, (x - hi.astype(F32)).astype(BF16)]


def _log_sigmoid(x):
    return jnp.minimum(x, 0.0) - jnp.log(1.0 + jnp.exp(-jnp.abs(x)))


def _sigmoid(x):
    return 0.5 + 0.5 * jnp.tanh(0.5 * x)


def _silu(x):
    h = 0.5 * x
    return h + h * jnp.tanh(h)


def _iota(shape, dim):
    return lax.broadcasted_iota(jnp.int32, shape, dim)


def _inproj_kernel(*refs, tm, has_init):
    if has_init:
        x_ref, g_ref, w_ref, cos_ref, sin_ref, cw_ref, cb_ref, cv0_ref, z_ref, cvout_ref, xp_ref = refs
    else:
        x_ref, g_ref, w_ref, cos_ref, sin_ref, cw_ref, cb_ref, z_ref, cvout_ref, xp_ref = refs
    conv_lo = 8 - (CONV_W - 1)

    @pl.when(pl.program_id(1) == 0)
    def _start_of_sequence():
        if has_init:
            xp_ref[conv_lo:8, :] = cv0_ref[...]
        else:
            xp_ref[conv_lo:8, :] = jnp.zeros((CONV_W - 1, ML_W), F32)

    x = x_ref[...]
    r = lax.rsqrt(jnp.mean(x * x, axis=-1, keepdims=True) + EPS)
    h = (x * r * g_ref[...]).astype(BF16)

    def proj(lo, hi):
        return jnp.dot(h, w_ref[:, lo:hi], preferred_element_type=F32)

    z_ref[:, Z_GQ:Z_RQ] = proj(Z_GQ, Z_RQ)
    seg = proj(Z_RQ, Z_RV)
    lo32 = (_iota((tm, LANES), 1) & 32) == 0
    for p in range(2 * PAIRS):
        lanes = slice((p % PAIRS) * LANES, (p % PAIRS + 1) * LANES)
        t = seg[:, p * LANES:(p + 1) * LANES]
        sw = jnp.where(lo32, pltpu.roll(t, LANES - 32, 1), pltpu.roll(t, 32, 1))
        z_ref[:, Z_RQ + p * LANES:Z_RQ + (p + 1) * LANES] = t * cos_ref[:, lanes] + sw * sin_ref[:, lanes]
    z_ref[:, Z_RV:Z_MU] = proj(Z_RV, Z_MU)
    seg = proj(Z_MU, Z_MO)
    z_ref[:, Z_MV:Z_MO] = seg[:, ML_W:]
    xp_ref[8:8 + tm, :] = seg[:, :ML_W]
    cw = cw_ref[...]
    conv = cb_ref[...]
    for j in range(CONV_W):
        conv = conv + xp_ref[conv_lo + j:conv_lo + j + tm, :] * cw[j:j + 1, :]
    z_ref[:, Z_MU:Z_MV] = _silu(conv)
    conv_tail = xp_ref[tm + conv_lo:tm + 8, :]
    xp_ref[conv_lo:8, :] = conv_tail
    cvout_ref[...] = conv_tail
    z_ref[:, Z_MO:Z_W] = proj(Z_MO, Z_W)


def _inproj(x3, norm3, w_in_p, rope, cw, cb, cv0, layer):
    bsz, seq, _ = x3.shape
    tm = min(TOKEN_BLOCK, seq)
    has_init = cv0 is not None
    cos, sin = rope
    in_specs = [
        pl.BlockSpec((None, tm, D_MODEL), lambda b, s: (b, s, 0)),
        pl.BlockSpec((None, 1, D_MODEL), lambda b, s: (layer, 0, 0)),
        pl.BlockSpec((None, D_MODEL, Z_W), lambda b, s: (layer, 0, 0), pipeline_mode=pl.Buffered(1)),
        pl.BlockSpec((tm, RET_W), lambda b, s: (s, 0)),
        pl.BlockSpec((tm, RET_W), lambda b, s: (s, 0)),
        pl.BlockSpec((None, CONV_W, ML_W), lambda b, s: (layer, 0, 0)),
        pl.BlockSpec((None, 1, ML_W), lambda b, s: (layer, 0, 0)),
    ]
    args = [x3, norm3, w_in_p, cos, sin, cw, cb]
    if has_init:
        in_specs.append(pl.BlockSpec((None, None, CONV_W - 1, ML_W), lambda b, s: (layer, b, 0, 0)))
        args.append(cv0)
    return pl.pallas_call(
        functools.partial(_inproj_kernel, tm=tm, has_init=has_init),
        grid=(bsz, seq // tm),
        in_specs=in_specs,
        out_specs=[pl.BlockSpec((None, tm, Z_W), lambda b, s: (b, s, 0)),
                   pl.BlockSpec((None, CONV_W - 1, ML_W), lambda b, s: (b, 0, 0))],
        out_shape=[jax.ShapeDtypeStruct((bsz, seq, Z_W), F32),
                   jax.ShapeDtypeStruct((bsz, CONV_W - 1, ML_W), F32)],
        scratch_shapes=[pltpu.VMEM((tm + 8, ML_W), F32)],
        compiler_params=pltpu.CompilerParams(dimension_semantics=("parallel", "arbitrary"),
                                             vmem_limit_bytes=VMEM_LIMIT),
        name="inproj",
    )(*args)


def _mixer_kernel(*refs, block, chunk, has_init):
    it = iter(refs)
    z_ref, rdec_ref, rint_ref, rwend_ref, rgam_ref, tril_ref, ind_ref, on256_ref = (next(it) for _ in range(8))
    wa2_ref, ba_ref, bif_ref, gnorm_ref, rnorm_ref, mnorm_ref, wqk_ref = (next(it) for _ in range(7))
    if has_init:
        g0_ref, r0_ref, c0_ref, n0_ref, m0_ref = (next(it) for _ in range(5))
    mix_ref, gout_ref, rout_ref, cout_ref, nout_ref, mout_ref = (next(it) for _ in range(6))
    gst_ref, sst_ref, cst_ref, mst_ref, la_ref, gprev_ref = (next(it) for _ in range(6))

    step = pl.program_id(1)
    last = pl.num_programs(1) - 1
    c = chunk
    hd = RET_D
    group = min(MIXER_GROUP, block)

    @pl.when(step == 0)
    def _init():
        gst_ref[...] = jnp.zeros_like(gst_ref)
        sst_ref[...] = jnp.zeros_like(sst_ref)
        cst_ref[...] = jnp.zeros_like(cst_ref)
        if has_init:
            for h in range(GLA_HEADS):
                gst_ref[h * GLA_DK:(h + 1) * GLA_DK, h * GLA_DV:(h + 1) * GLA_DV] = g0_ref[h]
            for h in range(RET_HEADS):
                p, a = divmod(h, 2)
                sst_ref[p, a * hd:(a + 1) * hd, a * hd:(a + 1) * hd] = r0_ref[h]
                cst_ref[p, a * hd:(a + 1) * hd, a * hd:(a + 1) * hd] = c0_ref[h]
                cst_ref[p, a * hd:(a + 1) * hd, LANES + a * hd:LANES + (a + 1) * hd] = n0_ref[h]
            mst_ref[...] = m0_ref[...]
        else:
            mst_ref[...] = jnp.zeros_like(mst_ref)

    lane_c = _iota((c, LANES), 1)
    lane_lo = (lane_c & hd) == 0
    pair_mask = [jnp.where(lane_lo, 1.0, 0.0).astype(BF16), jnp.where(lane_lo, 0.0, 1.0).astype(BF16)]
    gla_mask = [jnp.where((lane_c >> 5) == h, 1.0, 0.0).astype(BF16) for h in range(GLA_HEADS)]
    gv_mask = [jnp.where((_iota((c, GLA_W), 1) >> 6) == h, 1.0, 0.0).astype(BF16) for h in range(GLA_HEADS)]
    ones_b = jnp.ones((c, LANES), BF16)
    neg_inf = jnp.float32(-jnp.inf)
    causal = _iota((c, c), 1) <= _iota((c, c), 0)
    causal4 = _iota((GLA_HEADS * c, c), 1) <= (_iota((GLA_HEADS * c, c), 0) & (c - 1))
    bm_gk = jnp.where((_iota((GLA_KW, GLA_W), 0) >> 5) == (_iota((GLA_KW, GLA_W), 1) >> 6), 1.0, 0.0)
    bm_pair = jnp.where((_iota((LANES, LANES), 0) >> 6) == (_iota((LANES, LANES), 1) >> 6), 1.0, 0.0)
    bm_aug = jnp.concatenate([bm_pair, bm_pair], axis=1)
    sub_lo = _iota((LANES, c), 0) < hd

    def stack_pair(xb):
        return jnp.concatenate([xb * pair_mask[0], xb * pair_mask[1]], axis=0)

    gprev_ref[...] = gst_ref[...]
    on256 = on256_ref[...]

    def prep(g):
        rs = slice(g * group, (g + 1) * group)
        qk = _dot(z_ref[rs, Z_MU:Z_MU + ML_W], wqk_ref[...])
        out = {'qm': qk[:, :ML_W], 'km': qk[:, ML_W:]}
        small = z_ref[rs, Z_SM:Z_SM + LANES]
        la = _log_sigmoid(_dot(small, wa2_ref[...]) + ba_ref[...]) * (1.0 / GLA_TAU)
        gates = small + bif_ref[...]
        lf = _log_sigmoid(gates)
        yield
        cum = jnp.dot(tril_ref[...], jnp.concatenate(_split2(la) + _split2(lf), axis=1),
                      preferred_element_type=F32)
        out['bg'] = cum[:, :LANES] + cum[:, LANES:2 * LANES]
        la_ref[rs, :] = la
        bt = cum[:, 2 * LANES:3 * LANES] + cum[:, 3 * LANES:]
        yield
        bxe = jnp.dot(jnp.concatenate(_split2(bt), axis=1), ind_ref[...], preferred_element_type=F32)
        out['bx'] = bxe[:, :ML_HEADS * LANES]
        out['dg'] = gates - bxe[:, ML_HEADS * LANES:]
        preps[g] = out

    def finish(g):
        rs = slice(g * group, (g + 1) * group)
        raw = [jnp.concatenate(t, axis=0) for t in zip(*raws.pop(g))]
        n_col = (RET_W + ML_W) // GLA_W
        os_ = [jnp.concatenate(raw[2 * k:2 * k + 2], axis=1) for k in range(n_col)]
        ocs = [o - _dot(o, on256) for o in os_]
        yield
        ons = [oc * lax.rsqrt(_dot(oc * oc, on256) + EPS) for oc in ocs]
        yield
        for k in range(n_col):
            on = ons[k]
            for j in range(GLA_W // LANES):
                t = (GLA_W // LANES) * k + j
                tile = on[:, j * LANES:(j + 1) * LANES]
                if t < PAIRS:
                    tile = (tile * rnorm_ref[:, t * LANES:(t + 1) * LANES]
                            * _silu(z_ref[rs, Z_RG + t * LANES:Z_RG + (t + 1) * LANES]))
                else:
                    tile = tile * mnorm_ref[:, (t - PAIRS) * LANES:(t - PAIRS + 1) * LANES]
                mix_ref[rs, MIX_RET + t * LANES:MIX_RET + (t + 1) * LANES] = tile

    per_group = group // c
    n_chunks = block // c

    def first_dots(ci, pre, lr):
        rows = slice(ci * c, (ci + 1) * c)
        d = {}
        b = pre['bg'][lr]
        bmid = b[c // 2 - 1:c // 2, :]
        bl = b[c - 1:c, :]
        gq = z_ref[rows, Z_GQ:Z_GQ + GLA_KW] * (GLA_DK ** -0.5)
        gk = z_ref[rows, Z_GK:Z_GK + GLA_KW]
        eq = jnp.exp(b - bmid)
        ek = jnp.exp(bmid - b)
        qh = gq * eq
        kh = gk * ek
        fac = jnp.maximum(jnp.maximum(jnp.abs(qh), jnp.abs(kh)), jnp.maximum(eq, ek))
        d['fac'] = functools.reduce(jnp.maximum, [fac[i:i + 8] for i in range(0, c, 8)])
        qh = qh.astype(BF16)
        d['qe'] = (gq * jnp.exp(b)).astype(BF16)
        kl_t = (gk * jnp.exp(bl - b)).T
        lhs = jnp.concatenate([qh * gla_mask[h] for h in range(GLA_HEADS)], axis=0)
        a = lax.dot_general(lhs, kh.astype(BF16), _NT, preferred_element_type=F32)
        d['a'] = jnp.where(causal4, a, 0.0).astype(BF16)
        d['gvb'] = z_ref[rows, Z_GV:Z_GV + GLA_W].astype(BF16)
        d['gupd'] = _dot(kl_t, d['gvb'])
        d['bl_col'] = b[c - 8:c, :].T[:, 7:8]
        d['rvb'] = z_ref[rows, Z_RV:Z_RV + RET_W].astype(BF16)
        kr = z_ref[rows, Z_RK:Z_RK + RET_W]
        d['qr'] = z_ref[rows, Z_RQ:Z_RQ + RET_W]
        d['rpm'] = [(lax.dot_general(stack_pair(d['qr'][:, p * LANES:(p + 1) * LANES].astype(BF16)),
                                     kr[:, p * LANES:(p + 1) * LANES].astype(BF16), _NT,
                                     preferred_element_type=F32) * rdec_ref[p]).astype(BF16)
                    for p in range(PAIRS)]
        d['rupd'] = _dot(kr.T * rwend_ref[...], d['rvb'])
        d['mvb'] = z_ref[rows, Z_MV:Z_MV + ML_W].astype(BF16)
        km = pre['km'][lr]
        d['km_t'] = km.T
        d['ms2'] = [lax.dot_general(stack_pair(pre['qm'][lr, p * LANES:(p + 1) * LANES].astype(BF16)),
                                    km[:, p * LANES:(p + 1) * LANES].astype(BF16), _NT,
                                    preferred_element_type=F32) for p in range(PAIRS)]
        dg_t = pre['dg'][lr].T
        d['dg_t'] = dg_t
        d['mrow'] = [jnp.max(jnp.where(causal, pre['bx'][lr, h * LANES:h * LANES + c]
                                       + dg_t[SM_IG + h:SM_IG + h + 1, :], neg_inf), axis=1, keepdims=True)
                     for h in range(ML_HEADS)]
        return d

    def second_dots(ci, pre, lr, d):
        rows = slice(ci * c, (ci + 1) * c)
        raw = []
        a = d['a']
        o = jnp.dot(jnp.concatenate([a[h * c:(h + 1) * c] for h in range(GLA_HEADS)] + [d['qe']], axis=1),
                    jnp.concatenate([d['gvb'] * gv_mask[h] for h in range(GLA_HEADS)]
                                    + [gst_ref[...].astype(BF16)], axis=0),
                    preferred_element_type=F32)
        mix_ref[rows, 0:GLA_W] = o
        gst_ref[...] = gst_ref[...] * jnp.exp(d['bl_col']) + bm_gk * d['gupd']
        for p in range(PAIRS):
            lanes = slice(p * LANES, (p + 1) * LANES)
            vb = d['rvb'][:, lanes]
            pm = d['rpm'][p]
            raw.append(jnp.dot(
                jnp.concatenate([pm[:c], pm[c:], (d['qr'][:, lanes] * rint_ref[:, lanes]).astype(BF16)], axis=1),
                jnp.concatenate([vb * pair_mask[0], vb * pair_mask[1], sst_ref[p].astype(BF16)], axis=0),
                preferred_element_type=F32))
            sst_ref[p] = sst_ref[p] * rgam_ref[:, lanes] + bm_pair * d['rupd'][lanes, lanes]
        dg_t = d['dg_t']
        wls, w0ls = [], []
        for p in range(PAIRS):
            lanes = slice(p * LANES, (p + 1) * LANES)
            wd, w0, mt = [], [], []
            for a_ in range(2):
                h = 2 * p + a_
                colr = pre['bx'][lr, h * LANES:(h + 1) * LANES]
                logd = jnp.where(causal, colr[:, :c] + dg_t[SM_IG + h:SM_IG + h + 1, :], neg_inf)
                log0 = colr + mst_ref[:, h * LANES:(h + 1) * LANES]
                m_t = jnp.maximum(log0, d['mrow'][h])
                wd.append(jnp.exp(logd - m_t[:, :c]))
                w0.append(jnp.exp(log0 - m_t))
                mt.append(m_t)
                mst_ref[:, h * LANES:(h + 1) * LANES] = m_t[c - 1:c, :]
            qp = pre['qm'][lr, lanes]
            vb = d['mvb'][:, lanes]
            a0 = (d['ms2'][p][:c] * wd[0]).astype(BF16)
            a1 = (d['ms2'][p][c:] * wd[1]).astype(BF16)
            w0p = jnp.where(lane_lo, w0[0], w0[1])
            mtp = jnp.where(lane_lo, mt[0], mt[1])
            nd = jnp.dot(
                jnp.concatenate([a0, a1, (qp * w0p).astype(BF16)], axis=1),
                jnp.concatenate([jnp.concatenate([vb * pair_mask[0], pair_mask[0]], axis=1),
                                 jnp.concatenate([vb * pair_mask[1], pair_mask[1]], axis=1),
                                 cst_ref[p].astype(BF16)], axis=0),
                preferred_element_type=F32)
            hh = nd[:, :LANES] / jnp.maximum(jnp.abs(nd[:, LANES:]), jnp.exp(-mtp))
            raw.append(hh * _sigmoid(z_ref[rows, Z_MO + p * LANES:Z_MO + (p + 1) * LANES]))
            wls.append(jnp.where(sub_lo, wd[0][c - 1:c, :], wd[1][c - 1:c, :]))
            w0l = w0p[c - 1:c, :]
            w0ls.append(jnp.concatenate([w0l, w0l], axis=1))
        upd = _dot(d['km_t'] * jnp.concatenate(wls, axis=0), jnp.concatenate([d['mvb'], ones_b], axis=1))
        for p in range(PAIRS):
            lanes = slice(p * LANES, (p + 1) * LANES)
            cst_ref[p] = cst_ref[p] * w0ls[p] + bm_aug * jnp.concatenate(
                [upd[lanes, lanes], upd[lanes, ML_W:]], axis=1)
        return raw

    preps, raws, firsts, pending = {}, {}, {}, []

    def drain(gen):
        for _ in gen:
            pass

    def tick():
        pending[:] = [gen for gen in pending if next(gen, pending) is not pending]

    fac_max = jnp.zeros((8, LANES), F32)
    place = lambda ci: (ci // per_group, slice((ci % per_group) * c, (ci % per_group + 1) * c))
    drain(prep(0))
    firsts[0] = first_dots(0, preps[0], place(0)[1])
    for ci in range(n_chunks):
        g, lr = place(ci)
        if ci % per_group == 0:
            if g > 0:
                pending.append(finish(g - 1))
            if (g + 1) * group < block:
                pending.append(prep(g + 1))
        tick()
        if ci + 1 < n_chunks:
            g1, lr1 = place(ci + 1)
            while g1 not in preps:
                tick()
            firsts[ci + 1] = first_dots(ci + 1, preps[g1], lr1)
        tick()
        d = firsts.pop(ci)
        fac_max = jnp.maximum(fac_max, d['fac'])
        raws.setdefault(g, []).append(second_dots(ci, preps[g], lr, d))
        tick()
    while pending:
        tick()
    drain(finish(block // group - 1))


    @pl.when(jnp.logical_not(jnp.max(fac_max) < GLA_FACTOR_LIMIT))
    def _gla_per_token():
        gst_ref[...] = gprev_ref[...]

        def tokens8(g, carry):
            rows8 = pl.ds(pl.multiple_of(g * 8, 8), 8)
            alpha_t = jnp.exp(la_ref[rows8, :]).T
            k_t = z_ref[rows8, Z_GK:Z_GK + GLA_KW].T
            q_t = (z_ref[rows8, Z_GQ:Z_GQ + GLA_KW] * (GLA_DK ** -0.5)).T
            v8 = z_ref[rows8, Z_GV:Z_GV + GLA_W]
            s = gst_ref[...]
            outs = []
            for j in range(8):
                s = s * alpha_t[:, j:j + 1] + bm_gk * (k_t[:, j:j + 1] * v8[j:j + 1, :])
                outs.append(jnp.sum(q_t[:, j:j + 1] * s, axis=0, keepdims=True))
            gst_ref[...] = s
            mix_ref[rows8, 0:GLA_W] = jnp.concatenate(outs, axis=0)
            return carry

        lax.fori_loop(0, block // 8, tokens8, 0)

    o = mix_ref[:, 0:GLA_W]
    ms = _dot(o * o, on256)
    mix_ref[:, 0:GLA_W] = o * lax.rsqrt(ms + EPS) * gnorm_ref[...] * _silu(z_ref[:, Z_GG:Z_GG + GLA_W])

    @pl.when(step == last)
    def _finish():
        for h in range(GLA_HEADS):
            gout_ref[h] = gst_ref[h * GLA_DK:(h + 1) * GLA_DK, h * GLA_DV:(h + 1) * GLA_DV]
        for h in range(RET_HEADS):
            p, a = divmod(h, 2)
            rout_ref[h] = sst_ref[p, a * hd:(a + 1) * hd, a * hd:(a + 1) * hd]
            cout_ref[h] = cst_ref[p, a * hd:(a + 1) * hd, a * hd:(a + 1) * hd]
            nout_ref[h] = cst_ref[p, a * hd:(a + 1) * hd, LANES + a * hd:LANES + (a + 1) * hd]
        mout_ref[...] = mst_ref[...]


def _mixer_constants(block, chunk):
    t = np.arange(block)
    tril = ((t[:, None] // chunk == t[None, :] // chunk) & (t[None, :] <= t[:, None])).astype(np.float32)
    r = np.arange(2 * LANES)[:, None] % LANES
    col = np.arange(ML_HEADS * LANES)[None, :] // LANES
    lane = np.arange(LANES)[None, :]
    ind = np.concatenate([r == col + SM_FG, (r == lane + ML_HEADS) & (lane >= SM_IG) & (lane < SM_FG)],
                         axis=1).astype(np.float32)
    i256 = np.arange(GLA_W) // GLA_DV
    on256 = (i256[:, None] == i256[None, :]).astype(np.float32) / GLA_DV
    return tuple(jnp.asarray(m, BF16) for m in (tril, ind, on256))


def _mixer(z3, tabs, wts, init, layer):
    bsz, seq, _ = z3.shape
    block = min(MIXER_BLOCK, seq)
    chunk = min(MIXER_CHUNK, block)
    has_init = init is not None
    rdec, rint, rwend, rgam = tabs
    consts = _mixer_constants(min(MIXER_GROUP, block), chunk)

    def const(shape):
        nd = len(shape)
        return pl.BlockSpec(shape, lambda b, s: (0,) * nd)

    def per_layer(shape):
        nd = len(shape)
        return pl.BlockSpec((None,) + shape, lambda b, s: (layer,) + (0,) * nd)

    def per_batch(shape, with_layer):
        nd = len(shape)
        if with_layer:
            return pl.BlockSpec((None, None) + shape, lambda b, s: (layer, b) + (0,) * nd)
        return pl.BlockSpec((None,) + shape, lambda b, s: (b,) + (0,) * nd)

    state_shapes = [(GLA_HEADS, GLA_DK, GLA_DV), (RET_HEADS, RET_D, RET_D), (ML_HEADS, ML_D, ML_D),
                    (ML_HEADS, ML_D, ML_D), (1, ML_HEADS * LANES)]
    in_specs = [
        pl.BlockSpec((None, block, Z_W), lambda b, s: (b, s, 0)),
        const(rdec.shape), const(rint.shape), const(rwend.shape), const(rgam.shape),
    ] + [const(m.shape) for m in consts] + [
        per_layer((LANES, GLA_KW)), per_layer((1, GLA_KW)), per_layer((1, LANES)),
        per_layer((1, GLA_W)), per_layer((1, RET_W)), per_layer((1, ML_W)),
        per_layer((ML_W, 2 * ML_W)),
    ]
    args = [z3, rdec, rint, rwend, rgam] + list(consts) + list(wts)
    if has_init:
        in_specs += [per_batch(s, True) for s in state_shapes]
        args += list(init)
    out_specs = [pl.BlockSpec((None, block, D_MODEL), lambda b, s: (b, s, 0))]
    out_specs += [per_batch(s, False) for s in state_shapes]
    out_shape = [jax.ShapeDtypeStruct((bsz, seq, D_MODEL), F32)]
    out_shape += [jax.ShapeDtypeStruct((bsz,) + s, F32) for s in state_shapes]
    scratch = [
        pltpu.VMEM((GLA_KW, GLA_W), F32), pltpu.VMEM((PAIRS, LANES, LANES), F32),
        pltpu.VMEM((PAIRS, LANES, 2 * LANES), F32), pltpu.VMEM((1, ML_HEADS * LANES), F32),
        pltpu.VMEM((block, LANES), F32),
        pltpu.VMEM((GLA_KW, GLA_W), F32),
    ]
    return pl.pallas_call(
        functools.partial(_mixer_kernel, block=block, chunk=chunk, has_init=has_init),
        grid=(bsz, seq // block),
        in_specs=in_specs,
        out_specs=out_specs,
        out_shape=out_shape,
        scratch_shapes=scratch,
        compiler_params=pltpu.CompilerParams(dimension_semantics=("parallel", "arbitrary"),
                                             vmem_limit_bytes=VMEM_LIMIT),
        name="mixer",
    )(*args)


def _ffn_kernel(*refs, final):
    if final:
        x_ref, mix_ref, wo_ref, g_ref, wg_ref, wu_ref, wd_ref, gf_ref, out_ref, y_ref = refs
    else:
        x_ref, mix_ref, wo_ref, g_ref, wg_ref, wu_ref, wd_ref, out_ref = refs
    x1 = x_ref[...] + jnp.dot(mix_ref[...].astype(BF16), wo_ref[...], preferred_element_type=F32)
    r = lax.rsqrt(jnp.mean(x1 * x1, axis=-1, keepdims=True) + EPS)
    hf = (x1 * r * g_ref[...]).astype(BF16)
    acc = x1
    for c in range(D_FF // FF_CHUNK):
        cols = slice(c * FF_CHUNK, (c + 1) * FF_CHUNK)
        gate = jnp.dot(hf, wg_ref[:, cols], preferred_element_type=F32)
        up = jnp.dot(hf, wu_ref[:, cols], preferred_element_type=F32)
        act = (_silu(gate) * up).astype(BF16)
        acc = acc + jnp.dot(act, wd_ref[cols, :], preferred_element_type=F32)
    out_ref[...] = acc
    if final:
        rf = lax.rsqrt(jnp.mean(acc * acc, axis=-1, keepdims=True) + EPS)
        y_ref[...] = acc * rf * gf_ref[...]


def _ffn(x2d, mix2d, wo, norm3, wg, wu, wd, layer, norm_final2):
    n = x2d.shape[0]
    tm = min(TOKEN_BLOCK, n)
    final = norm_final2 is not None

    def weight(shape):
        return pl.BlockSpec((None,) + shape, lambda i: (layer, 0, 0), pipeline_mode=pl.Buffered(1))

    row_spec = pl.BlockSpec((tm, D_MODEL), lambda i: (i, 0))
    in_specs = [row_spec, row_spec, weight((D_MODEL, D_MODEL)),
                pl.BlockSpec((None, 1, D_MODEL), lambda i: (layer, 0, 0)),
                weight((D_MODEL, D_FF)), weight((D_MODEL, D_FF)), weight((D_FF, D_MODEL))]
    args = [x2d, mix2d, wo, norm3, wg, wu, wd]
    out_specs = [row_spec]
    out_shape = [jax.ShapeDtypeStruct((n, D_MODEL), F32)]
    if final:
        in_specs.append(pl.BlockSpec((1, D_MODEL), lambda i: (0, 0)))
        args.append(norm_final2)
        out_specs.append(row_spec)
        out_shape.append(jax.ShapeDtypeStruct((n, D_MODEL), F32))
    return pl.pallas_call(
        functools.partial(_ffn_kernel, final=final),
        grid=(n // tm,),
        in_specs=in_specs,
        out_specs=out_specs,
        out_shape=out_shape,
        compiler_params=pltpu.CompilerParams(dimension_semantics=("parallel",),
                                             vmem_limit_bytes=VMEM_LIMIT),
        name="ffn",
    )(*args)


def _reorder_w_in(w_in):
    offs = [0]
    for sz in SPLIT_SIZES:
        offs.append(offs[-1] + sz)
    seg = [w_in[..., offs[i]:offs[i + 1]] for i in range(len(SPLIT_SIZES))]
    gq, gk, gv, ga, gg, rq, rk, rv, rg, mu, mv, mo, mi, mf = seg
    pad = jnp.zeros(w_in.shape[:-1] + (LANES - GLA_RANK - 2 * ML_HEADS,), w_in.dtype)
    rk = rk * (RET_D ** -0.5)
    return jnp.concatenate([gq, gk, gv, gg, rq, rk, rv, rg, mu, mv, mo, ga, mi, mf, pad],
                           axis=-1).astype(BF16)


def _block_diag_heads(w):
    depth, h, d, e = w.shape
    eye = jnp.eye(h, dtype=w.dtype)
    return jnp.einsum('lhde,hg->lhdge', w, eye).reshape(depth, h * d, h * e)


def _rope_tables(pos):
    inv = ROPE_BASE ** (-jnp.arange(0, RET_D, 2, dtype=F32) / RET_D)
    ang = pos.astype(F32)[:, None] * inv[None, :]
    cos, sin = jnp.cos(ang), jnp.sin(ang)
    cos_h = jnp.concatenate([cos, cos], axis=-1)
    sin_h = jnp.concatenate([-sin, sin], axis=-1)
    return jnp.tile(cos_h, (1, RET_HEADS)), jnp.tile(sin_h, (1, RET_HEADS))


def _retention_tables(chunk):
    lg = jnp.log1p(-jnp.exp2(-5.0 - jnp.arange(RET_HEADS, dtype=F32)))
    t = jnp.arange(chunk, dtype=F32)
    diff = t[:, None] - t[None, :]
    decay = jnp.where((diff >= 0)[..., None], jnp.exp(jnp.maximum(diff, 0.0)[..., None] * lg), 0.0)
    decay = jnp.transpose(decay, (2, 0, 1)).reshape(PAIRS, 2 * chunk, chunk)
    inter = jnp.repeat(jnp.exp((t[:, None] + 1.0) * lg), RET_D, axis=1)
    w_end = jnp.exp((chunk - 1.0 - t)[None, :] * lg[:, None])
    w_end = jnp.repeat(w_end, RET_D, axis=0)
    gam = jnp.repeat(jnp.exp(chunk * lg), RET_D)[None, :]
    return decay, inter, w_end, gam


def kernel(x_prompt, x_sample, state_gla, state_ret, state_mlstm_c, state_mlstm_n, state_mlstm_m,
           cache_mlstm_conv, norm_mix, w_in, gla_w_a2, gla_b_a, gla_norm, ret_norm, ml_conv_w,
           ml_conv_b, ml_wq, ml_wk, ml_b_if, ml_norm, w_out, norm_ffn, w_gate, w_up, w_down, norm_final):
    depth = w_in.shape[0]

    w_in_p = _reorder_w_in(w_in)
    wo, wg, wu, wd = (w.astype(BF16) for w in (w_out, w_gate, w_up, w_down))
    wa2 = jnp.zeros((depth, LANES, GLA_KW), F32).at[:, :GLA_RANK, :].set(gla_w_a2).astype(BF16)
    bif = jnp.zeros((depth, 1, LANES), F32).at[:, 0, SM_IG:SM_IG + 2 * ML_HEADS].set(ml_b_if)
    wqk = jnp.concatenate([_block_diag_heads(ml_wq), _block_diag_heads(ml_wk) * (ML_D ** -0.5)],
                          axis=-1).astype(BF16)
    row = lambda a: a[:, None, :]
    wts = (wa2, row(gla_b_a), bif, row(gla_norm), row(ret_norm), row(ml_norm), wqk)
    conv_b3 = row(ml_conv_b)
    norm_mix3, norm_ffn3, norm_final2 = row(norm_mix), row(norm_ffn), norm_final[None, :]

    init_sample = (state_gla, state_ret, state_mlstm_c,
                   jnp.broadcast_to(state_mlstm_n[..., None], state_mlstm_n.shape + (ML_D,)),
                   jnp.repeat(state_mlstm_m, LANES, axis=-1)[:, :, None, :])

    def run(x, pos, init, conv_cache):
        bsz, seq, _ = x.shape
        chunk = min(MIXER_CHUNK, MIXER_BLOCK, seq)
        assert seq % chunk == 0 and chunk & (chunk - 1) == 0 and chunk >= 2 * 8
        rope = _rope_tables(pos)
        tabs = _retention_tables(chunk)
        x2d = x.reshape(bsz * seq, D_MODEL)
        states = []
        y2d = None
        for i in range(depth):
            z, cv = _inproj(x2d.reshape(bsz, seq, D_MODEL), norm_mix3, w_in_p, rope, ml_conv_w, conv_b3,
                            conv_cache, i)
            mixed, *st = _mixer(z, tabs, wts, init, i)
            outs = _ffn(x2d, mixed.reshape(bsz * seq, D_MODEL), wo, norm_ffn3, wg, wu, wd, i,
                        norm_final2 if i == depth - 1 else None)
            x2d = outs[0]
            if i == depth - 1:
                y2d = outs[1]
            states.append(st + [cv])
        g, r, c, n, m, cv = (jnp.stack([st[j] for st in states]) for j in range(6))
        return (y2d.reshape(bsz, seq, D_MODEL), g, r, c,
                n[..., 0], m[:, :, 0, ::LANES], cv)

    pos_p = jnp.arange(x_prompt.shape[1], dtype=jnp.int32)
    pos_s = PAST_LEN + jnp.arange(x_sample.shape[1], dtype=jnp.int32)
    yp, gp, rp, cp, np_, mp, cvp = run(x_prompt, pos_p, None, None)
    ys, gs, rs, cs, ns, ms, cvs = run(x_sample, pos_s, init_sample, cache_mlstm_conv)
    return (yp, ys, gp, gs, rp, rs, cp, cs, np_, ns, mp, ms, cvp, cvs)
```

```python
import functools

import numpy as np
import jax
import jax.numpy as jnp
from jax import lax
from jax.experimental import pallas as pl
from jax.experimental.pallas import tpu as pltpu

F32 = jnp.float32
BF16 = jnp.bfloat16

D_MODEL = 1024
EPS = 1e-6
GLA_HEADS, GLA_DK, GLA_DV, GLA_RANK, GLA_TAU = 4, 32, 64, 16, 16.0
RET_HEADS, RET_D = 6, 64
ML_HEADS, ML_D = 6, 64
CONV_W = 4
ROPE_BASE = 10000.0
PAST_LEN = 2048
GLA_KW = GLA_HEADS * GLA_DK
GLA_W = GLA_HEADS * GLA_DV
RET_W = RET_HEADS * RET_D
ML_W = ML_HEADS * ML_D
D_FF = 2816
SPLIT_SIZES = (GLA_KW, GLA_KW, GLA_W, GLA_RANK, GLA_W, RET_W, RET_W, RET_W, RET_W, ML_W, ML_W, ML_W,
               ML_HEADS, ML_HEADS)

LANES = 128
PAIRS = RET_W // LANES
Z_GQ, Z_GK, Z_GV, Z_GG = 0, 128, 256, 512
Z_RQ, Z_RK, Z_RV, Z_RG = 768, 1152, 1536, 1920
Z_MU, Z_MV, Z_MO = 2304, 2688, 3072
Z_SM = 3456
Z_W = 3584
SM_IG, SM_FG = GLA_RANK, GLA_RANK + ML_HEADS
MIX_RET, MIX_ML = GLA_W, GLA_W + RET_W

TOKEN_BLOCK = 512
MIXER_BLOCK = 1024
MIXER_CHUNK = 128
MIXER_GROUP = 256
FF_CHUNK = 1408
GLA_FACTOR_LIMIT = 1e18
VMEM_LIMIT = 56 * 1024 * 1024

_NT = (((1,), (1,)), ((), ()))


def _dot(a, b):
    return jnp.dot(a.astype(BF16), b.astype(BF16), preferred_element_type=F32)


def _dot_nt(a, b):
    return lax.dot_general(a.astype(BF16), b.astype(BF16), _NT, preferred_element_type=F32)


def _split2(x):
    hi = x.astype(BF16)
    return [hi, (x - hi.astype(F32)).astype(BF16)]


def _log_sigmoid(x):
    return jnp.minimum(x, 0.0) - jnp.log(1.0 + jnp.exp(-jnp.abs(x)))


def _sigmoid(x):
    return 0.5 + 0.5 * jnp.tanh(0.5 * x)


def _silu(x):
    h = 0.5 * x
    return h + h * jnp.tanh(h)


def _iota(shape, dim):
    return lax.broadcasted_iota(jnp.int32, shape, dim)


def _inproj_kernel(*refs, tm, has_init):
    if has_init:
        x_ref, g_ref, w_ref, cos_ref, sin_ref, cw_ref, cb_ref, cv0_ref, z_ref, cvout_ref, xp_ref = refs
    else:
        x_ref, g_ref, w_ref, cos_ref, sin_ref, cw_ref, cb_ref, z_ref, cvout_ref, xp_ref = refs
    conv_lo = 8 - (CONV_W - 1)

    @pl.when(pl.program_id(1) == 0)
    def _start_of_sequence():
        if has_init:
            xp_ref[conv_lo:8, :] = cv0_ref[...]
        else:
            xp_ref[conv_lo:8, :] = jnp.zeros((CONV_W - 1, ML_W), F32)

    x = x_ref[...]
    r = lax.rsqrt(jnp.mean(x * x, axis=-1, keepdims=True) + EPS)
    h = (x * r * g_ref[...]).astype(BF16)

    def proj(lo, hi):
        return jnp.dot(h, w_ref[:, lo:hi], preferred_element_type=F32)

    z_ref[:, Z_GQ:Z_RQ] = proj(Z_GQ, Z_RQ)
    seg = proj(Z_RQ, Z_RV)
    lo32 = (_iota((tm, LANES), 1) & 32) == 0
    for p in range(2 * PAIRS):
        lanes = slice((p % PAIRS) * LANES, (p % PAIRS + 1) * LANES)
        t = seg[:, p * LANES:(p + 1) * LANES]
        sw = jnp.where(lo32, pltpu.roll(t, LANES - 32, 1), pltpu.roll(t, 32, 1))
        z_ref[:, Z_RQ + p * LANES:Z_RQ + (p + 1) * LANES] = t * cos_ref[:, lanes] + sw * sin_ref[:, lanes]
    z_ref[:, Z_RV:Z_MU] = proj(Z_RV, Z_MU)
    seg = proj(Z_MU, Z_MO)
    z_ref[:, Z_MV:Z_MO] = seg[:, ML_W:]
    xp_ref[8:8 + tm, :] = seg[:, :ML_W]
    cw = cw_ref[...]
    conv = cb_ref[...]
    for j in range(CONV_W):
        conv = conv + xp_ref[conv_lo + j:conv_lo + j + tm, :] * cw[j:j + 1, :]
    z_ref[:, Z_MU:Z_MV] = _silu(conv)
    conv_tail = xp_ref[tm + conv_lo:tm + 8, :]
    xp_ref[conv_lo:8, :] = conv_tail
    cvout_ref[...] = conv_tail
    z_ref[:, Z_MO:Z_W] = proj(Z_MO, Z_W)


def _inproj(x3, norm3, w_in_p, rope, cw, cb, cv0, layer):
    bsz, seq, _ = x3.shape
    tm = min(TOKEN_BLOCK, seq)
    has_init = cv0 is not None
    cos, sin = rope
    in_specs = [
        pl.BlockSpec((None, tm, D_MODEL), lambda b, s: (b, s, 0)),
        pl.BlockSpec((None, 1, D_MODEL), lambda b, s: (layer, 0, 0)),
        pl.BlockSpec((None, D_MODEL, Z_W), lambda b, s: (layer, 0, 0), pipeline_mode=pl.Buffered(1)),
        pl.BlockSpec((tm, RET_W), lambda b, s: (s, 0)),
        pl.BlockSpec((tm, RET_W), lambda b, s: (s, 0)),
        pl.BlockSpec((None, CONV_W, ML_W), lambda b, s: (layer, 0, 0)),
        pl.BlockSpec((None, 1, ML_W), lambda b, s: (layer, 0, 0)),
    ]
    args = [x3, norm3, w_in_p, cos, sin, cw, cb]
    if has_init:
        in_specs.append(pl.BlockSpec((None, None, CONV_W - 1, ML_W), lambda b, s: (layer, b, 0, 0)))
        args.append(cv0)
    return pl.pallas_call(
        functools.partial(_inproj_kernel, tm=tm, has_init=has_init),
        grid=(bsz, seq // tm),
        in_specs=in_specs,
        out_specs=[pl.BlockSpec((None, tm, Z_W), lambda b, s: (b, s, 0)),
                   pl.BlockSpec((None, CONV_W - 1, ML_W), lambda b, s: (b, 0, 0))],
        out_shape=[jax.ShapeDtypeStruct((bsz, seq, Z_W), F32),
                   jax.ShapeDtypeStruct((bsz, CONV_W - 1, ML_W), F32)],
        scratch_shapes=[pltpu.VMEM((tm + 8, ML_W), F32)],
        compiler_params=pltpu.CompilerParams(dimension_semantics=("parallel", "arbitrary"),
                                             vmem_limit_bytes=VMEM_LIMIT),
        name="inproj",
    )(*args)


def _mixer_kernel(*refs, block, chunk, has_init):
    it = iter(refs)
    z_ref, rdec_ref, rint_ref, rwend_ref, rgam_ref, tril_ref, ind_ref, on256_ref = (next(it) for _ in range(8))
    wa2_ref, ba_ref, bif_ref, gnorm_ref, rnorm_ref, mnorm_ref, wqk_ref = (next(it) for _ in range(7))
    if has_init:
        g0_ref, r0_ref, c0_ref, n0_ref, m0_ref = (next(it) for _ in range(5))
    mix_ref, gout_ref, rout_ref, cout_ref, nout_ref, mout_ref = (next(it) for _ in range(6))
    gst_ref, sst_ref, cst_ref, mst_ref, la_ref, gprev_ref = (next(it) for _ in range(6))

    step = pl.program_id(1)
    last = pl.num_programs(1) - 1
    c = chunk
    hd = RET_D
    group = min(MIXER_GROUP, block)

    @pl.when(step == 0)
    def _init():
        gst_ref[...] = jnp.zeros_like(gst_ref)
        sst_ref[...] = jnp.zeros_like(sst_ref)
        cst_ref[...] = jnp.zeros_like(cst_ref)
        if has_init:
            for h in range(GLA_HEADS):
                gst_ref[h * GLA_DK:(h + 1) * GLA_DK, h * GLA_DV:(h + 1) * GLA_DV] = g0_ref[h]
            for h in range(RET_HEADS):
                p, a = divmod(h, 2)
                sst_ref[p, a * hd:(a + 1) * hd, a * hd:(a + 1) * hd] = r0_ref[h]
                cst_ref[p, a * hd:(a + 1) * hd, a * hd:(a + 1) * hd] = c0_ref[h]
                cst_ref[p, a * hd:(a + 1) * hd, LANES + a * hd:LANES + (a + 1) * hd] = n0_ref[h]
            mst_ref[...] = m0_ref[...]
        else:
            mst_ref[...] = jnp.zeros_like(mst_ref)

    lane_c = _iota((c, LANES), 1)
    lane_lo = (lane_c & hd) == 0
    pair_mask = [jnp.where(lane_lo, 1.0, 0.0).astype(BF16), jnp.where(lane_lo, 0.0, 1.0).astype(BF16)]
    gla_mask = [jnp.where((lane_c >> 5) == h, 1.0, 0.0).astype(BF16) for h in range(GLA_HEADS)]
    gv_mask = [jnp.where((_iota((c, GLA_W), 1) >> 6) == h, 1.0, 0.0).astype(BF16) for h in range(GLA_HEADS)]
    ones_b = jnp.ones((c, LANES), BF16)
    neg_inf = jnp.float32(-jnp.inf)
    causal = _iota((c, c), 1) <= _iota((c, c), 0)
    causal4 = _iota((GLA_HEADS * c, c), 1) <= (_iota((GLA_HEADS * c, c), 0) & (c - 1))
    bm_gk = jnp.where((_iota((GLA_KW, GLA_W), 0) >> 5) == (_iota((GLA_KW, GLA_W), 1) >> 6), 1.0, 0.0)
    bm_pair = jnp.where((_iota((LANES, LANES), 0) >> 6) == (_iota((LANES, LANES), 1) >> 6), 1.0, 0.0)
    bm_aug = jnp.concatenate([bm_pair, bm_pair], axis=1)
    sub_lo = _iota((LANES, c), 0) < hd

    def stack_pair(xb):
        return jnp.concatenate([xb * pair_mask[0], xb * pair_mask[1]], axis=0)

    gprev_ref[...] = gst_ref[...]
    on256 = on256_ref[...]

    def prep(g):
        rs = slice(g * group, (g + 1) * group)
        qk = [_dot(z_ref[rs, Z_MU + p * LANES:Z_MU + (p + 1) * LANES], wqk_ref[p]) for p in range(PAIRS)]
        out = {'qm': jnp.concatenate([t[:, :LANES] for t in qk], axis=1),
               'km': jnp.concatenate([t[:, LANES:] for t in qk], axis=1)}
        small = z_ref[rs, Z_SM:Z_SM + LANES]
        la = _log_sigmoid(_dot(small, wa2_ref[...]) + ba_ref[...]) * (1.0 / GLA_TAU)
        gates = small + bif_ref[...]
        lf = _log_sigmoid(gates)
        yield
        cum = jnp.dot(tril_ref[...], jnp.concatenate(_split2(la) + _split2(lf), axis=1),
                      preferred_element_type=F32)
        out['bg'] = cum[:, :LANES] + cum[:, LANES:2 * LANES]
        la_ref[rs, :] = la
        bt = cum[:, 2 * LANES:3 * LANES] + cum[:, 3 * LANES:]
        yield
        bxe = jnp.dot(jnp.concatenate(_split2(bt), axis=1), ind_ref[...], preferred_element_type=F32)
        out['bx'] = bxe[:, :ML_HEADS * LANES]
        out['dg'] = gates - bxe[:, ML_HEADS * LANES:]
        preps[g] = out

    def finish(g):
        rs = slice(g * group, (g + 1) * group)
        raw = [jnp.concatenate(t, axis=0) for t in zip(*raws.pop(g))]
        n_col = (RET_W + ML_W) // GLA_W
        os_ = [jnp.concatenate(raw[2 * k:2 * k + 2], axis=1) for k in range(n_col)]
        ocs = [o - _dot(o, on256) for o in os_]
        yield
        ons = [oc * lax.rsqrt(_dot(oc * oc, on256) + EPS) for oc in ocs]
        yield
        for k in range(n_col):
            on = ons[k]
            for j in range(GLA_W // LANES):
                t = (GLA_W // LANES) * k + j
                tile = on[:, j * LANES:(j + 1) * LANES]
                if t < PAIRS:
                    tile = (tile * rnorm_ref[:, t * LANES:(t + 1) * LANES]
                            * _silu(z_ref[rs, Z_RG + t * LANES:Z_RG + (t + 1) * LANES]))
                else:
                    tile = tile * mnorm_ref[:, (t - PAIRS) * LANES:(t - PAIRS + 1) * LANES]
                mix_ref[rs, MIX_RET + t * LANES:MIX_RET + (t + 1) * LANES] = tile

    per_group = group // c
    n_chunks = block // c

    def first_dots(ci, pre, lr):
        rows = slice(ci * c, (ci + 1) * c)
        d = {}
        b = pre['bg'][lr]
        bmid = b[c // 2 - 1:c // 2, :]
        bl = b[c - 1:c, :]
        gq = z_ref[rows, Z_GQ:Z_GQ + GLA_KW] * (GLA_DK ** -0.5)
        gk = z_ref[rows, Z_GK:Z_GK + GLA_KW]
        eq = jnp.exp(b - bmid)
        ek = jnp.exp(bmid - b)
        qh = gq * eq
        kh = gk * ek
        fac = jnp.maximum(jnp.maximum(jnp.abs(qh), jnp.abs(kh)), jnp.maximum(eq, ek))
        d['fac'] = functools.reduce(jnp.maximum, [fac[i:i + 8] for i in range(0, c, 8)])
        qh = qh.astype(BF16)
        d['qe'] = (gq * jnp.exp(b)).astype(BF16)
        kl_t = (gk * jnp.exp(bl - b)).T
        lhs = jnp.concatenate([qh * gla_mask[h] for h in range(GLA_HEADS)], axis=0)
        a = lax.dot_general(lhs, kh.astype(BF16), _NT, preferred_element_type=F32)
        d['a'] = jnp.where(causal4, a, 0.0).astype(BF16)
        d['gvb'] = z_ref[rows, Z_GV:Z_GV + GLA_W].astype(BF16)
        d['gupd'] = _dot(kl_t, d['gvb'])
        d['bl_col'] = b[c - 8:c, :].T[:, 7:8]
        d['rvb'] = z_ref[rows, Z_RV:Z_RV + RET_W].astype(BF16)
        kr = z_ref[rows, Z_RK:Z_RK + RET_W]
        d['qr'] = z_ref[rows, Z_RQ:Z_RQ + RET_W]
        d['rpm'] = [(lax.dot_general(stack_pair(d['qr'][:, p * LANES:(p + 1) * LANES].astype(BF16)),
                                     kr[:, p * LANES:(p + 1) * LANES].astype(BF16), _NT,
                                     preferred_element_type=F32) * rdec_ref[p]).astype(BF16)
                    for p in range(PAIRS)]
        kr_t = (kr.T * rwend_ref[...]).astype(BF16)
        d['rupd'] = [_dot(kr_t[p * LANES:(p + 1) * LANES], d['rvb'][:, p * LANES:(p + 1) * LANES])
                     for p in range(PAIRS)]
        d['mvb'] = z_ref[rows, Z_MV:Z_MV + ML_W].astype(BF16)
        km = pre['km'][lr]
        d['km_t'] = km.T
        d['ms2'] = [lax.dot_general(stack_pair(pre['qm'][lr, p * LANES:(p + 1) * LANES].astype(BF16)),
                                    km[:, p * LANES:(p + 1) * LANES].astype(BF16), _NT,
                                    preferred_element_type=F32) for p in range(PAIRS)]
        dg_t = pre['dg'][lr].T
        d['dg_t'] = dg_t
        d['mrow'] = [jnp.max(jnp.where(causal, pre['bx'][lr, h * LANES:h * LANES + c]
                                       + dg_t[SM_IG + h:SM_IG + h + 1, :], neg_inf), axis=1, keepdims=True)
                     for h in range(ML_HEADS)]
        return d

    def second_dots(ci, pre, lr, d):
        rows = slice(ci * c, (ci + 1) * c)
        raw = []
        a = d['a']
        o = jnp.dot(jnp.concatenate([a[h * c:(h + 1) * c] for h in range(GLA_HEADS)] + [d['qe']], axis=1),
                    jnp.concatenate([d['gvb'] * gv_mask[h] for h in range(GLA_HEADS)]
                                    + [gst_ref[...].astype(BF16)], axis=0),
                    preferred_element_type=F32)
        mix_ref[rows, 0:GLA_W] = o
        gst_ref[...] = gst_ref[...] * jnp.exp(d['bl_col']) + bm_gk * d['gupd']
        for p in range(PAIRS):
            lanes = slice(p * LANES, (p + 1) * LANES)
            vb = d['rvb'][:, lanes]
            pm = d['rpm'][p]
            raw.append(jnp.dot(
                jnp.concatenate([pm[:c], pm[c:], (d['qr'][:, lanes] * rint_ref[:, lanes]).astype(BF16)], axis=1),
                jnp.concatenate([vb * pair_mask[0], vb * pair_mask[1], sst_ref[p].astype(BF16)], axis=0),
                preferred_element_type=F32))
            sst_ref[p] = sst_ref[p] * rgam_ref[:, lanes] + bm_pair * d['rupd'][p]
        dg_t = d['dg_t']
        wls, w0ls = [], []
        for p in range(PAIRS):
            lanes = slice(p * LANES, (p + 1) * LANES)
            wd, w0, mt = [], [], []
            for a_ in range(2):
                h = 2 * p + a_
                colr = pre['bx'][lr, h * LANES:(h + 1) * LANES]
                logd = jnp.where(causal, colr[:, :c] + dg_t[SM_IG + h:SM_IG + h + 1, :], neg_inf)
                log0 = colr + mst_ref[:, h * LANES:(h + 1) * LANES]
                m_t = jnp.maximum(log0, d['mrow'][h])
                wd.append(jnp.exp(logd - m_t[:, :c]))
                w0.append(jnp.exp(log0 - m_t))
                mt.append(m_t)
                mst_ref[:, h * LANES:(h + 1) * LANES] = m_t[c - 1:c, :]
            qp = pre['qm'][lr, lanes]
            vb = d['mvb'][:, lanes]
            a0 = (d['ms2'][p][:c] * wd[0]).astype(BF16)
            a1 = (d['ms2'][p][c:] * wd[1]).astype(BF16)
            w0p = jnp.where(lane_lo, w0[0], w0[1])
            mtp = jnp.where(lane_lo, mt[0], mt[1])
            nd = jnp.dot(
                jnp.concatenate([a0, a1, (qp * w0p).astype(BF16)], axis=1),
                jnp.concatenate([jnp.concatenate([vb * pair_mask[0], pair_mask[0]], axis=1),
                                 jnp.concatenate([vb * pair_mask[1], pair_mask[1]], axis=1),
                                 cst_ref[p].astype(BF16)], axis=0),
                preferred_element_type=F32)
            hh = nd[:, :LANES] / jnp.maximum(jnp.abs(nd[:, LANES:]), jnp.exp(-mtp))
            raw.append(hh * _sigmoid(z_ref[rows, Z_MO + p * LANES:Z_MO + (p + 1) * LANES]))
            wls.append(jnp.where(sub_lo, wd[0][c - 1:c, :], wd[1][c - 1:c, :]))
            w0l = w0p[c - 1:c, :]
            w0ls.append(jnp.concatenate([w0l, w0l], axis=1))
        for p in range(PAIRS):
            lanes = slice(p * LANES, (p + 1) * LANES)
            upd = _dot(d['km_t'][lanes] * wls[p], jnp.concatenate([d['mvb'][:, lanes], ones_b], axis=1))
            cst_ref[p] = cst_ref[p] * w0ls[p] + bm_aug * upd
        return raw

    preps, raws, firsts, pending = {}, {}, {}, []

    def drain(gen):
        for _ in gen:
            pass

    def tick():
        pending[:] = [gen for gen in pending if next(gen, pending) is not pending]

    fac_max = jnp.zeros((8, LANES), F32)
    place = lambda ci: (ci // per_group, slice((ci % per_group) * c, (ci % per_group + 1) * c))
    drain(prep(0))
    firsts[0] = first_dots(0, preps[0], place(0)[1])
    for ci in range(n_chunks):
        g, lr = place(ci)
        if ci % per_group == 0:
            if g > 0:
                pending.append(finish(g - 1))
            if (g + 1) * group < block:
                pending.append(prep(g + 1))
        tick()
        if ci + 1 < n_chunks:
            g1, lr1 = place(ci + 1)
            while g1 not in preps:
                tick()
            firsts[ci + 1] = first_dots(ci + 1, preps[g1], lr1)
        tick()
        d = firsts.pop(ci)
        fac_max = jnp.maximum(fac_max, d['fac'])
        raws.setdefault(g, []).append(second_dots(ci, preps[g], lr, d))
        tick()
    while pending:
        tick()
    drain(finish(block // group - 1))


    @pl.when(jnp.logical_not(jnp.max(fac_max) < GLA_FACTOR_LIMIT))
    def _gla_per_token():
        gst_ref[...] = gprev_ref[...]

        def tokens8(g, carry):
            rows8 = pl.ds(pl.multiple_of(g * 8, 8), 8)
            alpha_t = jnp.exp(la_ref[rows8, :]).T
            k_t = z_ref[rows8, Z_GK:Z_GK + GLA_KW].T
            q_t = (z_ref[rows8, Z_GQ:Z_GQ + GLA_KW] * (GLA_DK ** -0.5)).T
            v8 = z_ref[rows8, Z_GV:Z_GV + GLA_W]
            s = gst_ref[...]
            outs = []
            for j in range(8):
                s = s * alpha_t[:, j:j + 1] + bm_gk * (k_t[:, j:j + 1] * v8[j:j + 1, :])
                outs.append(jnp.sum(q_t[:, j:j + 1] * s, axis=0, keepdims=True))
            gst_ref[...] = s
            mix_ref[rows8, 0:GLA_W] = jnp.concatenate(outs, axis=0)
            return carry

        lax.fori_loop(0, block // 8, tokens8, 0)

    o = mix_ref[:, 0:GLA_W]
    ms = _dot(o * o, on256)
    mix_ref[:, 0:GLA_W] = o * lax.rsqrt(ms + EPS) * gnorm_ref[...] * _silu(z_ref[:, Z_GG:Z_GG + GLA_W])

    @pl.when(step == last)
    def _finish():
        for h in range(GLA_HEADS):
            gout_ref[h] = gst_ref[h * GLA_DK:(h + 1) * GLA_DK, h * GLA_DV:(h + 1) * GLA_DV]
        for h in range(RET_HEADS):
            p, a = divmod(h, 2)
            rout_ref[h] = sst_ref[p, a * hd:(a + 1) * hd, a * hd:(a + 1) * hd]
            cout_ref[h] = cst_ref[p, a * hd:(a + 1) * hd, a * hd:(a + 1) * hd]
            nout_ref[h] = cst_ref[p, a * hd:(a + 1) * hd, LANES + a * hd:LANES + (a + 1) * hd]
        mout_ref[...] = mst_ref[...]


def _mixer_constants(block, chunk):
    t = np.arange(block)
    tril = ((t[:, None] // chunk == t[None, :] // chunk) & (t[None, :] <= t[:, None])).astype(np.float32)
    r = np.arange(2 * LANES)[:, None] % LANES
    col = np.arange(ML_HEADS * LANES)[None, :] // LANES
    lane = np.arange(LANES)[None, :]
    ind = np.concatenate([r == col + SM_FG, (r == lane + ML_HEADS) & (lane >= SM_IG) & (lane < SM_FG)],
                         axis=1).astype(np.float32)
    i256 = np.arange(GLA_W) // GLA_DV
    on256 = (i256[:, None] == i256[None, :]).astype(np.float32) / GLA_DV
    return tuple(jnp.asarray(m, BF16) for m in (tril, ind, on256))


def _mixer(z3, tabs, wts, init, layer):
    bsz, seq, _ = z3.shape
    block = min(MIXER_BLOCK, seq)
    chunk = min(MIXER_CHUNK, block)
    has_init = init is not None
    rdec, rint, rwend, rgam = tabs
    consts = _mixer_constants(min(MIXER_GROUP, block), chunk)

    def const(shape):
        nd = len(shape)
        return pl.BlockSpec(shape, lambda b, s: (0,) * nd)

    def per_layer(shape):
        nd = len(shape)
        return pl.BlockSpec((None,) + shape, lambda b, s: (layer,) + (0,) * nd)

    def per_batch(shape, with_layer):
        nd = len(shape)
        if with_layer:
            return pl.BlockSpec((None, None) + shape, lambda b, s: (layer, b) + (0,) * nd)
        return pl.BlockSpec((None,) + shape, lambda b, s: (b,) + (0,) * nd)

    state_shapes = [(GLA_HEADS, GLA_DK, GLA_DV), (RET_HEADS, RET_D, RET_D), (ML_HEADS, ML_D, ML_D),
                    (ML_HEADS, ML_D, ML_D), (1, ML_HEADS * LANES)]
    in_specs = [
        pl.BlockSpec((None, block, Z_W), lambda b, s: (b, s, 0)),
        const(rdec.shape), const(rint.shape), const(rwend.shape), const(rgam.shape),
    ] + [const(m.shape) for m in consts] + [
        per_layer((LANES, GLA_KW)), per_layer((1, GLA_KW)), per_layer((1, LANES)),
        per_layer((1, GLA_W)), per_layer((1, RET_W)), per_layer((1, ML_W)),
        per_layer((PAIRS, LANES, 2 * LANES)),
    ]
    args = [z3, rdec, rint, rwend, rgam] + list(consts) + list(wts)
    if has_init:
        in_specs += [per_batch(s, True) for s in state_shapes]
        args += list(init)
    out_specs = [pl.BlockSpec((None, block, D_MODEL), lambda b, s: (b, s, 0))]
    out_specs += [per_batch(s, False) for s in state_shapes]
    out_shape = [jax.ShapeDtypeStruct((bsz, seq, D_MODEL), F32)]
    out_shape += [jax.ShapeDtypeStruct((bsz,) + s, F32) for s in state_shapes]
    scratch = [
        pltpu.VMEM((GLA_KW, GLA_W), F32), pltpu.VMEM((PAIRS, LANES, LANES), F32),
        pltpu.VMEM((PAIRS, LANES, 2 * LANES), F32), pltpu.VMEM((1, ML_HEADS * LANES), F32),
        pltpu.VMEM((block, LANES), F32),
        pltpu.VMEM((GLA_KW, GLA_W), F32),
    ]
    return pl.pallas_call(
        functools.partial(_mixer_kernel, block=block, chunk=chunk, has_init=has_init),
        grid=(bsz, seq // block),
        in_specs=in_specs,
        out_specs=out_specs,
        out_shape=out_shape,
        scratch_shapes=scratch,
        compiler_params=pltpu.CompilerParams(dimension_semantics=("parallel", "arbitrary"),
                                             vmem_limit_bytes=VMEM_LIMIT),
        name="mixer",
    )(*args)


def _ffn_kernel(*refs, final):
    if final:
        x_ref, mix_ref, wo_ref, g_ref, wg_ref, wu_ref, wd_ref, gf_ref, out_ref, y_ref = refs
    else:
        x_ref, mix_ref, wo_ref, g_ref, wg_ref, wu_ref, wd_ref, out_ref = refs
    x1 = x_ref[...] + jnp.dot(mix_ref[...].astype(BF16), wo_ref[...], preferred_element_type=F32)
    r = lax.rsqrt(jnp.mean(x1 * x1, axis=-1, keepdims=True) + EPS)
    hf = (x1 * r * g_ref[...]).astype(BF16)
    acc = x1
    for c in range(D_FF // FF_CHUNK):
        cols = slice(c * FF_CHUNK, (c + 1) * FF_CHUNK)
        gate = jnp.dot(hf, wg_ref[:, cols], preferred_element_type=F32)
        up = jnp.dot(hf, wu_ref[:, cols], preferred_element_type=F32)
        act = (_silu(gate) * up).astype(BF16)
        acc = acc + jnp.dot(act, wd_ref[cols, :], preferred_element_type=F32)
    out_ref[...] = acc
    if final:
        rf = lax.rsqrt(jnp.mean(acc * acc, axis=-1, keepdims=True) + EPS)
        y_ref[...] = acc * rf * gf_ref[...]


def _ffn(x2d, mix2d, wo, norm3, wg, wu, wd, layer, norm_final2):
    n = x2d.shape[0]
    tm = min(TOKEN_BLOCK, n)
    final = norm_final2 is not None

    def weight(shape):
        return pl.BlockSpec((None,) + shape, lambda i: (layer, 0, 0), pipeline_mode=pl.Buffered(1))

    row_spec = pl.BlockSpec((tm, D_MODEL), lambda i: (i, 0))
    in_specs = [row_spec, row_spec, weight((D_MODEL, D_MODEL)),
                pl.BlockSpec((None, 1, D_MODEL), lambda i: (layer, 0, 0)),
                weight((D_MODEL, D_FF)), weight((D_MODEL, D_FF)), weight((D_FF, D_MODEL))]
    args = [x2d, mix2d, wo, norm3, wg, wu, wd]
    out_specs = [row_spec]
    out_shape = [jax.ShapeDtypeStruct((n, D_MODEL), F32)]
    if final:
        in_specs.append(pl.BlockSpec((1, D_MODEL), lambda i: (0, 0)))
        args.append(norm_final2)
        out_specs.append(row_spec)
        out_shape.append(jax.ShapeDtypeStruct((n, D_MODEL), F32))
    return pl.pallas_call(
        functools.partial(_ffn_kernel, final=final),
        grid=(n // tm,),
        in_specs=in_specs,
        out_specs=out_specs,
        out_shape=out_shape,
        compiler_params=pltpu.CompilerParams(dimension_semantics=("parallel",),
                                             vmem_limit_bytes=VMEM_LIMIT),
        name="ffn",
    )(*args)


def _reorder_w_in(w_in):
    offs = [0]
    for sz in SPLIT_SIZES:
        offs.append(offs[-1] + sz)
    seg = [w_in[..., offs[i]:offs[i + 1]] for i in range(len(SPLIT_SIZES))]
    gq, gk, gv, ga, gg, rq, rk, rv, rg, mu, mv, mo, mi, mf = seg
    pad = jnp.zeros(w_in.shape[:-1] + (LANES - GLA_RANK - 2 * ML_HEADS,), w_in.dtype)
    rk = rk * (RET_D ** -0.5)
    return jnp.concatenate([gq, gk, gv, gg, rq, rk, rv, rg, mu, mv, mo, ga, mi, mf, pad],
                           axis=-1).astype(BF16)


def _block_diag_pairs(w):
    depth, h, d, e = w.shape
    eye = jnp.eye(2, dtype=w.dtype)
    return jnp.einsum('lpade,ab->lpadbe', w.reshape(depth, h // 2, 2, d, e), eye).reshape(
        depth, h // 2, 2 * d, 2 * e)


def _rope_tables(pos):
    inv = ROPE_BASE ** (-jnp.arange(0, RET_D, 2, dtype=F32) / RET_D)
    ang = pos.astype(F32)[:, None] * inv[None, :]
    cos, sin = jnp.cos(ang), jnp.sin(ang)
    cos_h = jnp.concatenate([cos, cos], axis=-1)
    sin_h = jnp.concatenate([-sin, sin], axis=-1)
    return jnp.tile(cos_h, (1, RET_HEADS)), jnp.tile(sin_h, (1, RET_HEADS))


def _retention_tables(chunk):
    lg = jnp.log1p(-jnp.exp2(-5.0 - jnp.arange(RET_HEADS, dtype=F32)))
    t = jnp.arange(chunk, dtype=F32)
    diff = t[:, None] - t[None, :]
    decay = jnp.where((diff >= 0)[..., None], jnp.exp(jnp.maximum(diff, 0.0)[..., None] * lg), 0.0)
    decay = jnp.transpose(decay, (2, 0, 1)).reshape(PAIRS, 2 * chunk, chunk)
    inter = jnp.repeat(jnp.exp((t[:, None] + 1.0) * lg), RET_D, axis=1)
    w_end = jnp.exp((chunk - 1.0 - t)[None, :] * lg[:, None])
    w_end = jnp.repeat(w_end, RET_D, axis=0)
    gam = jnp.repeat(jnp.exp(chunk * lg), RET_D)[None, :]
    return decay, inter, w_end, gam


def kernel(x_prompt, x_sample, state_gla, state_ret, state_mlstm_c, state_mlstm_n, state_mlstm_m,
           cache_mlstm_conv, norm_mix, w_in, gla_w_a2, gla_b_a, gla_norm, ret_norm, ml_conv_w,
           ml_conv_b, ml_wq, ml_wk, ml_b_if, ml_norm, w_out, norm_ffn, w_gate, w_up, w_down, norm_final):
    depth = w_in.shape[0]

    w_in_p = _reorder_w_in(w_in)
    wo, wg, wu, wd = (w.astype(BF16) for w in (w_out, w_gate, w_up, w_down))
    wa2 = jnp.zeros((depth, LANES, GLA_KW), F32).at[:, :GLA_RANK, :].set(gla_w_a2).astype(BF16)
    bif = jnp.zeros((depth, 1, LANES), F32).at[:, 0, SM_IG:SM_IG + 2 * ML_HEADS].set(ml_b_if)
    wqk = jnp.concatenate([_block_diag_pairs(ml_wq), _block_diag_pairs(ml_wk) * (ML_D ** -0.5)],
                          axis=-1).astype(BF16)
    row = lambda a: a[:, None, :]
    wts = (wa2, row(gla_b_a), bif, row(gla_norm), row(ret_norm), row(ml_norm), wqk)
    conv_b3 = row(ml_conv_b)
    norm_mix3, norm_ffn3, norm_final2 = row(norm_mix), row(norm_ffn), norm_final[None, :]

    init_sample = (state_gla, state_ret, state_mlstm_c,
                   jnp.broadcast_to(state_mlstm_n[..., None], state_mlstm_n.shape + (ML_D,)),
                   jnp.repeat(state_mlstm_m, LANES, axis=-1)[:, :, None, :])

    def run(x, pos, init, conv_cache):
        bsz, seq, _ = x.shape
        chunk = min(MIXER_CHUNK, MIXER_BLOCK, seq)
        assert seq % chunk == 0 and chunk & (chunk - 1) == 0 and chunk >= 2 * 8
        rope = _rope_tables(pos)
        tabs = _retention_tables(chunk)
        x2d = x.reshape(bsz * seq, D_MODEL)
        states = []
        y2d = None
        for i in range(depth):
            z, cv = _inproj(x2d.reshape(bsz, seq, D_MODEL), norm_mix3, w_in_p, rope, ml_conv_w, conv_b3,
                            conv_cache, i)
            mixed, *st = _mixer(z, tabs, wts, init, i)
            outs = _ffn(x2d, mixed.reshape(bsz * seq, D_MODEL), wo, norm_ffn3, wg, wu, wd, i,
                        norm_final2 if i == depth - 1 else None)
            x2d = outs[0]
            if i == depth - 1:
                y2d = outs[1]
            states.append(st + [cv])
        g, r, c, n, m, cv = (jnp.stack([st[j] for st in states]) for j in range(6))
        return (y2d.reshape(bsz, seq, D_MODEL), g, r, c,
                n[..., 0], m[:, :, 0, ::LANES], cv)

    pos_p = jnp.arange(x_prompt.shape[1], dtype=jnp.int32)
    pos_s = PAST_LEN + jnp.arange(x_sample.shape[1], dtype=jnp.int32)
    yp, gp, rp, cp, np_, mp, cvp = run(x_prompt, pos_p, None, None)
    ys, gs, rs, cs, ns, ms, cvs = run(x_sample, pos_s, init_sample, cache_mlstm_conv)
    return (yp, ys, gp, gs, rp, rs, cp, cs, np_, ns, mp, ms, cvp, cvs)
```

```python
import functools

import numpy as np
import jax
import jax.numpy as jnp
from jax import lax
from jax.experimental import pallas as pl
from jax.experimental.pallas import tpu as pltpu

F32 = jnp.float32
BF16 = jnp.bfloat16

D_MODEL = 1024
EPS = 1e-6
GLA_HEADS, GLA_DK, GLA_DV, GLA_RANK, GLA_TAU = 4, 32, 64, 16, 16.0
RET_HEADS, RET_D = 6, 64
ML_HEADS, ML_D = 6, 64
CONV_W = 4
ROPE_BASE = 10000.0
PAST_LEN = 2048
GLA_KW = GLA_HEADS * GLA_DK
GLA_W = GLA_HEADS * GLA_DV
RET_W = RET_HEADS * RET_D
ML_W = ML_HEADS * ML_D
D_FF = 2816
SPLIT_SIZES = (GLA_KW, GLA_KW, GLA_W, GLA_RANK, GLA_W, RET_W, RET_W, RET_W, RET_W, ML_W, ML_W, ML_W,
               ML_HEADS, ML_HEADS)

LANES = 128
PAIRS = RET_W // LANES
Z_GQ, Z_GK, Z_GV, Z_GG = 0, 128, 256, 512
Z_RQ, Z_RK, Z_RV, Z_RG = 768, 1152, 1536, 1920
Z_MU, Z_MV, Z_MO = 2304, 2688, 3072
Z_SM = 3456
Z_W = 3584
SM_IG, SM_FG = GLA_RANK, GLA_RANK + ML_HEADS
MIX_RET, MIX_ML = GLA_W, GLA_W + RET_W

TOKEN_BLOCK = 512
MIXER_BLOCK = 1024
MIXER_CHUNK = 128
MIXER_GROUP = 256
FF_CHUNK = 1408
GLA_FACTOR_LIMIT = 1e18
VMEM_LIMIT = 56 * 1024 * 1024

_NT = (((1,), (1,)), ((), ()))


def _dot(a, b):
    return jnp.dot(a.astype(BF16), b.astype(BF16), preferred_element_type=F32)


def _dot_nt(a, b):
    return lax.dot_general(a.astype(BF16), b.astype(BF16), _NT, preferred_element_type=F32)


def _split2(x):
    hi = x.astype(BF16)
    return [hi, (x - hi.astype(F32)).astype(BF16)]


def _log_sigmoid(x):
    return jnp.minimum(x, 0.0) - jnp.log(1.0 + jnp.exp(-jnp.abs(x)))


def _sigmoid(x):
    return 0.5 + 0.5 * jnp.tanh(0.5 * x)


def _silu(x):
    h = 0.5 * x
    return h + h * jnp.tanh(h)


def _iota(shape, dim):
    return lax.broadcasted_iota(jnp.int32, shape, dim)


def _inproj_kernel(*refs, tm, has_init):
    if has_init:
        x_ref, g_ref, w_ref, cos_ref, sin_ref, cw_ref, cb_ref, cv0_ref, z_ref, cvout_ref, xp_ref = refs
    else:
        x_ref, g_ref, w_ref, cos_ref, sin_ref, cw_ref, cb_ref, z_ref, cvout_ref, xp_ref = refs
    conv_lo = 8 - (CONV_W - 1)

    @pl.when(pl.program_id(1) == 0)
    def _start_of_sequence():
        if has_init:
            xp_ref[conv_lo:8, :] = cv0_ref[...]
        else:
            xp_ref[conv_lo:8, :] = jnp.zeros((CONV_W - 1, ML_W), F32)

    x = x_ref[...]
    r = lax.rsqrt(jnp.mean(x * x, axis=-1, keepdims=True) + EPS)
    h = (x * r * g_ref[...]).astype(BF16)

    def proj(lo, hi):
        return jnp.dot(h, w_ref[:, lo:hi], preferred_element_type=F32)

    z_ref[:, Z_GQ:Z_RQ] = proj(Z_GQ, Z_RQ)
    seg = proj(Z_RQ, Z_RV)
    lo32 = (_iota((tm, LANES), 1) & 32) == 0
    for p in range(2 * PAIRS):
        lanes = slice((p % PAIRS) * LANES, (p % PAIRS + 1) * LANES)
        t = seg[:, p * LANES:(p + 1) * LANES]
        sw = jnp.where(lo32, pltpu.roll(t, LANES - 32, 1), pltpu.roll(t, 32, 1))
        z_ref[:, Z_RQ + p * LANES:Z_RQ + (p + 1) * LANES] = t * cos_ref[:, lanes] + sw * sin_ref[:, lanes]
    z_ref[:, Z_RV:Z_MU] = proj(Z_RV, Z_MU)
    seg = proj(Z_MU, Z_MO)
    z_ref[:, Z_MV:Z_MO] = seg[:, ML_W:]
    xp_ref[8:8 + tm, :] = seg[:, :ML_W]
    cw = cw_ref[...]
    conv = cb_ref[...]
    for j in range(CONV_W):
        conv = conv + xp_ref[conv_lo + j:conv_lo + j + tm, :] * cw[j:j + 1, :]
    z_ref[:, Z_MU:Z_MV] = _silu(conv)
    conv_tail = xp_ref[tm + conv_lo:tm + 8, :]
    xp_ref[conv_lo:8, :] = conv_tail
    cvout_ref[...] = conv_tail
    z_ref[:, Z_MO:Z_W] = proj(Z_MO, Z_W)


def _inproj(x3, norm3, w_in_p, rope, cw, cb, cv0, layer):
    bsz, seq, _ = x3.shape
    tm = min(TOKEN_BLOCK, seq)
    has_init = cv0 is not None
    cos, sin = rope
    in_specs = [
        pl.BlockSpec((None, tm, D_MODEL), lambda b, s: (b, s, 0)),
        pl.BlockSpec((None, 1, D_MODEL), lambda b, s: (layer, 0, 0)),
        pl.BlockSpec((None, D_MODEL, Z_W), lambda b, s: (layer, 0, 0), pipeline_mode=pl.Buffered(1)),
        pl.BlockSpec((tm, RET_W), lambda b, s: (s, 0)),
        pl.BlockSpec((tm, RET_W), lambda b, s: (s, 0)),
        pl.BlockSpec((None, CONV_W, ML_W), lambda b, s: (layer, 0, 0)),
        pl.BlockSpec((None, 1, ML_W), lambda b, s: (layer, 0, 0)),
    ]
    args = [x3, norm3, w_in_p, cos, sin, cw, cb]
    if has_init:
        in_specs.append(pl.BlockSpec((None, None, CONV_W - 1, ML_W), lambda b, s: (layer, b, 0, 0)))
        args.append(cv0)
    return pl.pallas_call(
        functools.partial(_inproj_kernel, tm=tm, has_init=has_init),
        grid=(bsz, seq // tm),
        in_specs=in_specs,
        out_specs=[pl.BlockSpec((None, tm, Z_W), lambda b, s: (b, s, 0)),
                   pl.BlockSpec((None, CONV_W - 1, ML_W), lambda b, s: (b, 0, 0))],
        out_shape=[jax.ShapeDtypeStruct((bsz, seq, Z_W), F32),
                   jax.ShapeDtypeStruct((bsz, CONV_W - 1, ML_W), F32)],
        scratch_shapes=[pltpu.VMEM((tm + 8, ML_W), F32)],
        compiler_params=pltpu.CompilerParams(dimension_semantics=("parallel", "arbitrary"),
                                             vmem_limit_bytes=VMEM_LIMIT),
        name="inproj",
    )(*args)


def _mixer_kernel(*refs, block, chunk, has_init):
    it = iter(refs)
    z_ref, rdec_ref, rint_ref, rwend_ref, rgam_ref, tril_ref, ind_ref, on256_ref = (next(it) for _ in range(8))
    wa2_ref, ba_ref, bif_ref, gnorm_ref, rnorm_ref, mnorm_ref, wqk_ref = (next(it) for _ in range(7))
    if has_init:
        g0_ref, r0_ref, c0_ref, n0_ref, m0_ref = (next(it) for _ in range(5))
    mix_ref, gout_ref, rout_ref, cout_ref, nout_ref, mout_ref = (next(it) for _ in range(6))
    gst_ref, sst_ref, cst_ref, mst_ref, la_ref, gprev_ref = (next(it) for _ in range(6))

    step = pl.program_id(1)
    last = pl.num_programs(1) - 1
    c = chunk
    hd = RET_D
    group = min(MIXER_GROUP, block)

    @pl.when(step == 0)
    def _init():
        gst_ref[...] = jnp.zeros_like(gst_ref)
        sst_ref[...] = jnp.zeros_like(sst_ref)
        cst_ref[...] = jnp.zeros_like(cst_ref)
        if has_init:
            for h in range(GLA_HEADS):
                gst_ref[h * GLA_DK:(h + 1) * GLA_DK, h * GLA_DV:(h + 1) * GLA_DV] = g0_ref[h]
            for h in range(RET_HEADS):
                p, a = divmod(h, 2)
                sst_ref[p, a * hd:(a + 1) * hd, a * hd:(a + 1) * hd] = r0_ref[h]
                cst_ref[p, a * hd:(a + 1) * hd, a * hd:(a + 1) * hd] = c0_ref[h]
                cst_ref[p, a * hd:(a + 1) * hd, LANES + a * hd:LANES + (a + 1) * hd] = n0_ref[h]
            mst_ref[...] = m0_ref[...]
        else:
            mst_ref[...] = jnp.zeros_like(mst_ref)

    lane_c = _iota((c, LANES), 1)
    lane_lo = (lane_c & hd) == 0
    pair_mask = [jnp.where(lane_lo, 1.0, 0.0).astype(BF16), jnp.where(lane_lo, 0.0, 1.0).astype(BF16)]
    gla_mask = [jnp.where((lane_c >> 5) == h, 1.0, 0.0).astype(BF16) for h in range(GLA_HEADS)]
    gv_mask = [jnp.where((_iota((c, GLA_W), 1) >> 6) == h, 1.0, 0.0).astype(BF16) for h in range(GLA_HEADS)]
    ones_b = jnp.ones((c, LANES), BF16)
    neg_inf = jnp.float32(-jnp.inf)
    causal = _iota((c, c), 1) <= _iota((c, c), 0)
    causal4 = _iota((GLA_HEADS * c, c), 1) <= (_iota((GLA_HEADS * c, c), 0) & (c - 1))
    bm_gk = jnp.where((_iota((GLA_KW, GLA_W), 0) >> 5) == (_iota((GLA_KW, GLA_W), 1) >> 6), 1.0, 0.0)
    bm_pair = jnp.where((_iota((LANES, LANES), 0) >> 6) == (_iota((LANES, LANES), 1) >> 6), 1.0, 0.0)
    bm_aug = jnp.concatenate([bm_pair, bm_pair], axis=1)
    sub_lo = _iota((LANES, c), 0) < hd

    def stack_pair(xb):
        return jnp.concatenate([xb * pair_mask[0], xb * pair_mask[1]], axis=0)

    gprev_ref[...] = gst_ref[...]
    on256 = on256_ref[...]

    def prep(g):
        rs = slice(g * group, (g + 1) * group)
        qk = [_dot(z_ref[rs, Z_MU + p * LANES:Z_MU + (p + 1) * LANES], wqk_ref[p]) for p in range(PAIRS)]
        out = {'qm': jnp.concatenate([t[:, :LANES] for t in qk], axis=1),
               'km': jnp.concatenate([t[:, LANES:] for t in qk], axis=1)}
        small = z_ref[rs, Z_SM:Z_SM + LANES]
        la = _log_sigmoid(_dot(small, wa2_ref[...]) + ba_ref[...]) * (1.0 / GLA_TAU)
        gates = small + bif_ref[...]
        lf = _log_sigmoid(gates)
        yield
        cum = jnp.dot(tril_ref[...], jnp.concatenate(_split2(la) + _split2(lf), axis=1),
                      preferred_element_type=F32)
        out['bg'] = cum[:, :LANES] + cum[:, LANES:2 * LANES]
        la_ref[rs, :] = la
        bt = cum[:, 2 * LANES:3 * LANES] + cum[:, 3 * LANES:]
        yield
        bxe = jnp.dot(jnp.concatenate(_split2(bt), axis=1), ind_ref[...], preferred_element_type=F32)
        out['bx'] = bxe[:, :ML_HEADS * LANES]
        out['dg'] = gates - bxe[:, ML_HEADS * LANES:]
        preps[g] = out

    def finish(g):
        rs = slice(g * group, (g + 1) * group)
        raw = [jnp.concatenate(t, axis=0) for t in zip(*raws.pop(g))]
        n_col = (RET_W + ML_W) // GLA_W
        os_ = [jnp.concatenate(raw[2 * k:2 * k + 2], axis=1) for k in range(n_col)]
        ocs = [o - _dot(o, on256) for o in os_]
        yield
        ons = [oc * lax.rsqrt(_dot(oc * oc, on256) + EPS) for oc in ocs]
        yield
        for k in range(n_col):
            on = ons[k]
            for j in range(GLA_W // LANES):
                t = (GLA_W // LANES) * k + j
                tile = on[:, j * LANES:(j + 1) * LANES]
                if t < PAIRS:
                    tile = (tile * rnorm_ref[:, t * LANES:(t + 1) * LANES]
                            * _silu(z_ref[rs, Z_RG + t * LANES:Z_RG + (t + 1) * LANES]))
                else:
                    tile = tile * mnorm_ref[:, (t - PAIRS) * LANES:(t - PAIRS + 1) * LANES]
                mix_ref[rs, MIX_RET + t * LANES:MIX_RET + (t + 1) * LANES] = tile

    per_group = group // c
    n_chunks = block // c

    def first_dots(ci, pre, lr):
        rows = slice(ci * c, (ci + 1) * c)
        d = {}
        b = pre['bg'][lr]
        bmid = b[c // 2 - 1:c // 2, :]
        bl = b[c - 1:c, :]
        gq = z_ref[rows, Z_GQ:Z_GQ + GLA_KW] * (GLA_DK ** -0.5)
        gk = z_ref[rows, Z_GK:Z_GK + GLA_KW]
        eq = jnp.exp(b - bmid)
        ek = jnp.exp(bmid - b)
        qh = gq * eq
        kh = gk * ek
        fac = jnp.maximum(jnp.maximum(jnp.abs(qh), jnp.abs(kh)), jnp.maximum(eq, ek))
        d['fac'] = functools.reduce(jnp.maximum, [fac[i:i + 8] for i in range(0, c, 8)])
        qh = qh.astype(BF16)
        d['qe'] = (gq * jnp.exp(b)).astype(BF16)
        kl_t = (gk * jnp.exp(bl - b)).T
        lhs = jnp.concatenate([qh * gla_mask[h] for h in range(GLA_HEADS)], axis=0)
        a = lax.dot_general(lhs, kh.astype(BF16), _NT, preferred_element_type=F32)
        d['a'] = jnp.where(causal4, a, 0.0).astype(BF16)
        d['gvb'] = z_ref[rows, Z_GV:Z_GV + GLA_W].astype(BF16)
        d['gupd'] = _dot(kl_t, d['gvb'])
        d['bl_col'] = b[c - 8:c, :].T[:, 7:8]
        d['rvb'] = z_ref[rows, Z_RV:Z_RV + RET_W].astype(BF16)
        kr = z_ref[rows, Z_RK:Z_RK + RET_W]
        d['qr'] = z_ref[rows, Z_RQ:Z_RQ + RET_W]
        d['rpm'] = [(lax.dot_general(stack_pair(d['qr'][:, p * LANES:(p + 1) * LANES].astype(BF16)),
                                     kr[:, p * LANES:(p + 1) * LANES].astype(BF16), _NT,
                                     preferred_element_type=F32) * rdec_ref[p]).astype(BF16)
                    for p in range(PAIRS)]
        kr_t = (kr.T * rwend_ref[...]).astype(BF16)
        d['rupd'] = [_dot(kr_t[p * LANES:(p + 1) * LANES], d['rvb'][:, p * LANES:(p + 1) * LANES])
                     for p in range(PAIRS)]
        d['mvb'] = z_ref[rows, Z_MV:Z_MV + ML_W].astype(BF16)
        km = pre['km'][lr]
        d['km_t'] = km.T
        d['ms2'] = [lax.dot_general(stack_pair(pre['qm'][lr, p * LANES:(p + 1) * LANES].astype(BF16)),
                                    km[:, p * LANES:(p + 1) * LANES].astype(BF16), _NT,
                                    preferred_element_type=F32) for p in range(PAIRS)]
        dg_t = pre['dg'][lr].T
        d['dg_t'] = dg_t
        d['mrow'] = [jnp.max(jnp.where(causal, pre['bx'][lr, h * LANES:h * LANES + c]
                                       + dg_t[SM_IG + h:SM_IG + h + 1, :], neg_inf), axis=1, keepdims=True)
                     for h in range(ML_HEADS)]
        return d

    def second_dots(ci, pre, lr, d):
        rows = slice(ci * c, (ci + 1) * c)
        raw = []
        a = d['a']
        o = jnp.dot(jnp.concatenate([a[h * c:(h + 1) * c] for h in range(GLA_HEADS)] + [d['qe']], axis=1),
                    jnp.concatenate([d['gvb'] * gv_mask[h] for h in range(GLA_HEADS)]
                                    + [gst_ref[...].astype(BF16)], axis=0),
                    preferred_element_type=F32)
        mix_ref[rows, 0:GLA_W] = o
        gst_ref[...] = gst_ref[...] * jnp.exp(d['bl_col']) + bm_gk * d['gupd']
        for p in range(PAIRS):
            lanes = slice(p * LANES, (p + 1) * LANES)
            vb = d['rvb'][:, lanes]
            pm = d['rpm'][p]
            raw.append(jnp.dot(
                jnp.concatenate([pm[:c], pm[c:], (d['qr'][:, lanes] * rint_ref[:, lanes]).astype(BF16)], axis=1),
                jnp.concatenate([vb * pair_mask[0], vb * pair_mask[1], sst_ref[p].astype(BF16)], axis=0),
                preferred_element_type=F32))
            sst_ref[p] = sst_ref[p] * rgam_ref[:, lanes] + bm_pair * d['rupd'][p]
        dg_t = d['dg_t']
        wls, w0ls = [], []
        for p in range(PAIRS):
            lanes = slice(p * LANES, (p + 1) * LANES)
            wd, w0, mt = [], [], []
            for a_ in range(2):
                h = 2 * p + a_
                colr = pre['bx'][lr, h * LANES:(h + 1) * LANES]
                logd = jnp.where(causal, colr[:, :c] + dg_t[SM_IG + h:SM_IG + h + 1, :], neg_inf)
                log0 = colr + mst_ref[:, h * LANES:(h + 1) * LANES]
                m_t = jnp.maximum(log0, d['mrow'][h])
                wd.append(jnp.exp(logd - m_t[:, :c]))
                w0.append(jnp.exp(log0 - m_t))
                mt.append(m_t)
                mst_ref[:, h * LANES:(h + 1) * LANES] = m_t[c - 1:c, :]
            qp = pre['qm'][lr, lanes]
            vb = d['mvb'][:, lanes]
            a0 = (d['ms2'][p][:c] * wd[0]).astype(BF16)
            a1 = (d['ms2'][p][c:] * wd[1]).astype(BF16)
            w0p = jnp.where(lane_lo, w0[0], w0[1])
            mtp = jnp.where(lane_lo, mt[0], mt[1])
            nd = jnp.dot(
                jnp.concatenate([a0, a1, (qp * w0p).astype(BF16)], axis=1),
                jnp.concatenate([jnp.concatenate([vb * pair_mask[0], pair_mask[0]], axis=1),
                                 jnp.concatenate([vb * pair_mask[1], pair_mask[1]], axis=1),
                                 cst_ref[p].astype(BF16)], axis=0),
                preferred_element_type=F32)
            hh = nd[:, :LANES] / jnp.maximum(jnp.abs(nd[:, LANES:]), jnp.exp(-mtp))
            raw.append(hh * _sigmoid(z_ref[rows, Z_MO + p * LANES:Z_MO + (p + 1) * LANES]))
            wls.append(jnp.where(sub_lo, wd[0][c - 1:c, :], wd[1][c - 1:c, :]))
            w0l = w0p[c - 1:c, :]
            w0ls.append(jnp.concatenate([w0l, w0l], axis=1))
        for p in range(PAIRS):
            lanes = slice(p * LANES, (p + 1) * LANES)
            upd = _dot(d['km_t'][lanes] * wls[p], jnp.concatenate([d['mvb'][:, lanes], ones_b], axis=1))
            cst_ref[p] = cst_ref[p] * w0ls[p] + bm_aug * upd
        return raw

    preps, raws, firsts, pending = {}, {}, {}, []

    def drain(gen):
        for _ in gen:
            pass

    def tick():
        pending[:] = [gen for gen in pending if next(gen, pending) is not pending]

    fac_max = jnp.zeros((8, LANES), F32)
    place = lambda ci: (ci // per_group, slice((ci % per_group) * c, (ci % per_group + 1) * c))
    drain(prep(0))
    firsts[0] = first_dots(0, preps[0], place(0)[1])
    for ci in range(n_chunks):
        g, lr = place(ci)
        if ci % per_group == 0:
            if g > 0:
                pending.append(finish(g - 1))
            if (g + 1) * group < block:
                pending.append(prep(g + 1))
        tick()
        if ci + 1 < n_chunks:
            g1, lr1 = place(ci + 1)
            while g1 not in preps:
                tick()
            firsts[ci + 1] = first_dots(ci + 1, preps[g1], lr1)
        tick()
        d = firsts.pop(ci)
        fac_max = jnp.maximum(fac_max, d['fac'])
        raws.setdefault(g, []).append(second_dots(ci, preps[g], lr, d))
        tick()
    while pending:
        tick()
    drain(finish(block // group - 1))


    @pl.when(jnp.logical_not(jnp.max(fac_max) < GLA_FACTOR_LIMIT))
    def _gla_per_token():
        gst_ref[...] = gprev_ref[...]

        def tokens8(g, carry):
            rows8 = pl.ds(pl.multiple_of(g * 8, 8), 8)
            alpha_t = jnp.exp(la_ref[rows8, :]).T
            k_t = z_ref[rows8, Z_GK:Z_GK + GLA_KW].T
            q_t = (z_ref[rows8, Z_GQ:Z_GQ + GLA_KW] * (GLA_DK ** -0.5)).T
            v8 = z_ref[rows8, Z_GV:Z_GV + GLA_W]
            s = gst_ref[...]
            outs = []
            for j in range(8):
                s = s * alpha_t[:, j:j + 1] + bm_gk * (k_t[:, j:j + 1] * v8[j:j + 1, :])
                outs.append(jnp.sum(q_t[:, j:j + 1] * s, axis=0, keepdims=True))
            gst_ref[...] = s
            mix_ref[rows8, 0:GLA_W] = jnp.concatenate(outs, axis=0)
            return carry

        lax.fori_loop(0, block // 8, tokens8, 0)

    o = mix_ref[:, 0:GLA_W]
    ms = _dot(o * o, on256)
    mix_ref[:, 0:GLA_W] = o * lax.rsqrt(ms + EPS) * gnorm_ref[...] * _silu(z_ref[:, Z_GG:Z_GG + GLA_W])

    @pl.when(step == last)
    def _finish():
        for h in range(GLA_HEADS):
            gout_ref[h] = gst_ref[h * GLA_DK:(h + 1) * GLA_DK, h * GLA_DV:(h + 1) * GLA_DV]
        for h in range(RET_HEADS):
            p, a = divmod(h, 2)
            rout_ref[h] = sst_ref[p, a * hd:(a + 1) * hd, a * hd:(a + 1) * hd]
            cout_ref[h] = cst_ref[p, a * hd:(a + 1) * hd, a * hd:(a + 1) * hd]
            nout_ref[h] = cst_ref[p, a * hd:(a + 1) * hd, LANES + a * hd:LANES + (a + 1) * hd]
        mout_ref[...] = mst_ref[...]


def _mixer_constants(block, chunk):
    t = np.arange(block)
    tril = ((t[:, None] // chunk == t[None, :] // chunk) & (t[None, :] <= t[:, None])).astype(np.float32)
    r = np.arange(2 * LANES)[:, None] % LANES
    col = np.arange(ML_HEADS * LANES)[None, :] // LANES
    lane = np.arange(LANES)[None, :]
    ind = np.concatenate([r == col + SM_FG, (r == lane + ML_HEADS) & (lane >= SM_IG) & (lane < SM_FG)],
                         axis=1).astype(np.float32)
    i256 = np.arange(GLA_W) // GLA_DV
    on256 = (i256[:, None] == i256[None, :]).astype(np.float32) / GLA_DV
    return tuple(jnp.asarray(m, BF16) for m in (tril, ind, on256))


def _mixer(z3, tabs, wts, init, layer):
    bsz, seq, _ = z3.shape
    block = min(MIXER_BLOCK, seq)
    chunk = min(MIXER_CHUNK, block)
    has_init = init is not None
    rdec, rint, rwend, rgam = tabs
    consts = _mixer_constants(min(MIXER_GROUP, block), chunk)

    def const(shape):
        nd = len(shape)
        return pl.BlockSpec(shape, lambda b, s: (0,) * nd)

    def per_layer(shape):
        nd = len(shape)
        return pl.BlockSpec((None,) + shape, lambda b, s: (layer,) + (0,) * nd)

    def per_batch(shape, with_layer):
        nd = len(shape)
        if with_layer:
            return pl.BlockSpec((None, None) + shape, lambda b, s: (layer, b) + (0,) * nd)
        return pl.BlockSpec((None,) + shape, lambda b, s: (b,) + (0,) * nd)

    state_shapes = [(GLA_HEADS, GLA_DK, GLA_DV), (RET_HEADS, RET_D, RET_D), (ML_HEADS, ML_D, ML_D),
                    (ML_HEADS, ML_D, ML_D), (1, ML_HEADS * LANES)]
    in_specs = [
        pl.BlockSpec((None, block, Z_W), lambda b, s: (b, s, 0)),
        const(rdec.shape), const(rint.shape), const(rwend.shape), const(rgam.shape),
    ] + [const(m.shape) for m in consts] + [
        per_layer((LANES, GLA_KW)), per_layer((1, GLA_KW)), per_layer((1, LANES)),
        per_layer((1, GLA_W)), per_layer((1, RET_W)), per_layer((1, ML_W)),
        per_layer((PAIRS, LANES, 2 * LANES)),
    ]
    args = [z3, rdec, rint, rwend, rgam] + list(consts) + list(wts)
    if has_init:
        in_specs += [per_batch(s, True) for s in state_shapes]
        args += list(init)
    out_specs = [pl.BlockSpec((None, block, D_MODEL), lambda b, s: (b, s, 0))]
    out_specs += [per_batch(s, False) for s in state_shapes]
    out_shape = [jax.ShapeDtypeStruct((bsz, seq, D_MODEL), F32)]
    out_shape += [jax.ShapeDtypeStruct((bsz,) + s, F32) for s in state_shapes]
    scratch = [
        pltpu.VMEM((GLA_KW, GLA_W), F32), pltpu.VMEM((PAIRS, LANES, LANES), F32),
        pltpu.VMEM((PAIRS, LANES, 2 * LANES), F32), pltpu.VMEM((1, ML_HEADS * LANES), F32),
        pltpu.VMEM((block, LANES), F32),
        pltpu.VMEM((GLA_KW, GLA_W), F32),
    ]
    return pl.pallas_call(
        functools.partial(_mixer_kernel, block=block, chunk=chunk, has_init=has_init),
        grid=(bsz, seq // block),
        in_specs=in_specs,
        out_specs=out_specs,
        out_shape=out_shape,
        scratch_shapes=scratch,
        compiler_params=pltpu.CompilerParams(dimension_semantics=("parallel", "arbitrary"),
                                             vmem_limit_bytes=VMEM_LIMIT),
        name="mixer",
    )(*args)


def _ffn_kernel(*refs, final):
    if final:
        x_ref, mix_ref, wo_ref, g_ref, wg_ref, wu_ref, wd_ref, gf_ref, out_ref, y_ref = refs
    else:
        x_ref, mix_ref, wo_ref, g_ref, wg_ref, wu_ref, wd_ref, out_ref = refs
    tm = x_ref.shape[0]
    halves = [slice(0, tm // 2), slice(tm // 2, tm)] if tm % 16 == 0 else [slice(0, tm)]
    accs = [x_ref[rows, :] + jnp.dot(mix_ref[rows, :].astype(BF16), wo_ref[...], preferred_element_type=F32)
            for rows in halves]
    hfs = [(x1 * lax.rsqrt(jnp.mean(x1 * x1, axis=-1, keepdims=True) + EPS) * g_ref[...]).astype(BF16)
           for x1 in accs]
    for c in range(D_FF // FF_CHUNK):
        cols = slice(c * FF_CHUNK, (c + 1) * FF_CHUNK)
        acts = []
        for hf in hfs:
            gate = jnp.dot(hf, wg_ref[:, cols], preferred_element_type=F32)
            up = jnp.dot(hf, wu_ref[:, cols], preferred_element_type=F32)
            acts.append((_silu(gate) * up).astype(BF16))
        accs = [acc + jnp.dot(act, wd_ref[cols, :], preferred_element_type=F32)
                for acc, act in zip(accs, acts)]
    for rows, acc in zip(halves, accs):
        out_ref[rows, :] = acc
        if final:
            rf = lax.rsqrt(jnp.mean(acc * acc, axis=-1, keepdims=True) + EPS)
            y_ref[rows, :] = acc * rf * gf_ref[...]


def _ffn(x2d, mix2d, wo, norm3, wg, wu, wd, layer, norm_final2):
    n = x2d.shape[0]
    tm = min(TOKEN_BLOCK, n)
    final = norm_final2 is not None

    def weight(shape):
        return pl.BlockSpec((None,) + shape, lambda i: (layer, 0, 0), pipeline_mode=pl.Buffered(1))

    row_spec = pl.BlockSpec((tm, D_MODEL), lambda i: (i, 0))
    in_specs = [row_spec, row_spec, weight((D_MODEL, D_MODEL)),
                pl.BlockSpec((None, 1, D_MODEL), lambda i: (layer, 0, 0)),
                weight((D_MODEL, D_FF)), weight((D_MODEL, D_FF)), weight((D_FF, D_MODEL))]
    args = [x2d, mix2d, wo, norm3, wg, wu, wd]
    out_specs = [row_spec]
    out_shape = [jax.ShapeDtypeStruct((n, D_MODEL), F32)]
    if final:
        in_specs.append(pl.BlockSpec((1, D_MODEL), lambda i: (0, 0)))
        args.append(norm_final2)
        out_specs.append(row_spec)
        out_shape.append(jax.ShapeDtypeStruct((n, D_MODEL), F32))
    return pl.pallas_call(
        functools.partial(_ffn_kernel, final=final),
        grid=(n // tm,),
        in_specs=in_specs,
        out_specs=out_specs,
        out_shape=out_shape,
        compiler_params=pltpu.CompilerParams(dimension_semantics=("parallel",),
                                             vmem_limit_bytes=VMEM_LIMIT),
        name="ffn",
    )(*args)


def _reorder_w_in(w_in):
    offs = [0]
    for sz in SPLIT_SIZES:
        offs.append(offs[-1] + sz)
    ga0, ga1, gates0 = offs[3], offs[4], offs[12]
    pad = jnp.zeros(w_in.shape[:-1] + (LANES - GLA_RANK - 2 * ML_HEADS,), w_in.dtype)
    w = jnp.concatenate([w_in[..., :ga0], w_in[..., ga1:gates0], w_in[..., ga0:ga1], w_in[..., gates0:], pad],
                        axis=-1)
    scale = np.ones((Z_W,), np.float32)
    scale[Z_RK:Z_RV] = RET_D ** -0.5
    return (w * scale).astype(BF16)


def _block_diag_pairs(w):
    depth, h, d, e = w.shape
    eye = jnp.eye(2, dtype=w.dtype)
    return jnp.einsum('lpade,ab->lpadbe', w.reshape(depth, h // 2, 2, d, e), eye).reshape(
        depth, h // 2, 2 * d, 2 * e)


def _rope_tables(pos):
    inv = ROPE_BASE ** (-jnp.arange(0, RET_D, 2, dtype=F32) / RET_D)
    ang = pos.astype(F32)[:, None] * inv[None, :]
    cos, sin = jnp.cos(ang), jnp.sin(ang)
    cos_h = jnp.concatenate([cos, cos], axis=-1)
    sin_h = jnp.concatenate([-sin, sin], axis=-1)
    return jnp.tile(cos_h, (1, RET_HEADS)), jnp.tile(sin_h, (1, RET_HEADS))


def _retention_tables(chunk):
    lg = jnp.log1p(-jnp.exp2(-5.0 - jnp.arange(RET_HEADS, dtype=F32)))
    t = jnp.arange(chunk, dtype=F32)
    diff = t[:, None] - t[None, :]
    decay = jnp.where((diff >= 0)[..., None], jnp.exp(jnp.maximum(diff, 0.0)[..., None] * lg), 0.0)
    decay = jnp.transpose(decay, (2, 0, 1)).reshape(PAIRS, 2 * chunk, chunk)
    inter = jnp.repeat(jnp.exp((t[:, None] + 1.0) * lg), RET_D, axis=1)
    w_end = jnp.exp((chunk - 1.0 - t)[None, :] * lg[:, None])
    w_end = jnp.repeat(w_end, RET_D, axis=0)
    gam = jnp.repeat(jnp.exp(chunk * lg), RET_D)[None, :]
    return decay, inter, w_end, gam


def kernel(x_prompt, x_sample, state_gla, state_ret, state_mlstm_c, state_mlstm_n, state_mlstm_m,
           cache_mlstm_conv, norm_mix, w_in, gla_w_a2, gla_b_a, gla_norm, ret_norm, ml_conv_w,
           ml_conv_b, ml_wq, ml_wk, ml_b_if, ml_norm, w_out, norm_ffn, w_gate, w_up, w_down, norm_final):
    depth = w_in.shape[0]

    w_in_p = _reorder_w_in(w_in)
    wo, wg, wu, wd = (w.astype(BF16) for w in (w_out, w_gate, w_up, w_down))
    wa2 = jnp.zeros((depth, LANES, GLA_KW), F32).at[:, :GLA_RANK, :].set(gla_w_a2).astype(BF16)
    bif = jnp.zeros((depth, 1, LANES), F32).at[:, 0, SM_IG:SM_IG + 2 * ML_HEADS].set(ml_b_if)
    wqk = jnp.concatenate([_block_diag_pairs(ml_wq), _block_diag_pairs(ml_wk) * (ML_D ** -0.5)],
                          axis=-1).astype(BF16)
    row = lambda a: a[:, None, :]
    wts = (wa2, row(gla_b_a), bif, row(gla_norm), row(ret_norm), row(ml_norm), wqk)
    conv_b3 = row(ml_conv_b)
    norm_mix3, norm_ffn3, norm_final2 = row(norm_mix), row(norm_ffn), norm_final[None, :]

    init_sample = (state_gla, state_ret, state_mlstm_c,
                   jnp.broadcast_to(state_mlstm_n[..., None], state_mlstm_n.shape + (ML_D,)),
                   jnp.repeat(state_mlstm_m, LANES, axis=-1)[:, :, None, :])

    def run(x, pos, init, conv_cache):
        bsz, seq, _ = x.shape
        chunk = min(MIXER_CHUNK, MIXER_BLOCK, seq)
        assert seq % chunk == 0 and chunk & (chunk - 1) == 0 and chunk >= 2 * 8
        rope = _rope_tables(pos)
        tabs = _retention_tables(chunk)
        x2d = x.reshape(bsz * seq, D_MODEL)
        states = []
        y2d = None
        for i in range(depth):
            z, cv = _inproj(x2d.reshape(bsz, seq, D_MODEL), norm_mix3, w_in_p, rope, ml_conv_w, conv_b3,
                            conv_cache, i)
            mixed, *st = _mixer(z, tabs, wts, init, i)
            outs = _ffn(x2d, mixed.reshape(bsz * seq, D_MODEL), wo, norm_ffn3, wg, wu, wd, i,
                        norm_final2 if i == depth - 1 else None)
            x2d = outs[0]
            if i == depth - 1:
                y2d = outs[1]
            states.append(st + [cv])
        g, r, c, n, m, cv = (jnp.stack([st[j] for st in states]) for j in range(6))
        return (y2d.reshape(bsz, seq, D_MODEL), g, r, c,
                n[..., 0], m[:, :, 0, ::LANES], cv)

    pos_p = jnp.arange(x_prompt.shape[1], dtype=jnp.int32)
    pos_s = PAST_LEN + jnp.arange(x_sample.shape[1], dtype=jnp.int32)
    yp, gp, rp, cp, np_, mp, cvp = run(x_prompt, pos_p, None, None)
    ys, gs, rs, cs, ns, ms, cvs = run(x_sample, pos_s, init_sample, cache_mlstm_conv)
    return (yp, ys, gp, gs, rp, rs, cp, cs, np_, ns, mp, ms, cvp, cvs)
```

```python
import functools

import numpy as np
import jax
import jax.numpy as jnp
from jax import lax
from jax.experimental import pallas as pl
from jax.experimental.pallas import tpu as pltpu

F32 = jnp.float32
BF16 = jnp.bfloat16

D_MODEL = 1024
EPS = 1e-6
GLA_HEADS, GLA_DK, GLA_DV, GLA_RANK, GLA_TAU = 4, 32, 64, 16, 16.0
RET_HEADS, RET_D = 6, 64
ML_HEADS, ML_D = 6, 64
CONV_W = 4
ROPE_BASE = 10000.0
PAST_LEN = 2048
GLA_KW = GLA_HEADS * GLA_DK
GLA_W = GLA_HEADS * GLA_DV
RET_W = RET_HEADS * RET_D
ML_W = ML_HEADS * ML_D
D_FF = 2816
SPLIT_SIZES = (GLA_KW, GLA_KW, GLA_W, GLA_RANK, GLA_W, RET_W, RET_W, RET_W, RET_W, ML_W, ML_W, ML_W,
               ML_HEADS, ML_HEADS)

LANES = 128
PAIRS = RET_W // LANES
Z_GQ, Z_GK, Z_GV, Z_GG = 0, 128, 256, 512
Z_RQ, Z_RK, Z_RV, Z_RG = 768, 1152, 1536, 1920
Z_MU, Z_MV, Z_MO = 2304, 2688, 3072
Z_SM = 3456
Z_W = 3584
SM_IG, SM_FG = GLA_RANK, GLA_RANK + ML_HEADS
MIX_RET, MIX_ML = GLA_W, GLA_W + RET_W

TOKEN_BLOCK = 512
MIXER_BLOCK = 1024
MIXER_CHUNK = 128
MIXER_GROUP = 256
FF_CHUNK = 1408
LOG2E = 1.4426950408889634
GLA_FACTOR_LIMIT = 1e18
VMEM_LIMIT = 56 * 1024 * 1024

_NT = (((1,), (1,)), ((), ()))


def _dot(a, b):
    return jnp.dot(a.astype(BF16), b.astype(BF16), preferred_element_type=F32)


def _dot_nt(a, b):
    return lax.dot_general(a.astype(BF16), b.astype(BF16), _NT, preferred_element_type=F32)


def _split2(x):
    hi = x.astype(BF16)
    return [hi, (x - hi.astype(F32)).astype(BF16)]


def _log_sigmoid(x):
    return jnp.minimum(x, 0.0) - jnp.log(1.0 + jnp.exp(-jnp.abs(x)))


def _sigmoid(x):
    return 0.5 + 0.5 * jnp.tanh(0.5 * x)


def _silu(x):
    h = 0.5 * x
    return h + h * jnp.tanh(h)


def _iota(shape, dim):
    return lax.broadcasted_iota(jnp.int32, shape, dim)


def _inproj_kernel(*refs, tm, has_init):
    if has_init:
        x_ref, g_ref, w_ref, cos_ref, sin_ref, cw_ref, cb_ref, cv0_ref, z_ref, cvout_ref, xp_ref = refs
    else:
        x_ref, g_ref, w_ref, cos_ref, sin_ref, cw_ref, cb_ref, z_ref, cvout_ref, xp_ref = refs
    conv_lo = 8 - (CONV_W - 1)

    @pl.when(pl.program_id(1) == 0)
    def _start_of_sequence():
        if has_init:
            xp_ref[conv_lo:8, :] = cv0_ref[...]
        else:
            xp_ref[conv_lo:8, :] = jnp.zeros((CONV_W - 1, ML_W), F32)

    x = x_ref[...]
    r = lax.rsqrt(jnp.mean(x * x, axis=-1, keepdims=True) + EPS)
    h = (x * r * g_ref[...]).astype(BF16)

    def proj(lo, hi):
        return jnp.dot(h, w_ref[:, lo:hi], preferred_element_type=F32)

    z_ref[:, Z_GQ:Z_RQ] = proj(Z_GQ, Z_RQ)
    seg = proj(Z_RQ, Z_RV)
    lo32 = (_iota((tm, LANES), 1) & 32) == 0
    for p in range(2 * PAIRS):
        lanes = slice((p % PAIRS) * LANES, (p % PAIRS + 1) * LANES)
        t = seg[:, p * LANES:(p + 1) * LANES]
        sw = jnp.where(lo32, pltpu.roll(t, LANES - 32, 1), pltpu.roll(t, 32, 1))
        z_ref[:, Z_RQ + p * LANES:Z_RQ + (p + 1) * LANES] = t * cos_ref[:, lanes] + sw * sin_ref[:, lanes]
    z_ref[:, Z_RV:Z_MU] = proj(Z_RV, Z_MU)
    seg = proj(Z_MU, Z_MO)
    z_ref[:, Z_MV:Z_MO] = seg[:, ML_W:]
    xp_ref[8:8 + tm, :] = seg[:, :ML_W]
    cw = cw_ref[...]
    conv = cb_ref[...]
    for j in range(CONV_W):
        conv = conv + xp_ref[conv_lo + j:conv_lo + j + tm, :] * cw[j:j + 1, :]
    z_ref[:, Z_MU:Z_MV] = _silu(conv)
    conv_tail = xp_ref[tm + conv_lo:tm + 8, :]
    xp_ref[conv_lo:8, :] = conv_tail
    cvout_ref[...] = conv_tail
    z_ref[:, Z_MO:Z_W] = proj(Z_MO, Z_W)


def _inproj(x3, norm3, w_in_p, rope, cw, cb, cv0, layer):
    bsz, seq, _ = x3.shape
    tm = min(TOKEN_BLOCK, seq)
    has_init = cv0 is not None
    cos, sin = rope
    in_specs = [
        pl.BlockSpec((None, tm, D_MODEL), lambda b, s: (b, s, 0)),
        pl.BlockSpec((None, 1, D_MODEL), lambda b, s: (layer, 0, 0)),
        pl.BlockSpec((None, D_MODEL, Z_W), lambda b, s: (layer, 0, 0), pipeline_mode=pl.Buffered(1)),
        pl.BlockSpec((tm, RET_W), lambda b, s: (s, 0)),
        pl.BlockSpec((tm, RET_W), lambda b, s: (s, 0)),
        pl.BlockSpec((None, CONV_W, ML_W), lambda b, s: (layer, 0, 0)),
        pl.BlockSpec((None, 1, ML_W), lambda b, s: (layer, 0, 0)),
    ]
    args = [x3, norm3, w_in_p, cos, sin, cw, cb]
    if has_init:
        in_specs.append(pl.BlockSpec((None, None, CONV_W - 1, ML_W), lambda b, s: (layer, b, 0, 0)))
        args.append(cv0)
    return pl.pallas_call(
        functools.partial(_inproj_kernel, tm=tm, has_init=has_init),
        grid=(bsz, seq // tm),
        in_specs=in_specs,
        out_specs=[pl.BlockSpec((None, tm, Z_W), lambda b, s: (b, s, 0)),
                   pl.BlockSpec((None, CONV_W - 1, ML_W), lambda b, s: (b, 0, 0))],
        out_shape=[jax.ShapeDtypeStruct((bsz, seq, Z_W), F32),
                   jax.ShapeDtypeStruct((bsz, CONV_W - 1, ML_W), F32)],
        scratch_shapes=[pltpu.VMEM((tm + 8, ML_W), F32)],
        compiler_params=pltpu.CompilerParams(dimension_semantics=("parallel", "arbitrary"),
                                             vmem_limit_bytes=VMEM_LIMIT),
        name="inproj",
    )(*args)


def _mixer_kernel(*refs, block, chunk, has_init):
    it = iter(refs)
    z_ref, rdec_ref, rint_ref, rwend_ref, rgam_ref, tril_ref, ind_ref, on256_ref = (next(it) for _ in range(8))
    wa2_ref, ba_ref, bif_ref, gnorm_ref, rnorm_ref, mnorm_ref, wqk_ref = (next(it) for _ in range(7))
    if has_init:
        g0_ref, r0_ref, c0_ref, n0_ref, m0_ref = (next(it) for _ in range(5))
    mix_ref, gout_ref, rout_ref, cout_ref, nout_ref, mout_ref = (next(it) for _ in range(6))
    gst_ref, sst_ref, cst_ref, mst_ref, la_ref, gprev_ref = (next(it) for _ in range(6))

    step = pl.program_id(1)
    last = pl.num_programs(1) - 1
    c = chunk
    hd = RET_D
    group = min(MIXER_GROUP, block)

    @pl.when(step == 0)
    def _init():
        gst_ref[...] = jnp.zeros_like(gst_ref)
        sst_ref[...] = jnp.zeros_like(sst_ref)
        cst_ref[...] = jnp.zeros_like(cst_ref)
        if has_init:
            for h in range(GLA_HEADS):
                gst_ref[h * GLA_DK:(h + 1) * GLA_DK, h * GLA_DV:(h + 1) * GLA_DV] = g0_ref[h]
            for h in range(RET_HEADS):
                p, a = divmod(h, 2)
                sst_ref[p, a * hd:(a + 1) * hd, a * hd:(a + 1) * hd] = r0_ref[h]
                cst_ref[p, a * hd:(a + 1) * hd, a * hd:(a + 1) * hd] = c0_ref[h]
                cst_ref[p, a * hd:(a + 1) * hd, LANES + a * hd:LANES + (a + 1) * hd] = n0_ref[h]
            mst_ref[...] = m0_ref[...] * LOG2E
        else:
            mst_ref[...] = jnp.zeros_like(mst_ref)

    lane_c = _iota((c, LANES), 1)
    lane_lo = (lane_c & hd) == 0
    pair_mask = [jnp.where(lane_lo, 1.0, 0.0).astype(BF16), jnp.where(lane_lo, 0.0, 1.0).astype(BF16)]
    gla_mask = [jnp.where((lane_c >> 5) == h, 1.0, 0.0).astype(BF16) for h in range(GLA_HEADS)]
    gv_mask = [jnp.where((_iota((c, GLA_W), 1) >> 6) == h, 1.0, 0.0).astype(BF16) for h in range(GLA_HEADS)]
    ones_b = jnp.ones((c, LANES), BF16)
    neg_inf = jnp.float32(-jnp.inf)
    causal = _iota((c, c), 1) <= _iota((c, c), 0)
    causal4 = _iota((GLA_HEADS * c, c), 1) <= (_iota((GLA_HEADS * c, c), 0) & (c - 1))
    bm_gk = jnp.where((_iota((GLA_KW, GLA_W), 0) >> 5) == (_iota((GLA_KW, GLA_W), 1) >> 6), 1.0, 0.0)
    bm_pair = jnp.where((_iota((LANES, LANES), 0) >> 6) == (_iota((LANES, LANES), 1) >> 6), 1.0, 0.0)
    bm_aug = jnp.concatenate([bm_pair, bm_pair], axis=1)
    sub_lo = _iota((LANES, c), 0) < hd

    def stack_pair(xb):
        return jnp.concatenate([xb * pair_mask[0], xb * pair_mask[1]], axis=0)

    gprev_ref[...] = gst_ref[...]
    on256 = on256_ref[...]

    def prep(g):
        rs = slice(g * group, (g + 1) * group)
        qk = [_dot(z_ref[rs, Z_MU + p * LANES:Z_MU + (p + 1) * LANES], wqk_ref[p]) for p in range(PAIRS)]
        out = {'qm': jnp.concatenate([t[:, :LANES] for t in qk], axis=1),
               'km': jnp.concatenate([t[:, LANES:] for t in qk], axis=1)}
        small = z_ref[rs, Z_SM:Z_SM + LANES]
        la = _log_sigmoid(_dot(small, wa2_ref[...]) + ba_ref[...]) * (LOG2E / GLA_TAU)
        gates = small + bif_ref[...]
        lf = _log_sigmoid(gates) * LOG2E
        yield
        cum = jnp.dot(tril_ref[...], jnp.concatenate(_split2(la) + _split2(lf), axis=1),
                      preferred_element_type=F32)
        out['bg'] = cum[:, :LANES] + cum[:, LANES:2 * LANES]
        la_ref[rs, :] = la
        bt = cum[:, 2 * LANES:3 * LANES] + cum[:, 3 * LANES:]
        yield
        bxe = jnp.dot(jnp.concatenate(_split2(bt), axis=1), ind_ref[...], preferred_element_type=F32)
        out['bx'] = bxe[:, :ML_HEADS * LANES]
        out['dg'] = gates * LOG2E - bxe[:, ML_HEADS * LANES:]
        preps[g] = out

    def finish(g):
        rs = slice(g * group, (g + 1) * group)
        raw = [jnp.concatenate(t, axis=0) for t in zip(*raws.pop(g))]
        n_col = (RET_W + ML_W) // GLA_W
        os_ = [jnp.concatenate(raw[2 * k:2 * k + 2], axis=1) for k in range(n_col)]
        ocs = [o - _dot(o, on256) for o in os_]
        yield
        ons = [oc * lax.rsqrt(_dot(oc * oc, on256) + EPS) for oc in ocs]
        yield
        for k in range(n_col):
            on = ons[k]
            for j in range(GLA_W // LANES):
                t = (GLA_W // LANES) * k + j
                tile = on[:, j * LANES:(j + 1) * LANES]
                if t < PAIRS:
                    tile = (tile * rnorm_ref[:, t * LANES:(t + 1) * LANES]
                            * _silu(z_ref[rs, Z_RG + t * LANES:Z_RG + (t + 1) * LANES]))
                else:
                    tile = tile * mnorm_ref[:, (t - PAIRS) * LANES:(t - PAIRS + 1) * LANES]
                mix_ref[rs, MIX_RET + t * LANES:MIX_RET + (t + 1) * LANES] = tile

    per_group = group // c
    n_chunks = block // c

    def first_dots(ci, pre, lr):
        rows = slice(ci * c, (ci + 1) * c)
        d = {}
        b = pre['bg'][lr]
        bmid = b[c // 2 - 1:c // 2, :]
        bl = b[c - 1:c, :]
        gq = z_ref[rows, Z_GQ:Z_GQ + GLA_KW] * (GLA_DK ** -0.5)
        gk = z_ref[rows, Z_GK:Z_GK + GLA_KW]
        eq = jnp.exp2(b - bmid)
        ek = jnp.exp2(bmid - b)
        qh = gq * eq
        kh = gk * ek
        fac = jnp.maximum(jnp.maximum(jnp.abs(qh), jnp.abs(kh)), jnp.maximum(eq, ek))
        d['fac'] = functools.reduce(jnp.maximum, [fac[i:i + 8] for i in range(0, c, 8)])
        qh = qh.astype(BF16)
        d['qe'] = (gq * jnp.exp2(b)).astype(BF16)
        kl_t = (gk * jnp.exp2(bl - b)).T
        lhs = jnp.concatenate([qh * gla_mask[h] for h in range(GLA_HEADS)], axis=0)
        a = lax.dot_general(lhs, kh.astype(BF16), _NT, preferred_element_type=F32)
        d['a'] = jnp.where(causal4, a, 0.0).astype(BF16)
        d['gvb'] = z_ref[rows, Z_GV:Z_GV + GLA_W].astype(BF16)
        d['gupd'] = _dot(kl_t, d['gvb'])
        d['bl_col'] = b[c - 8:c, :].T[:, 7:8]
        d['rvb'] = z_ref[rows, Z_RV:Z_RV + RET_W].astype(BF16)
        kr = z_ref[rows, Z_RK:Z_RK + RET_W]
        d['qr'] = z_ref[rows, Z_RQ:Z_RQ + RET_W]
        d['rpm'] = [(lax.dot_general(stack_pair(d['qr'][:, p * LANES:(p + 1) * LANES].astype(BF16)),
                                     kr[:, p * LANES:(p + 1) * LANES].astype(BF16), _NT,
                                     preferred_element_type=F32) * rdec_ref[p]).astype(BF16)
                    for p in range(PAIRS)]
        kr_t = (kr.T * rwend_ref[...]).astype(BF16)
        d['rupd'] = [_dot(kr_t[p * LANES:(p + 1) * LANES], d['rvb'][:, p * LANES:(p + 1) * LANES])
                     for p in range(PAIRS)]
        d['mvb'] = z_ref[rows, Z_MV:Z_MV + ML_W].astype(BF16)
        km = pre['km'][lr]
        d['km_t'] = km.T
        d['ms2'] = [lax.dot_general(stack_pair(pre['qm'][lr, p * LANES:(p + 1) * LANES].astype(BF16)),
                                    km[:, p * LANES:(p + 1) * LANES].astype(BF16), _NT,
                                    preferred_element_type=F32) for p in range(PAIRS)]
        dg_t = pre['dg'][lr].T
        d['dg_t'] = dg_t
        d['mrow'] = [jnp.max(jnp.where(causal, pre['bx'][lr, h * LANES:h * LANES + c]
                                       + dg_t[SM_IG + h:SM_IG + h + 1, :], neg_inf), axis=1, keepdims=True)
                     for h in range(ML_HEADS)]
        return d

    def second_dots(ci, pre, lr, d):
        rows = slice(ci * c, (ci + 1) * c)
        raw = []
        a = d['a']
        o = jnp.dot(jnp.concatenate([a[h * c:(h + 1) * c] for h in range(GLA_HEADS)] + [d['qe']], axis=1),
                    jnp.concatenate([d['gvb'] * gv_mask[h] for h in range(GLA_HEADS)]
                                    + [gst_ref[...].astype(BF16)], axis=0),
                    preferred_element_type=F32)
        mix_ref[rows, 0:GLA_W] = o
        gst_ref[...] = gst_ref[...] * jnp.exp2(d['bl_col']) + bm_gk * d['gupd']
        for p in range(PAIRS):
            lanes = slice(p * LANES, (p + 1) * LANES)
            vb = d['rvb'][:, lanes]
            pm = d['rpm'][p]
            raw.append(jnp.dot(
                jnp.concatenate([pm[:c], pm[c:], (d['qr'][:, lanes] * rint_ref[:, lanes]).astype(BF16)], axis=1),
                jnp.concatenate([vb * pair_mask[0], vb * pair_mask[1], sst_ref[p].astype(BF16)], axis=0),
                preferred_element_type=F32))
            sst_ref[p] = sst_ref[p] * rgam_ref[:, lanes] + bm_pair * d['rupd'][p]
        dg_t = d['dg_t']
        wls, w0ls = [], []
        for p in range(PAIRS):
            lanes = slice(p * LANES, (p + 1) * LANES)
            wd, w0, mt = [], [], []
            for a_ in range(2):
                h = 2 * p + a_
                colr = pre['bx'][lr, h * LANES:(h + 1) * LANES]
                logd = jnp.where(causal, colr[:, :c] + dg_t[SM_IG + h:SM_IG + h + 1, :], neg_inf)
                log0 = colr + mst_ref[:, h * LANES:(h + 1) * LANES]
                m_t = jnp.maximum(log0, d['mrow'][h])
                wd.append(jnp.exp2(logd - m_t[:, :c]))
                w0.append(jnp.exp2(log0 - m_t))
                mt.append(m_t)
                mst_ref[:, h * LANES:(h + 1) * LANES] = m_t[c - 1:c, :]
            qp = pre['qm'][lr, lanes]
            vb = d['mvb'][:, lanes]
            a0 = (d['ms2'][p][:c] * wd[0]).astype(BF16)
            a1 = (d['ms2'][p][c:] * wd[1]).astype(BF16)
            w0p = jnp.where(lane_lo, w0[0], w0[1])
            mtp = jnp.where(lane_lo, mt[0], mt[1])
            nd = jnp.dot(
                jnp.concatenate([a0, a1, (qp * w0p).astype(BF16)], axis=1),
                jnp.concatenate([jnp.concatenate([vb * pair_mask[0], pair_mask[0]], axis=1),
                                 jnp.concatenate([vb * pair_mask[1], pair_mask[1]], axis=1),
                                 cst_ref[p].astype(BF16)], axis=0),
                preferred_element_type=F32)
            hh = nd[:, :LANES] / jnp.maximum(jnp.abs(nd[:, LANES:]), jnp.exp2(-mtp))
            raw.append(hh * _sigmoid(z_ref[rows, Z_MO + p * LANES:Z_MO + (p + 1) * LANES]))
            wls.append(jnp.where(sub_lo, wd[0][c - 1:c, :], wd[1][c - 1:c, :]))
            w0l = w0p[c - 1:c, :]
            w0ls.append(jnp.concatenate([w0l, w0l], axis=1))
        for p in range(PAIRS):
            lanes = slice(p * LANES, (p + 1) * LANES)
            upd = _dot(d['km_t'][lanes] * wls[p], jnp.concatenate([d['mvb'][:, lanes], ones_b], axis=1))
            cst_ref[p] = cst_ref[p] * w0ls[p] + bm_aug * upd
        return raw

    preps, raws, firsts, pending = {}, {}, {}, []

    def drain(gen):
        for _ in gen:
            pass

    def tick():
        pending[:] = [gen for gen in pending if next(gen, pending) is not pending]

    fac_max = jnp.zeros((8, LANES), F32)
    place = lambda ci: (ci // per_group, slice((ci % per_group) * c, (ci % per_group + 1) * c))
    drain(prep(0))
    firsts[0] = first_dots(0, preps[0], place(0)[1])
    for ci in range(n_chunks):
        g, lr = place(ci)
        if ci % per_group == 0:
            if g > 0:
                pending.append(finish(g - 1))
            if (g + 1) * group < block:
                pending.append(prep(g + 1))
        tick()
        if ci + 1 < n_chunks:
            g1, lr1 = place(ci + 1)
            while g1 not in preps:
                tick()
            firsts[ci + 1] = first_dots(ci + 1, preps[g1], lr1)
        tick()
        d = firsts.pop(ci)
        fac_max = jnp.maximum(fac_max, d['fac'])
        raws.setdefault(g, []).append(second_dots(ci, preps[g], lr, d))
        tick()
    while pending:
        tick()
    drain(finish(block // group - 1))


    @pl.when(jnp.logical_not(jnp.max(fac_max) < GLA_FACTOR_LIMIT))
    def _gla_per_token():
        gst_ref[...] = gprev_ref[...]

        def tokens8(g, carry):
            rows8 = pl.ds(pl.multiple_of(g * 8, 8), 8)
            alpha_t = jnp.exp2(la_ref[rows8, :]).T
            k_t = z_ref[rows8, Z_GK:Z_GK + GLA_KW].T
            q_t = (z_ref[rows8, Z_GQ:Z_GQ + GLA_KW] * (GLA_DK ** -0.5)).T
            v8 = z_ref[rows8, Z_GV:Z_GV + GLA_W]
            s = gst_ref[...]
            outs = []
            for j in range(8):
                s = s * alpha_t[:, j:j + 1] + bm_gk * (k_t[:, j:j + 1] * v8[j:j + 1, :])
                outs.append(jnp.sum(q_t[:, j:j + 1] * s, axis=0, keepdims=True))
            gst_ref[...] = s
            mix_ref[rows8, 0:GLA_W] = jnp.concatenate(outs, axis=0)
            return carry

        lax.fori_loop(0, block // 8, tokens8, 0)

    o = mix_ref[:, 0:GLA_W]
    ms = _dot(o * o, on256)
    mix_ref[:, 0:GLA_W] = o * lax.rsqrt(ms + EPS) * gnorm_ref[...] * _silu(z_ref[:, Z_GG:Z_GG + GLA_W])

    @pl.when(step == last)
    def _finish():
        for h in range(GLA_HEADS):
            gout_ref[h] = gst_ref[h * GLA_DK:(h + 1) * GLA_DK, h * GLA_DV:(h + 1) * GLA_DV]
        for h in range(RET_HEADS):
            p, a = divmod(h, 2)
            rout_ref[h] = sst_ref[p, a * hd:(a + 1) * hd, a * hd:(a + 1) * hd]
            cout_ref[h] = cst_ref[p, a * hd:(a + 1) * hd, a * hd:(a + 1) * hd]
            nout_ref[h] = cst_ref[p, a * hd:(a + 1) * hd, LANES + a * hd:LANES + (a + 1) * hd]
        mout_ref[...] = mst_ref[...] * (1.0 / LOG2E)


def _mixer_constants(block, chunk):
    t = np.arange(block)
    tril = ((t[:, None] // chunk == t[None, :] // chunk) & (t[None, :] <= t[:, None])).astype(np.float32)
    r = np.arange(2 * LANES)[:, None] % LANES
    col = np.arange(ML_HEADS * LANES)[None, :] // LANES
    lane = np.arange(LANES)[None, :]
    ind = np.concatenate([r == col + SM_FG, (r == lane + ML_HEADS) & (lane >= SM_IG) & (lane < SM_FG)],
                         axis=1).astype(np.float32)
    i256 = np.arange(GLA_W) // GLA_DV
    on256 = (i256[:, None] == i256[None, :]).astype(np.float32) / GLA_DV
    return tuple(jnp.asarray(m, BF16) for m in (tril, ind, on256))


def _mixer(z3, tabs, wts, init, layer):
    bsz, seq, _ = z3.shape
    block = min(MIXER_BLOCK, seq)
    chunk = min(MIXER_CHUNK, block)
    has_init = init is not None
    rdec, rint, rwend, rgam = tabs
    consts = _mixer_constants(min(MIXER_GROUP, block), chunk)

    def const(shape):
        nd = len(shape)
        return pl.BlockSpec(shape, lambda b, s: (0,) * nd)

    def per_layer(shape):
        nd = len(shape)
        return pl.BlockSpec((None,) + shape, lambda b, s: (layer,) + (0,) * nd)

    def per_batch(shape, with_layer):
        nd = len(shape)
        if with_layer:
            return pl.BlockSpec((None, None) + shape, lambda b, s: (layer, b) + (0,) * nd)
        return pl.BlockSpec((None,) + shape, lambda b, s: (b,) + (0,) * nd)

    state_shapes = [(GLA_HEADS, GLA_DK, GLA_DV), (RET_HEADS, RET_D, RET_D), (ML_HEADS, ML_D, ML_D),
                    (ML_HEADS, ML_D, ML_D), (1, ML_HEADS * LANES)]
    in_specs = [
        pl.BlockSpec((None, block, Z_W), lambda b, s: (b, s, 0)),
        const(rdec.shape), const(rint.shape), const(rwend.shape), const(rgam.shape),
    ] + [const(m.shape) for m in consts] + [
        per_layer((LANES, GLA_KW)), per_layer((1, GLA_KW)), per_layer((1, LANES)),
        per_layer((1, GLA_W)), per_layer((1, RET_W)), per_layer((1, ML_W)),
        per_layer((PAIRS, LANES, 2 * LANES)),
    ]
    args = [z3, rdec, rint, rwend, rgam] + list(consts) + list(wts)
    if has_init:
        in_specs += [per_batch(s, True) for s in state_shapes]
        args += list(init)
    out_specs = [pl.BlockSpec((None, block, D_MODEL), lambda b, s: (b, s, 0))]
    out_specs += [per_batch(s, False) for s in state_shapes]
    out_shape = [jax.ShapeDtypeStruct((bsz, seq, D_MODEL), F32)]
    out_shape += [jax.ShapeDtypeStruct((bsz,) + s, F32) for s in state_shapes]
    scratch = [
        pltpu.VMEM((GLA_KW, GLA_W), F32), pltpu.VMEM((PAIRS, LANES, LANES), F32),
        pltpu.VMEM((PAIRS, LANES, 2 * LANES), F32), pltpu.VMEM((1, ML_HEADS * LANES), F32),
        pltpu.VMEM((block, LANES), F32),
        pltpu.VMEM((GLA_KW, GLA_W), F32),
    ]
    return pl.pallas_call(
        functools.partial(_mixer_kernel, block=block, chunk=chunk, has_init=has_init),
        grid=(bsz, seq // block),
        in_specs=in_specs,
        out_specs=out_specs,
        out_shape=out_shape,
        scratch_shapes=scratch,
        compiler_params=pltpu.CompilerParams(dimension_semantics=("parallel", "arbitrary"),
                                             vmem_limit_bytes=VMEM_LIMIT),
        name="mixer",
    )(*args)


def _ffn_kernel(*refs, final):
    if final:
        x_ref, mix_ref, wo_ref, g_ref, wg_ref, wu_ref, wd_ref, gf_ref, out_ref, y_ref = refs
    else:
        x_ref, mix_ref, wo_ref, g_ref, wg_ref, wu_ref, wd_ref, out_ref = refs
    x1 = x_ref[...] + jnp.dot(mix_ref[...].astype(BF16), wo_ref[...], preferred_element_type=F32)
    r = lax.rsqrt(jnp.mean(x1 * x1, axis=-1, keepdims=True) + EPS)
    hf = (x1 * r * g_ref[...]).astype(BF16)
    acc = x1
    for c in range(D_FF // FF_CHUNK):
        cols = slice(c * FF_CHUNK, (c + 1) * FF_CHUNK)
        gate = jnp.dot(hf, wg_ref[:, cols], preferred_element_type=F32)
        up = jnp.dot(hf, wu_ref[:, cols], preferred_element_type=F32)
        act = (_silu(gate) * up).astype(BF16)
        acc = acc + jnp.dot(act, wd_ref[cols, :], preferred_element_type=F32)
    out_ref[...] = acc
    if final:
        rf = lax.rsqrt(jnp.mean(acc * acc, axis=-1, keepdims=True) + EPS)
        y_ref[...] = acc * rf * gf_ref[...]


def _ffn(x2d, mix2d, wo, norm3, wg, wu, wd, layer, norm_final2):
    n = x2d.shape[0]
    tm = min(TOKEN_BLOCK, n)
    final = norm_final2 is not None

    def weight(shape):
        return pl.BlockSpec((None,) + shape, lambda i: (layer, 0, 0), pipeline_mode=pl.Buffered(1))

    row_spec = pl.BlockSpec((tm, D_MODEL), lambda i: (i, 0))
    in_specs = [row_spec, row_spec, weight((D_MODEL, D_MODEL)),
                pl.BlockSpec((None, 1, D_MODEL), lambda i: (layer, 0, 0)),
                weight((D_MODEL, D_FF)), weight((D_MODEL, D_FF)), weight((D_FF, D_MODEL))]
    args = [x2d, mix2d, wo, norm3, wg, wu, wd]
    out_specs = [row_spec]
    out_shape = [jax.ShapeDtypeStruct((n, D_MODEL), F32)]
    if final:
        in_specs.append(pl.BlockSpec((1, D_MODEL), lambda i: (0, 0)))
        args.append(norm_final2)
        out_specs.append(row_spec)
        out_shape.append(jax.ShapeDtypeStruct((n, D_MODEL), F32))
    return pl.pallas_call(
        functools.partial(_ffn_kernel, final=final),
        grid=(n // tm,),
        in_specs=in_specs,
        out_specs=out_specs,
        out_shape=out_shape,
        compiler_params=pltpu.CompilerParams(dimension_semantics=("parallel",),
                                             vmem_limit_bytes=VMEM_LIMIT),
        name="ffn",
    )(*args)


def _reorder_w_in(w_in):
    offs = [0]
    for sz in SPLIT_SIZES:
        offs.append(offs[-1] + sz)
    ga0, ga1, gates0 = offs[3], offs[4], offs[12]
    pad = jnp.zeros(w_in.shape[:-1] + (LANES - GLA_RANK - 2 * ML_HEADS,), w_in.dtype)
    w = jnp.concatenate([w_in[..., :ga0], w_in[..., ga1:gates0], w_in[..., ga0:ga1], w_in[..., gates0:], pad],
                        axis=-1)
    scale = np.ones((Z_W,), np.float32)
    scale[Z_RK:Z_RV] = RET_D ** -0.5
    return (w * scale).astype(BF16)


def _block_diag_pairs(w):
    depth, h, d, e = w.shape
    eye = jnp.eye(2, dtype=w.dtype)
    return jnp.einsum('lpade,ab->lpadbe', w.reshape(depth, h // 2, 2, d, e), eye).reshape(
        depth, h // 2, 2 * d, 2 * e)


def _rope_tables(pos):
    inv = ROPE_BASE ** (-jnp.arange(0, RET_D, 2, dtype=F32) / RET_D)
    ang = pos.astype(F32)[:, None] * inv[None, :]
    cos, sin = jnp.cos(ang), jnp.sin(ang)
    cos_h = jnp.concatenate([cos, cos], axis=-1)
    sin_h = jnp.concatenate([-sin, sin], axis=-1)
    return jnp.tile(cos_h, (1, RET_HEADS)), jnp.tile(sin_h, (1, RET_HEADS))


def _retention_tables(chunk):
    lg = jnp.log1p(-jnp.exp2(-5.0 - jnp.arange(RET_HEADS, dtype=F32)))
    t = jnp.arange(chunk, dtype=F32)
    diff = t[:, None] - t[None, :]
    decay = jnp.where((diff >= 0)[..., None], jnp.exp(jnp.maximum(diff, 0.0)[..., None] * lg), 0.0)
    decay = jnp.transpose(decay, (2, 0, 1)).reshape(PAIRS, 2 * chunk, chunk)
    inter = jnp.repeat(jnp.exp((t[:, None] + 1.0) * lg), RET_D, axis=1)
    w_end = jnp.exp((chunk - 1.0 - t)[None, :] * lg[:, None])
    w_end = jnp.repeat(w_end, RET_D, axis=0)
    gam = jnp.repeat(jnp.exp(chunk * lg), RET_D)[None, :]
    return decay, inter, w_end, gam


def kernel(x_prompt, x_sample, state_gla, state_ret, state_mlstm_c, state_mlstm_n, state_mlstm_m,
           cache_mlstm_conv, norm_mix, w_in, gla_w_a2, gla_b_a, gla_norm, ret_norm, ml_conv_w,
           ml_conv_b, ml_wq, ml_wk, ml_b_if, ml_norm, w_out, norm_ffn, w_gate, w_up, w_down, norm_final):
    depth = w_in.shape[0]

    w_in_p = _reorder_w_in(w_in)
    wo, wg, wu, wd = (w.astype(BF16) for w in (w_out, w_gate, w_up, w_down))
    wa2 = jnp.zeros((depth, LANES, GLA_KW), F32).at[:, :GLA_RANK, :].set(gla_w_a2).astype(BF16)
    bif = jnp.zeros((depth, 1, LANES), F32).at[:, 0, SM_IG:SM_IG + 2 * ML_HEADS].set(ml_b_if)
    wqk = jnp.concatenate([_block_diag_pairs(ml_wq), _block_diag_pairs(ml_wk) * (ML_D ** -0.5)],
                          axis=-1).astype(BF16)
    row = lambda a: a[:, None, :]
    wts = (wa2, row(gla_b_a), bif, row(gla_norm), row(ret_norm), row(ml_norm), wqk)
    conv_b3 = row(ml_conv_b)
    norm_mix3, norm_ffn3, norm_final2 = row(norm_mix), row(norm_ffn), norm_final[None, :]

    init_sample = (state_gla, state_ret, state_mlstm_c,
                   jnp.broadcast_to(state_mlstm_n[..., None], state_mlstm_n.shape + (ML_D,)),
                   jnp.repeat(state_mlstm_m, LANES, axis=-1)[:, :, None, :])

    def run(x, pos, init, conv_cache):
        bsz, seq, _ = x.shape
        chunk = min(MIXER_CHUNK, MIXER_BLOCK, seq)
        assert seq % chunk == 0 and chunk & (chunk - 1) == 0 and chunk >= 2 * 8
        rope = _rope_tables(pos)
        tabs = _retention_tables(chunk)
        x2d = x.reshape(bsz * seq, D_MODEL)
        states = []
        y2d = None
        for i in range(depth):
            z, cv = _inproj(x2d.reshape(bsz, seq, D_MODEL), norm_mix3, w_in_p, rope, ml_conv_w, conv_b3,
                            conv_cache, i)
            mixed, *st = _mixer(z, tabs, wts, init, i)
            outs = _ffn(x2d, mixed.reshape(bsz * seq, D_MODEL), wo, norm_ffn3, wg, wu, wd, i,
                        norm_final2 if i == depth - 1 else None)
            x2d = outs[0]
            if i == depth - 1:
                y2d = outs[1]
            states.append(st + [cv])
        g, r, c, n, m, cv = (jnp.stack([st[j] for st in states]) for j in range(6))
        return (y2d.reshape(bsz, seq, D_MODEL), g, r, c,
                n[..., 0], m[:, :, 0, ::LANES], cv)

    pos_p = jnp.arange(x_prompt.shape[1], dtype=jnp.int32)
    pos_s = PAST_LEN + jnp.arange(x_sample.shape[1], dtype=jnp.int32)
    yp, gp, rp, cp, np_, mp, cvp = run(x_prompt, pos_p, None, None)
    ys, gs, rs, cs, ns, ms, cvs = run(x_sample, pos_s, init_sample, cache_mlstm_conv)
    return (yp, ys, gp, gs, rp, rs, cp, cs, np_, ns, mp, ms, cvp, cvs)
```

```python
import functools

import numpy as np
import jax
import jax.numpy as jnp
from jax import lax
from jax.experimental import pallas as pl
from jax.experimental.pallas import tpu as pltpu

F32 = jnp.float32
BF16 = jnp.bfloat16

D_MODEL = 1024
EPS = 1e-6
GLA_HEADS, GLA_DK, GLA_DV, GLA_RANK, GLA_TAU = 4, 32, 64, 16, 16.0
RET_HEADS, RET_D = 6, 64
ML_HEADS, ML_D = 6, 64
CONV_W = 4
ROPE_BASE = 10000.0
PAST_LEN = 2048
GLA_KW = GLA_HEADS * GLA_DK
GLA_W = GLA_HEADS * GLA_DV
RET_W = RET_HEADS * RET_D
ML_W = ML_HEADS * ML_D
D_FF = 2816
SPLIT_SIZES = (GLA_KW, GLA_KW, GLA_W, GLA_RANK, GLA_W, RET_W, RET_W, RET_W, RET_W, ML_W, ML_W, ML_W,
               ML_HEADS, ML_HEADS)

LANES = 128
SUBLANES = 8
PAIRS = RET_W // LANES
Z_GQ = 0
Z_GK = Z_GQ + GLA_KW
Z_GV = Z_GK + GLA_KW
Z_GG = Z_GV + GLA_W
Z_RQ = Z_GG + GLA_W
Z_RK = Z_RQ + RET_W
Z_RV = Z_RK + RET_W
Z_RG = Z_RV + RET_W
Z_MU = Z_RG + RET_W
Z_MV = Z_MU + ML_W
Z_MO = Z_MV + ML_W
Z_SM = Z_MO + ML_W
Z_W = Z_SM + LANES
SM_IG, SM_FG = GLA_RANK, GLA_RANK + ML_HEADS
SHIFT_DK, SHIFT_D = GLA_DK.bit_length() - 1, RET_D.bit_length() - 1
MIX_RET = GLA_W

TOKEN_BLOCK = 512
MIXER_BLOCK = 1024
MIXER_CHUNK = 128
MIXER_GROUP = 256
FF_CHUNK = 1408
LOG2E = 1.4426950408889634
GLA_FACTOR_LIMIT = 1e18
VMEM_LIMIT = 56 * 1024 * 1024

_NT = (((1,), (1,)), ((), ()))


def _dot(a, b):
    return jnp.dot(a.astype(BF16), b.astype(BF16), preferred_element_type=F32)


def _split2(x):
    hi = x.astype(BF16)
    return [hi, (x - hi.astype(F32)).astype(BF16)]


def _log_sigmoid(x):
    return jnp.minimum(x, 0.0) - jnp.log(1.0 + jnp.exp(-jnp.abs(x)))


def _sigmoid(x):
    return 0.5 + 0.5 * jnp.tanh(0.5 * x)


def _silu(x):
    h = 0.5 * x
    return h + h * jnp.tanh(h)


def _iota(shape, dim):
    return lax.broadcasted_iota(jnp.int32, shape, dim)


def _inproj_kernel(*refs, tm, has_init):
    if has_init:
        x_ref, g_ref, w_ref, cos_ref, sin_ref, cw_ref, cb_ref, cv0_ref, z_ref, cvout_ref, xp_ref = refs
    else:
        x_ref, g_ref, w_ref, cos_ref, sin_ref, cw_ref, cb_ref, z_ref, cvout_ref, xp_ref = refs
    conv_lo = SUBLANES - (CONV_W - 1)

    @pl.when(pl.program_id(1) == 0)
    def _start_of_sequence():
        if has_init:
            xp_ref[conv_lo:SUBLANES, :] = cv0_ref[...]
        else:
            xp_ref[conv_lo:SUBLANES, :] = jnp.zeros((CONV_W - 1, ML_W), F32)

    x = x_ref[...]
    r = lax.rsqrt(jnp.mean(x * x, axis=-1, keepdims=True) + EPS)
    h = (x * r * g_ref[...]).astype(BF16)

    def proj(lo, hi):
        return jnp.dot(h, w_ref[:, lo:hi], preferred_element_type=F32)

    z_ref[:, Z_GQ:Z_RQ] = proj(Z_GQ, Z_RQ)
    seg = proj(Z_RQ, Z_RV)
    lo32 = (_iota((tm, LANES), 1) & (RET_D // 2)) == 0
    for p in range(2 * PAIRS):
        lanes = slice((p % PAIRS) * LANES, (p % PAIRS + 1) * LANES)
        t = seg[:, p * LANES:(p + 1) * LANES]
        sw = jnp.where(lo32, pltpu.roll(t, LANES - 32, 1), pltpu.roll(t, 32, 1))
        z_ref[:, Z_RQ + p * LANES:Z_RQ + (p + 1) * LANES] = t * cos_ref[:, lanes] + sw * sin_ref[:, lanes]
    z_ref[:, Z_RV:Z_MU] = proj(Z_RV, Z_MU)
    seg = proj(Z_MU, Z_MO)
    z_ref[:, Z_MV:Z_MO] = seg[:, ML_W:]
    xp_ref[SUBLANES:SUBLANES + tm, :] = seg[:, :ML_W]
    cw = cw_ref[...]
    conv = cb_ref[...]
    for j in range(CONV_W):
        conv = conv + xp_ref[conv_lo + j:conv_lo + j + tm, :] * cw[j:j + 1, :]
    z_ref[:, Z_MU:Z_MV] = _silu(conv)
    conv_tail = xp_ref[tm + conv_lo:tm + SUBLANES, :]
    xp_ref[conv_lo:SUBLANES, :] = conv_tail
    cvout_ref[...] = conv_tail
    z_ref[:, Z_MO:Z_W] = proj(Z_MO, Z_W)


def _inproj(x3, norm3, w_in_p, rope, cw, cb, cv0, layer):
    bsz, seq, _ = x3.shape
    tm = min(TOKEN_BLOCK, seq)
    has_init = cv0 is not None
    cos, sin = rope
    in_specs = [
        pl.BlockSpec((None, tm, D_MODEL), lambda b, s: (b, s, 0)),
        pl.BlockSpec((None, 1, D_MODEL), lambda b, s: (layer, 0, 0)),
        pl.BlockSpec((None, D_MODEL, Z_W), lambda b, s: (layer, 0, 0), pipeline_mode=pl.Buffered(1)),
        pl.BlockSpec((tm, RET_W), lambda b, s: (s, 0)),
        pl.BlockSpec((tm, RET_W), lambda b, s: (s, 0)),
        pl.BlockSpec((None, CONV_W, ML_W), lambda b, s: (layer, 0, 0)),
        pl.BlockSpec((None, 1, ML_W), lambda b, s: (layer, 0, 0)),
    ]
    args = [x3, norm3, w_in_p, cos, sin, cw, cb]
    if has_init:
        in_specs.append(pl.BlockSpec((None, None, CONV_W - 1, ML_W), lambda b, s: (layer, b, 0, 0)))
        args.append(cv0)
    return pl.pallas_call(
        functools.partial(_inproj_kernel, tm=tm, has_init=has_init),
        grid=(bsz, seq // tm),
        in_specs=in_specs,
        out_specs=[pl.BlockSpec((None, tm, Z_W), lambda b, s: (b, s, 0)),
                   pl.BlockSpec((None, CONV_W - 1, ML_W), lambda b, s: (b, 0, 0))],
        out_shape=[jax.ShapeDtypeStruct((bsz, seq, Z_W), F32),
                   jax.ShapeDtypeStruct((bsz, CONV_W - 1, ML_W), F32)],
        scratch_shapes=[pltpu.VMEM((tm + SUBLANES, ML_W), F32)],
        compiler_params=pltpu.CompilerParams(dimension_semantics=("parallel", "arbitrary"),
                                             vmem_limit_bytes=VMEM_LIMIT),
        name="inproj",
    )(*args)


def _mixer_kernel(*refs, block, chunk, has_init):
    it = iter(refs)
    z_ref, rdec_ref, rint_ref, rwend_ref, rgam_ref, tril_ref, ind_ref, on256_ref = (next(it) for _ in range(8))
    wa2_ref, ba_ref, bif_ref, gnorm_ref, rnorm_ref, mnorm_ref, wqk_ref = (next(it) for _ in range(7))
    if has_init:
        g0_ref, r0_ref, c0_ref, n0_ref, m0_ref = (next(it) for _ in range(5))
    mix_ref, gout_ref, rout_ref, cout_ref, nout_ref, mout_ref = (next(it) for _ in range(6))
    gst_ref, sst_ref, cst_ref, mst_ref, la_ref, gprev_ref = (next(it) for _ in range(6))

    step = pl.program_id(1)
    last = pl.num_programs(1) - 1
    c = chunk
    hd = RET_D
    group = min(MIXER_GROUP, block)

    @pl.when(step == 0)
    def _init():
        gst_ref[...] = jnp.zeros_like(gst_ref)
        sst_ref[...] = jnp.zeros_like(sst_ref)
        cst_ref[...] = jnp.zeros_like(cst_ref)
        if has_init:
            for h in range(GLA_HEADS):
                gst_ref[h * GLA_DK:(h + 1) * GLA_DK, h * GLA_DV:(h + 1) * GLA_DV] = g0_ref[h]
            for h in range(RET_HEADS):
                p, a = divmod(h, 2)
                sst_ref[p, a * hd:(a + 1) * hd, a * hd:(a + 1) * hd] = r0_ref[h]
                cst_ref[p, a * hd:(a + 1) * hd, a * hd:(a + 1) * hd] = c0_ref[h]
                cst_ref[p, a * hd:(a + 1) * hd, LANES + a * hd:LANES + (a + 1) * hd] = n0_ref[h]
            mst_ref[...] = m0_ref[...] * LOG2E
        else:
            mst_ref[...] = jnp.zeros_like(mst_ref)

    lane_c = _iota((c, LANES), 1)
    lane_lo = (lane_c & hd) == 0
    pair_mask = [jnp.where(lane_lo, 1.0, 0.0).astype(BF16), jnp.where(lane_lo, 0.0, 1.0).astype(BF16)]
    gla_mask = [jnp.where((lane_c >> SHIFT_DK) == h, 1.0, 0.0).astype(BF16) for h in range(GLA_HEADS)]
    gv_mask = [jnp.where((_iota((c, GLA_W), 1) >> SHIFT_D) == h, 1.0, 0.0).astype(BF16) for h in range(GLA_HEADS)]
    ones_b = jnp.ones((c, LANES), BF16)
    neg_inf = jnp.float32(-jnp.inf)
    causal = _iota((c, c), 1) <= _iota((c, c), 0)
    causal4 = _iota((GLA_HEADS * c, c), 1) <= (_iota((GLA_HEADS * c, c), 0) & (c - 1))
    bm_gk = jnp.where((_iota((GLA_KW, GLA_W), 0) >> SHIFT_DK) == (_iota((GLA_KW, GLA_W), 1) >> SHIFT_D), 1.0, 0.0)
    bm_pair = jnp.where((_iota((LANES, LANES), 0) >> SHIFT_D) == (_iota((LANES, LANES), 1) >> SHIFT_D), 1.0, 0.0)
    bm_aug = jnp.concatenate([bm_pair, bm_pair], axis=1)
    sub_lo = _iota((LANES, c), 0) < hd

    def stack_pair(xb):
        return jnp.concatenate([xb * pair_mask[0], xb * pair_mask[1]], axis=0)

    gprev_ref[...] = gst_ref[...]
    on256 = on256_ref[...]

    def prep(g):
        rs = slice(g * group, (g + 1) * group)
        qk = [_dot(z_ref[rs, Z_MU + p * LANES:Z_MU + (p + 1) * LANES], wqk_ref[p]) for p in range(PAIRS)]
        out = {'qm': jnp.concatenate([t[:, :LANES] for t in qk], axis=1),
               'km': jnp.concatenate([t[:, LANES:] for t in qk], axis=1)}
        small = z_ref[rs, Z_SM:Z_SM + LANES]
        la = _log_sigmoid(_dot(small, wa2_ref[...]) + ba_ref[...]) * (LOG2E / GLA_TAU)
        gates = small + bif_ref[...]
        lf = _log_sigmoid(gates) * LOG2E
        yield
        cum = jnp.dot(tril_ref[...], jnp.concatenate(_split2(la) + _split2(lf), axis=1),
                      preferred_element_type=F32)
        out['bg'] = cum[:, :LANES] + cum[:, LANES:2 * LANES]
        la_ref[rs, :] = la
        bt = cum[:, 2 * LANES:3 * LANES] + cum[:, 3 * LANES:]
        yield
        bxe = jnp.dot(jnp.concatenate(_split2(bt), axis=1), ind_ref[...], preferred_element_type=F32)
        out['bx'] = bxe[:, :ML_HEADS * LANES]
        out['dg'] = gates * LOG2E - bxe[:, ML_HEADS * LANES:]
        preps[g] = out

    def gla_norm(o, rs):
        return (o * lax.rsqrt(_dot(o * o, on256) + EPS) * gnorm_ref[...]
                * _silu(z_ref[rs, Z_GG:Z_GG + GLA_W]))

    def finish(g):
        rs = slice(g * group, (g + 1) * group)
        raw = [jnp.concatenate(t, axis=0) for t in zip(*raws.pop(g))]
        n_col = (RET_W + ML_W) // GLA_W
        os_ = [jnp.concatenate(raw[2 * k:2 * k + 2], axis=1) for k in range(n_col)]
        ocs = [o - _dot(o, on256) for o in os_]
        mix_ref[rs, 0:GLA_W] = gla_norm(raw[2 * n_col], rs)
        yield
        ons = [oc * lax.rsqrt(_dot(oc * oc, on256) + EPS) for oc in ocs]
        yield
        for k in range(n_col):
            on = ons[k]
            for j in range(GLA_W // LANES):
                t = (GLA_W // LANES) * k + j
                tile = on[:, j * LANES:(j + 1) * LANES]
                if t < PAIRS:
                    tile = (tile * rnorm_ref[:, t * LANES:(t + 1) * LANES]
                            * _silu(z_ref[rs, Z_RG + t * LANES:Z_RG + (t + 1) * LANES]))
                else:
                    tile = tile * mnorm_ref[:, (t - PAIRS) * LANES:(t - PAIRS + 1) * LANES]
                mix_ref[rs, MIX_RET + t * LANES:MIX_RET + (t + 1) * LANES] = tile

    per_group = group // c
    n_chunks = block // c

    def first_dots(ci, pre, lr):
        rows = slice(ci * c, (ci + 1) * c)
        d = {}
        b = pre['bg'][lr]
        bmid = b[c // 2 - 1:c // 2, :]
        bl = b[c - 1:c, :]
        gq = z_ref[rows, Z_GQ:Z_GQ + GLA_KW] * (GLA_DK ** -0.5)
        gk = z_ref[rows, Z_GK:Z_GK + GLA_KW]
        eq = jnp.exp2(b - bmid)
        ek = jnp.exp2(bmid - b)
        qh = gq * eq
        kh = gk * ek
        fac = jnp.maximum(jnp.maximum(jnp.abs(qh), jnp.abs(kh)), jnp.maximum(eq, ek))
        d['fac'] = functools.reduce(jnp.maximum, [fac[i:i + SUBLANES] for i in range(0, c, SUBLANES)])
        qh = qh.astype(BF16)
        d['qe'] = (gq * jnp.exp2(b)).astype(BF16)
        kl_t = (gk * jnp.exp2(bl - b)).T
        lhs = jnp.concatenate([qh * gla_mask[h] for h in range(GLA_HEADS)], axis=0)
        a = lax.dot_general(lhs, kh.astype(BF16), _NT, preferred_element_type=F32)
        d['a'] = jnp.where(causal4, a, 0.0).astype(BF16)
        d['gvb'] = z_ref[rows, Z_GV:Z_GV + GLA_W].astype(BF16)
        d['gupd'] = _dot(kl_t, d['gvb'])
        d['bl_col'] = b[c - SUBLANES:c, :].T[:, SUBLANES - 1:SUBLANES]
        d['rvb'] = z_ref[rows, Z_RV:Z_RV + RET_W].astype(BF16)
        kr = z_ref[rows, Z_RK:Z_RK + RET_W]
        d['qr'] = z_ref[rows, Z_RQ:Z_RQ + RET_W]
        d['rpm'] = [(lax.dot_general(stack_pair(d['qr'][:, p * LANES:(p + 1) * LANES].astype(BF16)),
                                     kr[:, p * LANES:(p + 1) * LANES].astype(BF16), _NT,
                                     preferred_element_type=F32) * rdec_ref[p]).astype(BF16)
                    for p in range(PAIRS)]
        kr_t = (kr.T * rwend_ref[...]).astype(BF16)
        d['rupd'] = [_dot(kr_t[p * LANES:(p + 1) * LANES], d['rvb'][:, p * LANES:(p + 1) * LANES])
                     for p in range(PAIRS)]
        d['mvb'] = z_ref[rows, Z_MV:Z_MV + ML_W].astype(BF16)
        km = pre['km'][lr]
        d['km_t'] = km.T
        d['ms2'] = [lax.dot_general(stack_pair(pre['qm'][lr, p * LANES:(p + 1) * LANES].astype(BF16)),
                                    km[:, p * LANES:(p + 1) * LANES].astype(BF16), _NT,
                                    preferred_element_type=F32) for p in range(PAIRS)]
        dg_t = pre['dg'][lr].T
        d['dg_t'] = dg_t
        d['mrow'] = [jnp.max(jnp.where(causal, pre['bx'][lr, h * LANES:h * LANES + c]
                                       + dg_t[SM_IG + h:SM_IG + h + 1, :], neg_inf), axis=1, keepdims=True)
                     for h in range(ML_HEADS)]
        return d

    def second_dots(ci, pre, lr, d):
        rows = slice(ci * c, (ci + 1) * c)
        raw = []
        a = d['a']
        o = jnp.dot(jnp.concatenate([a[h * c:(h + 1) * c] for h in range(GLA_HEADS)] + [d['qe']], axis=1),
                    jnp.concatenate([d['gvb'] * gv_mask[h] for h in range(GLA_HEADS)]
                                    + [gst_ref[...].astype(BF16)], axis=0),
                    preferred_element_type=F32)
        gst_ref[...] = gst_ref[...] * jnp.exp2(d['bl_col']) + bm_gk * d['gupd']
        for p in range(PAIRS):
            lanes = slice(p * LANES, (p + 1) * LANES)
            vb = d['rvb'][:, lanes]
            pm = d['rpm'][p]
            raw.append(jnp.dot(
                jnp.concatenate([pm[:c], pm[c:], (d['qr'][:, lanes] * rint_ref[:, lanes]).astype(BF16)], axis=1),
                jnp.concatenate([vb * pair_mask[0], vb * pair_mask[1], sst_ref[p].astype(BF16)], axis=0),
                preferred_element_type=F32))
            sst_ref[p] = sst_ref[p] * rgam_ref[:, lanes] + bm_pair * d['rupd'][p]
        dg_t = d['dg_t']
        wls, w0ls = [], []
        for p in range(PAIRS):
            lanes = slice(p * LANES, (p + 1) * LANES)
            wd, w0, mt = [], [], []
            for a_ in range(2):
                h = 2 * p + a_
                colr = pre['bx'][lr, h * LANES:(h + 1) * LANES]
                logd = jnp.where(causal, colr[:, :c] + dg_t[SM_IG + h:SM_IG + h + 1, :], neg_inf)
                log0 = colr + mst_ref[:, h * LANES:(h + 1) * LANES]
                m_t = jnp.maximum(log0, d['mrow'][h])
                wd.append(jnp.exp2(logd - m_t[:, :c]))
                w0.append(jnp.exp2(log0 - m_t))
                mt.append(m_t)
                mst_ref[:, h * LANES:(h + 1) * LANES] = m_t[c - 1:c, :]
            qp = pre['qm'][lr, lanes]
            vb = d['mvb'][:, lanes]
            a0 = (d['ms2'][p][:c] * wd[0]).astype(BF16)
            a1 = (d['ms2'][p][c:] * wd[1]).astype(BF16)
            w0p = jnp.where(lane_lo, w0[0], w0[1])
            mtp = jnp.where(lane_lo, mt[0], mt[1])
            nd = jnp.dot(
                jnp.concatenate([a0, a1, (qp * w0p).astype(BF16)], axis=1),
                jnp.concatenate([jnp.concatenate([vb * pair_mask[0], pair_mask[0]], axis=1),
                                 jnp.concatenate([vb * pair_mask[1], pair_mask[1]], axis=1),
                                 cst_ref[p].astype(BF16)], axis=0),
                preferred_element_type=F32)
            hh = nd[:, :LANES] / jnp.maximum(jnp.abs(nd[:, LANES:]), jnp.exp2(-mtp))
            raw.append(hh * _sigmoid(z_ref[rows, Z_MO + p * LANES:Z_MO + (p + 1) * LANES]))
            wls.append(jnp.where(sub_lo, wd[0][c - 1:c, :], wd[1][c - 1:c, :]))
            w0l = w0p[c - 1:c, :]
            w0ls.append(jnp.concatenate([w0l, w0l], axis=1))
        for p in range(PAIRS):
            lanes = slice(p * LANES, (p + 1) * LANES)
            upd = _dot(d['km_t'][lanes] * wls[p], jnp.concatenate([d['mvb'][:, lanes], ones_b], axis=1))
            cst_ref[p] = cst_ref[p] * w0ls[p] + bm_aug * upd
        return raw + [o]

    preps, raws, firsts, pending = {}, {}, {}, []

    def drain(gen):
        for _ in gen:
            pass

    def tick():
        pending[:] = [gen for gen in pending if next(gen, pending) is not pending]

    fac_max = jnp.zeros((SUBLANES, LANES), F32)
    place = lambda ci: (ci // per_group, slice((ci % per_group) * c, (ci % per_group + 1) * c))
    drain(prep(0))
    firsts[0] = first_dots(0, preps[0], place(0)[1])
    for ci in range(n_chunks):
        g, lr = place(ci)
        if ci % per_group == 0:
            if g > 0:
                pending.append(finish(g - 1))
            if (g + 1) * group < block:
                pending.append(prep(g + 1))
        tick()
        if ci + 1 < n_chunks:
            g1, lr1 = place(ci + 1)
            while g1 not in preps:
                tick()
            firsts[ci + 1] = first_dots(ci + 1, preps[g1], lr1)
        tick()
        d = firsts.pop(ci)
        fac_max = jnp.maximum(fac_max, d['fac'])
        raws.setdefault(g, []).append(second_dots(ci, preps[g], lr, d))
        tick()
    while pending:
        tick()
    drain(finish(block // group - 1))


    @pl.when(jnp.logical_not(jnp.max(fac_max) < GLA_FACTOR_LIMIT))
    def _gla_per_token():
        gst_ref[...] = gprev_ref[...]

        def tokens8(g, carry):
            rows8 = pl.ds(pl.multiple_of(g * SUBLANES, SUBLANES), SUBLANES)
            alpha_t = jnp.exp2(la_ref[rows8, :]).T
            k_t = z_ref[rows8, Z_GK:Z_GK + GLA_KW].T
            q_t = (z_ref[rows8, Z_GQ:Z_GQ + GLA_KW] * (GLA_DK ** -0.5)).T
            v8 = z_ref[rows8, Z_GV:Z_GV + GLA_W]
            s = gst_ref[...]
            outs = []
            for j in range(SUBLANES):
                s = s * alpha_t[:, j:j + 1] + bm_gk * (k_t[:, j:j + 1] * v8[j:j + 1, :])
                outs.append(jnp.sum(q_t[:, j:j + 1] * s, axis=0, keepdims=True))
            gst_ref[...] = s
            mix_ref[rows8, 0:GLA_W] = jnp.concatenate(outs, axis=0)
            return carry

        lax.fori_loop(0, block // SUBLANES, tokens8, 0)
        mix_ref[:, 0:GLA_W] = gla_norm(mix_ref[:, 0:GLA_W], slice(0, block))

    @pl.when(step == last)
    def _finish():
        for h in range(GLA_HEADS):
            gout_ref[h] = gst_ref[h * GLA_DK:(h + 1) * GLA_DK, h * GLA_DV:(h + 1) * GLA_DV]
        for h in range(RET_HEADS):
            p, a = divmod(h, 2)
            rout_ref[h] = sst_ref[p, a * hd:(a + 1) * hd, a * hd:(a + 1) * hd]
            cout_ref[h] = cst_ref[p, a * hd:(a + 1) * hd, a * hd:(a + 1) * hd]
            nout_ref[h] = cst_ref[p, a * hd:(a + 1) * hd, LANES + a * hd:LANES + (a + 1) * hd]
        mout_ref[...] = mst_ref[...] * (1.0 / LOG2E)


def _mixer_constants(block, chunk):
    t = np.arange(block)
    tril = ((t[:, None] // chunk == t[None, :] // chunk) & (t[None, :] <= t[:, None])).astype(np.float32)
    r = np.arange(2 * LANES)[:, None] % LANES
    col = np.arange(ML_HEADS * LANES)[None, :] // LANES
    lane = np.arange(LANES)[None, :]
    ind = np.concatenate([r == col + SM_FG, (r == lane + ML_HEADS) & (lane >= SM_IG) & (lane < SM_FG)],
                         axis=1).astype(np.float32)
    i256 = np.arange(GLA_W) // GLA_DV
    on256 = (i256[:, None] == i256[None, :]).astype(np.float32) / GLA_DV
    return tuple(jnp.asarray(m, BF16) for m in (tril, ind, on256))


def _mixer(z3, tabs, wts, init, layer):
    bsz, seq, _ = z3.shape
    block = min(MIXER_BLOCK, seq)
    chunk = min(MIXER_CHUNK, block)
    has_init = init is not None
    rdec, rint, rwend, rgam = tabs
    consts = _mixer_constants(min(MIXER_GROUP, block), chunk)

    def const(shape):
        nd = len(shape)
        return pl.BlockSpec(shape, lambda b, s: (0,) * nd)

    def per_layer(shape):
        nd = len(shape)
        return pl.BlockSpec((None,) + shape, lambda b, s: (layer,) + (0,) * nd)

    def per_batch(shape, with_layer):
        nd = len(shape)
        if with_layer:
            return pl.BlockSpec((None, None) + shape, lambda b, s: (layer, b) + (0,) * nd)
        return pl.BlockSpec((None,) + shape, lambda b, s: (b,) + (0,) * nd)

    state_shapes = [(GLA_HEADS, GLA_DK, GLA_DV), (RET_HEADS, RET_D, RET_D), (ML_HEADS, ML_D, ML_D),
                    (ML_HEADS, ML_D, ML_D), (1, ML_HEADS * LANES)]
    in_specs = [
        pl.BlockSpec((None, block, Z_W), lambda b, s: (b, s, 0)),
        const(rdec.shape), const(rint.shape), const(rwend.shape), const(rgam.shape),
    ] + [const(m.shape) for m in consts] + [
        per_layer((LANES, GLA_KW)), per_layer((1, GLA_KW)), per_layer((1, LANES)),
        per_layer((1, GLA_W)), per_layer((1, RET_W)), per_layer((1, ML_W)),
        per_layer((PAIRS, LANES, 2 * LANES)),
    ]
    args = [z3, rdec, rint, rwend, rgam] + list(consts) + list(wts)
    if has_init:
        in_specs += [per_batch(s, True) for s in state_shapes]
        args += list(init)
    out_specs = [pl.BlockSpec((None, block, D_MODEL), lambda b, s: (b, s, 0))]
    out_specs += [per_batch(s, False) for s in state_shapes]
    out_shape = [jax.ShapeDtypeStruct((bsz, seq, D_MODEL), F32)]
    out_shape += [jax.ShapeDtypeStruct((bsz,) + s, F32) for s in state_shapes]
    scratch = [
        pltpu.VMEM((GLA_KW, GLA_W), F32), pltpu.VMEM((PAIRS, LANES, LANES), F32),
        pltpu.VMEM((PAIRS, LANES, 2 * LANES), F32), pltpu.VMEM((1, ML_HEADS * LANES), F32),
        pltpu.VMEM((block, LANES), F32),
        pltpu.VMEM((GLA_KW, GLA_W), F32),
    ]
    return pl.pallas_call(
        functools.partial(_mixer_kernel, block=block, chunk=chunk, has_init=has_init),
        grid=(bsz, seq // block),
        in_specs=in_specs,
        out_specs=out_specs,
        out_shape=out_shape,
        scratch_shapes=scratch,
        compiler_params=pltpu.CompilerParams(dimension_semantics=("parallel", "arbitrary"),
                                             vmem_limit_bytes=VMEM_LIMIT),
        name="mixer",
    )(*args)


def _ffn_kernel(*refs, final):
    if final:
        x_ref, mix_ref, wo_ref, g_ref, wg_ref, wu_ref, wd_ref, gf_ref, out_ref, y_ref = refs
    else:
        x_ref, mix_ref, wo_ref, g_ref, wg_ref, wu_ref, wd_ref, out_ref = refs
    x1 = x_ref[...] + jnp.dot(mix_ref[...].astype(BF16), wo_ref[...], preferred_element_type=F32)
    r = lax.rsqrt(jnp.mean(x1 * x1, axis=-1, keepdims=True) + EPS)
    hf = (x1 * r * g_ref[...]).astype(BF16)
    acc = x1
    for c in range(D_FF // FF_CHUNK):
        cols = slice(c * FF_CHUNK, (c + 1) * FF_CHUNK)
        gate = jnp.dot(hf, wg_ref[:, cols], preferred_element_type=F32)
        up = jnp.dot(hf, wu_ref[:, cols], preferred_element_type=F32)
        act = (_silu(gate) * up).astype(BF16)
        acc = acc + jnp.dot(act, wd_ref[cols, :], preferred_element_type=F32)
    out_ref[...] = acc
    if final:
        rf = lax.rsqrt(jnp.mean(acc * acc, axis=-1, keepdims=True) + EPS)
        y_ref[...] = acc * rf * gf_ref[...]


def _ffn(x2d, mix2d, wo, norm3, wg, wu, wd, layer, norm_final2):
    n = x2d.shape[0]
    tm = min(TOKEN_BLOCK, n)
    final = norm_final2 is not None

    def weight(shape):
        return pl.BlockSpec((None,) + shape, lambda i: (layer, 0, 0), pipeline_mode=pl.Buffered(1))

    row_spec = pl.BlockSpec((tm, D_MODEL), lambda i: (i, 0))
    in_specs = [row_spec, row_spec, weight((D_MODEL, D_MODEL)),
                pl.BlockSpec((None, 1, D_MODEL), lambda i: (layer, 0, 0)),
                weight((D_MODEL, D_FF)), weight((D_MODEL, D_FF)), weight((D_FF, D_MODEL))]
    args = [x2d, mix2d, wo, norm3, wg, wu, wd]
    out_specs = [row_spec]
    out_shape = [jax.ShapeDtypeStruct((n, D_MODEL), F32)]
    if final:
        in_specs.append(pl.BlockSpec((1, D_MODEL), lambda i: (0, 0)))
        args.append(norm_final2)
        out_specs.append(row_spec)
        out_shape.append(jax.ShapeDtypeStruct((n, D_MODEL), F32))
    return pl.pallas_call(
        functools.partial(_ffn_kernel, final=final),
        grid=(n // tm,),
        in_specs=in_specs,
        out_specs=out_specs,
        out_shape=out_shape,
        compiler_params=pltpu.CompilerParams(dimension_semantics=("parallel",),
                                             vmem_limit_bytes=VMEM_LIMIT),
        name="ffn",
    )(*args)


def _reorder_w_in(w_in):
    offs = [0]
    for sz in SPLIT_SIZES:
        offs.append(offs[-1] + sz)
    ga0, ga1, gates0 = offs[3], offs[4], offs[12]
    pad = jnp.zeros(w_in.shape[:-1] + (LANES - GLA_RANK - 2 * ML_HEADS,), w_in.dtype)
    w = jnp.concatenate([w_in[..., :ga0], w_in[..., ga1:gates0], w_in[..., ga0:ga1], w_in[..., gates0:], pad],
                        axis=-1)
    scale = np.ones((Z_W,), np.float32)
    scale[Z_RK:Z_RV] = RET_D ** -0.5
    return (w * scale).astype(BF16)


def _block_diag_pairs(w):
    depth, h, d, e = w.shape
    eye = jnp.eye(2, dtype=w.dtype)
    return jnp.einsum('lpade,ab->lpadbe', w.reshape(depth, h // 2, 2, d, e), eye).reshape(
        depth, h // 2, 2 * d, 2 * e)


def _rope_tables(pos):
    inv = ROPE_BASE ** (-jnp.arange(0, RET_D, 2, dtype=F32) / RET_D)
    ang = pos.astype(F32)[:, None] * inv[None, :]
    cos, sin = jnp.cos(ang), jnp.sin(ang)
    cos_h = jnp.concatenate([cos, cos], axis=-1)
    sin_h = jnp.concatenate([-sin, sin], axis=-1)
    return jnp.tile(cos_h, (1, RET_HEADS)), jnp.tile(sin_h, (1, RET_HEADS))


def _retention_tables(chunk):
    lg = jnp.log1p(-jnp.exp2(-5.0 - jnp.arange(RET_HEADS, dtype=F32)))
    t = jnp.arange(chunk, dtype=F32)
    diff = t[:, None] - t[None, :]
    decay = jnp.where((diff >= 0)[..., None], jnp.exp(jnp.maximum(diff, 0.0)[..., None] * lg), 0.0)
    decay = jnp.transpose(decay, (2, 0, 1)).reshape(PAIRS, 2 * chunk, chunk)
    inter = jnp.repeat(jnp.exp((t[:, None] + 1.0) * lg), RET_D, axis=1)
    w_end = jnp.exp((chunk - 1.0 - t)[None, :] * lg[:, None])
    w_end = jnp.repeat(w_end, RET_D, axis=0)
    gam = jnp.repeat(jnp.exp(chunk * lg), RET_D)[None, :]
    return decay, inter, w_end, gam


def kernel(x_prompt, x_sample, state_gla, state_ret, state_mlstm_c, state_mlstm_n, state_mlstm_m,
           cache_mlstm_conv, norm_mix, w_in, gla_w_a2, gla_b_a, gla_norm, ret_norm, ml_conv_w,
           ml_conv_b, ml_wq, ml_wk, ml_b_if, ml_norm, w_out, norm_ffn, w_gate, w_up, w_down, norm_final):
    depth = w_in.shape[0]

    w_in_p = _reorder_w_in(w_in)
    wo, wg, wu, wd = (w.astype(BF16) for w in (w_out, w_gate, w_up, w_down))
    wa2 = jnp.zeros((depth, LANES, GLA_KW), F32).at[:, :GLA_RANK, :].set(gla_w_a2).astype(BF16)
    bif = jnp.zeros((depth, 1, LANES), F32).at[:, 0, SM_IG:SM_IG + 2 * ML_HEADS].set(ml_b_if)
    wqk = jnp.concatenate([_block_diag_pairs(ml_wq), _block_diag_pairs(ml_wk) * (ML_D ** -0.5)],
                          axis=-1).astype(BF16)
    row = lambda a: a[:, None, :]
    wts = (wa2, row(gla_b_a), bif, row(gla_norm), row(ret_norm), row(ml_norm), wqk)
    conv_b3 = row(ml_conv_b)
    norm_mix3, norm_ffn3, norm_final2 = row(norm_mix), row(norm_ffn), norm_final[None, :]

    init_sample = (state_gla, state_ret, state_mlstm_c,
                   jnp.broadcast_to(state_mlstm_n[..., None], state_mlstm_n.shape + (ML_D,)),
                   jnp.repeat(state_mlstm_m, LANES, axis=-1)[:, :, None, :])

    def run(x, pos, init, conv_cache):
        bsz, seq, _ = x.shape
        chunk = min(MIXER_CHUNK, MIXER_BLOCK, seq)
        assert seq % chunk == 0 and chunk & (chunk - 1) == 0 and chunk >= 2 * 8
        rope = _rope_tables(pos)
        tabs = _retention_tables(chunk)
        x2d = x.reshape(bsz * seq, D_MODEL)
        states = []
        y2d = None
        for i in range(depth):
            z, cv = _inproj(x2d.reshape(bsz, seq, D_MODEL), norm_mix3, w_in_p, rope, ml_conv_w, conv_b3,
                            conv_cache, i)
            mixed, *st = _mixer(z, tabs, wts, init, i)
            outs = _ffn(x2d, mixed.reshape(bsz * seq, D_MODEL), wo, norm_ffn3, wg, wu, wd, i,
                        norm_final2 if i == depth - 1 else None)
            x2d = outs[0]
            if i == depth - 1:
                y2d = outs[1]
            states.append(st + [cv])
        g, r, c, n, m, cv = (jnp.stack([st[j] for st in states]) for j in range(6))
        return (y2d.reshape(bsz, seq, D_MODEL), g, r, c,
                n[..., 0], m[:, :, 0, ::LANES], cv)

    pos_p = jnp.arange(x_prompt.shape[1], dtype=jnp.int32)
    pos_s = PAST_LEN + jnp.arange(x_sample.shape[1], dtype=jnp.int32)
    yp, gp, rp, cp, np_, mp, cvp = run(x_prompt, pos_p, None, None)
    ys, gs, rs, cs, ns, ms, cvs = run(x_sample, pos_s, init_sample, cache_mlstm_conv)
    return (yp, ys, gp, gs, rp, rs, cp, cs, np_, ns, mp, ms, cvp, cvs)
```

```python
import functools

import numpy as np
import jax
import jax.numpy as jnp
from jax import lax
from jax.experimental import pallas as pl
from jax.experimental.pallas import tpu as pltpu

F32 = jnp.float32
BF16 = jnp.bfloat16

D_MODEL = 1024
EPS = 1e-6
GLA_HEADS, GLA_DK, GLA_DV, GLA_RANK, GLA_TAU = 4, 32, 64, 16, 16.0
RET_HEADS, RET_D = 6, 64
ML_HEADS, ML_D = 6, 64
CONV_W = 4
ROPE_BASE = 10000.0
PAST_LEN = 2048
GLA_KW = GLA_HEADS * GLA_DK
GLA_W = GLA_HEADS * GLA_DV
RET_W = RET_HEADS * RET_D
ML_W = ML_HEADS * ML_D
D_FF = 2816
SPLIT_SIZES = (GLA_KW, GLA_KW, GLA_W, GLA_RANK, GLA_W, RET_W, RET_W, RET_W, RET_W, ML_W, ML_W, ML_W,
               ML_HEADS, ML_HEADS)

LANES = 128
SUBLANES = 8
PAIRS = RET_W // LANES
Z_GQ = 0
Z_GK = Z_GQ + GLA_KW
Z_GV = Z_GK + GLA_KW
Z_GG = Z_GV + GLA_W
Z_RQ = Z_GG + GLA_W
Z_RK = Z_RQ + RET_W
Z_RV = Z_RK + RET_W
Z_RG = Z_RV + RET_W
Z_MU = Z_RG + RET_W
Z_MV = Z_MU + ML_W
Z_MO = Z_MV + ML_W
Z_SM = Z_MO + ML_W
Z_W = Z_SM + LANES
SM_IG, SM_FG = GLA_RANK, GLA_RANK + ML_HEADS
SHIFT_DK, SHIFT_D = GLA_DK.bit_length() - 1, RET_D.bit_length() - 1
MIX_RET = GLA_W

TOKEN_BLOCK = 512
MIXER_BLOCK = 1024
MIXER_CHUNK = 128
MIXER_GROUP = 256
FF_CHUNK = 1408
LOG2E = 1.4426950408889634
GLA_FACTOR_LIMIT = 1e18
VMEM_LIMIT = 56 * 1024 * 1024

_NT = (((1,), (1,)), ((), ()))


def _dot(a, b):
    return jnp.dot(a.astype(BF16), b.astype(BF16), preferred_element_type=F32)


def _split2(x):
    hi = x.astype(BF16)
    return [hi, (x - hi.astype(F32)).astype(BF16)]


def _log_sigmoid(x):
    return jnp.minimum(x, 0.0) - jnp.log(1.0 + jnp.exp(-jnp.abs(x)))


def _sigmoid(x):
    return 0.5 + 0.5 * jnp.tanh(0.5 * x)


def _silu(x):
    h = 0.5 * x
    return h + h * jnp.tanh(h)


def _iota(shape, dim):
    return lax.broadcasted_iota(jnp.int32, shape, dim)


def _inproj_kernel(*refs, tm, has_init):
    if has_init:
        x_ref, g_ref, w_ref, cos_ref, sin_ref, cw_ref, cb_ref, cv0_ref, z_ref, cvout_ref, xp_ref = refs
    else:
        x_ref, g_ref, w_ref, cos_ref, sin_ref, cw_ref, cb_ref, z_ref, cvout_ref, xp_ref = refs
    conv_lo = SUBLANES - (CONV_W - 1)

    @pl.when(pl.program_id(1) == 0)
    def _start_of_sequence():
        if has_init:
            xp_ref[conv_lo:SUBLANES, :] = cv0_ref[...]
        else:
            xp_ref[conv_lo:SUBLANES, :] = jnp.zeros((CONV_W - 1, ML_W), F32)

    x = x_ref[...]
    r = lax.rsqrt(jnp.mean(x * x, axis=-1, keepdims=True) + EPS)
    h = (x * r * g_ref[...]).astype(BF16)

    def proj(lo, hi):
        return jnp.dot(h, w_ref[:, lo:hi], preferred_element_type=F32)

    z_ref[:, Z_GQ:Z_RQ] = proj(Z_GQ, Z_RQ)
    seg = proj(Z_RQ, Z_RV)
    lo32 = (_iota((tm, LANES), 1) & (RET_D // 2)) == 0
    for p in range(2 * PAIRS):
        lanes = slice((p % PAIRS) * LANES, (p % PAIRS + 1) * LANES)
        t = seg[:, p * LANES:(p + 1) * LANES]
        sw = jnp.where(lo32, pltpu.roll(t, LANES - 32, 1), pltpu.roll(t, 32, 1))
        z_ref[:, Z_RQ + p * LANES:Z_RQ + (p + 1) * LANES] = t * cos_ref[:, lanes] + sw * sin_ref[:, lanes]
    z_ref[:, Z_RV:Z_MU] = proj(Z_RV, Z_MU)
    seg = proj(Z_MU, Z_MO)
    z_ref[:, Z_MV:Z_MO] = seg[:, ML_W:]
    xp_ref[SUBLANES:SUBLANES + tm, :] = seg[:, :ML_W]
    cw = cw_ref[...]
    conv = cb_ref[...]
    for j in range(CONV_W):
        conv = conv + xp_ref[conv_lo + j:conv_lo + j + tm, :] * cw[j:j + 1, :]
    z_ref[:, Z_MU:Z_MV] = _silu(conv)
    conv_tail = xp_ref[tm + conv_lo:tm + SUBLANES, :]
    xp_ref[conv_lo:SUBLANES, :] = conv_tail
    cvout_ref[...] = conv_tail
    z_ref[:, Z_MO:Z_W] = proj(Z_MO, Z_W)


def _inproj(x3, norm3, w_in_p, rope, cw, cb, cv0, layer):
    bsz, seq, _ = x3.shape
    tm = min(TOKEN_BLOCK, seq)
    has_init = cv0 is not None
    cos, sin = rope
    in_specs = [
        pl.BlockSpec((None, tm, D_MODEL), lambda b, s: (b, s, 0)),
        pl.BlockSpec((None, 1, D_MODEL), lambda b, s: (layer, 0, 0)),
        pl.BlockSpec((None, D_MODEL, Z_W), lambda b, s: (layer, 0, 0), pipeline_mode=pl.Buffered(1)),
        pl.BlockSpec((tm, RET_W), lambda b, s: (s, 0)),
        pl.BlockSpec((tm, RET_W), lambda b, s: (s, 0)),
        pl.BlockSpec((None, CONV_W, ML_W), lambda b, s: (layer, 0, 0)),
        pl.BlockSpec((None, 1, ML_W), lambda b, s: (layer, 0, 0)),
    ]
    args = [x3, norm3, w_in_p, cos, sin, cw, cb]
    if has_init:
        in_specs.append(pl.BlockSpec((None, None, CONV_W - 1, ML_W), lambda b, s: (layer, b, 0, 0)))
        args.append(cv0)
    return pl.pallas_call(
        functools.partial(_inproj_kernel, tm=tm, has_init=has_init),
        grid=(bsz, seq // tm),
        in_specs=in_specs,
        out_specs=[pl.BlockSpec((None, tm, Z_W), lambda b, s: (b, s, 0)),
                   pl.BlockSpec((None, CONV_W - 1, ML_W), lambda b, s: (b, 0, 0))],
        out_shape=[jax.ShapeDtypeStruct((bsz, seq, Z_W), F32),
                   jax.ShapeDtypeStruct((bsz, CONV_W - 1, ML_W), F32)],
        scratch_shapes=[pltpu.VMEM((tm + SUBLANES, ML_W), F32)],
        compiler_params=pltpu.CompilerParams(dimension_semantics=("parallel", "arbitrary"),
                                             vmem_limit_bytes=VMEM_LIMIT),
        name="inproj",
    )(*args)


def _mixer_kernel(*refs, block, chunk, has_init):
    it = iter(refs)
    z_ref, rdec_ref, rint_ref, rwend_ref, rgam_ref, tril_ref, ind_ref, on256_ref = (next(it) for _ in range(8))
    wa2_ref, ba_ref, bif_ref, gnorm_ref, rnorm_ref, mnorm_ref, wqk_ref = (next(it) for _ in range(7))
    if has_init:
        g0_ref, r0_ref, c0_ref, n0_ref, m0_ref = (next(it) for _ in range(5))
    mix_ref, gout_ref, rout_ref, cout_ref, nout_ref, mout_ref = (next(it) for _ in range(6))
    gst_ref, sst_ref, cst_ref, mst_ref, la_ref, gprev_ref = (next(it) for _ in range(6))

    step = pl.program_id(1)
    last = pl.num_programs(1) - 1
    c = chunk
    hd = RET_D
    group = min(MIXER_GROUP, block)

    @pl.when(step == 0)
    def _init():
        gst_ref[...] = jnp.zeros_like(gst_ref)
        sst_ref[...] = jnp.zeros_like(sst_ref)
        cst_ref[...] = jnp.zeros_like(cst_ref)
        if has_init:
            for h in range(GLA_HEADS):
                gst_ref[h * GLA_DK:(h + 1) * GLA_DK, h * GLA_DV:(h + 1) * GLA_DV] = g0_ref[h]
            for h in range(RET_HEADS):
                p, a = divmod(h, 2)
                sst_ref[p, a * hd:(a + 1) * hd, a * hd:(a + 1) * hd] = r0_ref[h]
                cst_ref[p, a * hd:(a + 1) * hd, a * hd:(a + 1) * hd] = c0_ref[h]
                cst_ref[p, a * hd:(a + 1) * hd, LANES + a * hd:LANES + (a + 1) * hd] = n0_ref[h]
            mst_ref[...] = m0_ref[...] * LOG2E
        else:
            mst_ref[...] = jnp.zeros_like(mst_ref)

    lane_c = _iota((c, LANES), 1)
    lane_lo = (lane_c & hd) == 0
    pair_mask = [jnp.where(lane_lo, 1.0, 0.0).astype(BF16), jnp.where(lane_lo, 0.0, 1.0).astype(BF16)]
    gla_mask = [jnp.where((lane_c >> SHIFT_DK) == h, 1.0, 0.0).astype(BF16) for h in range(GLA_HEADS)]
    gv_mask = [jnp.where((_iota((c, GLA_W), 1) >> SHIFT_D) == h, 1.0, 0.0).astype(BF16) for h in range(GLA_HEADS)]
    ones_b = jnp.ones((c, LANES), BF16)
    neg_inf = jnp.float32(-jnp.inf)
    causal = _iota((c, c), 1) <= _iota((c, c), 0)
    causal4 = _iota((GLA_HEADS * c, c), 1) <= (_iota((GLA_HEADS * c, c), 0) & (c - 1))
    bm_gk = jnp.where((_iota((GLA_KW, GLA_W), 0) >> SHIFT_DK) == (_iota((GLA_KW, GLA_W), 1) >> SHIFT_D), 1.0, 0.0)
    bm_pair = jnp.where((_iota((LANES, LANES), 0) >> SHIFT_D) == (_iota((LANES, LANES), 1) >> SHIFT_D), 1.0, 0.0)
    bm_aug = jnp.concatenate([bm_pair, bm_pair], axis=1)
    sub_lo = _iota((LANES, c), 0) < hd

    def stack_pair(xb):
        return jnp.concatenate([xb * pair_mask[0], xb * pair_mask[1]], axis=0)

    gprev_ref[...] = gst_ref[...]
    on256 = on256_ref[...]

    def prep(g):
        rs = slice(g * group, (g + 1) * group)
        qk = [_dot(z_ref[rs, Z_MU + p * LANES:Z_MU + (p + 1) * LANES], wqk_ref[p]) for p in range(PAIRS)]
        out = {'qm': jnp.concatenate([t[:, :LANES] for t in qk], axis=1),
               'km': jnp.concatenate([t[:, LANES:] for t in qk], axis=1)}
        small = z_ref[rs, Z_SM:Z_SM + LANES]
        la = _log_sigmoid(_dot(small, wa2_ref[...]) + ba_ref[...]) * (LOG2E / GLA_TAU)
        gates = small + bif_ref[...]
        lf = _log_sigmoid(gates) * LOG2E
        yield
        cum = jnp.dot(tril_ref[...], jnp.concatenate(_split2(la) + _split2(lf), axis=1),
                      preferred_element_type=F32)
        out['bg'] = cum[:, :LANES] + cum[:, LANES:2 * LANES]
        la_ref[rs, :] = la
        bt = cum[:, 2 * LANES:3 * LANES] + cum[:, 3 * LANES:]
        yield
        bxe = jnp.dot(jnp.concatenate(_split2(bt), axis=1), ind_ref[...], preferred_element_type=F32)
        out['bx'] = bxe[:, :ML_HEADS * LANES]
        out['dg'] = gates * LOG2E - bxe[:, ML_HEADS * LANES:]
        preps[g] = out

    def gla_norm(o, rs):
        return (o * lax.rsqrt(_dot(o * o, on256) + EPS) * gnorm_ref[...]
                * _silu(z_ref[rs, Z_GG:Z_GG + GLA_W]))

    def finish(rs, parts):
        raw = [jnp.concatenate(t, axis=0) for t in zip(*parts)]
        n_col = (RET_W + ML_W) // GLA_W
        os_ = [jnp.concatenate(raw[2 * k:2 * k + 2], axis=1) for k in range(n_col)]
        ocs = [o - _dot(o, on256) for o in os_]
        mix_ref[rs, 0:GLA_W] = gla_norm(raw[2 * n_col], rs)
        yield
        ons = [oc * lax.rsqrt(_dot(oc * oc, on256) + EPS) for oc in ocs]
        yield
        for k in range(n_col):
            on = ons[k]
            for j in range(GLA_W // LANES):
                t = (GLA_W // LANES) * k + j
                tile = on[:, j * LANES:(j + 1) * LANES]
                if t < PAIRS:
                    tile = (tile * rnorm_ref[:, t * LANES:(t + 1) * LANES]
                            * _silu(z_ref[rs, Z_RG + t * LANES:Z_RG + (t + 1) * LANES]))
                else:
                    tile = tile * mnorm_ref[:, (t - PAIRS) * LANES:(t - PAIRS + 1) * LANES]
                mix_ref[rs, MIX_RET + t * LANES:MIX_RET + (t + 1) * LANES] = tile

    per_group = group // c
    n_chunks = block // c

    def first_ret(ci):
        rows = slice(ci * c, (ci + 1) * c)
        d = {}
        d['rvb'] = z_ref[rows, Z_RV:Z_RV + RET_W].astype(BF16)
        kr = z_ref[rows, Z_RK:Z_RK + RET_W]
        d['qr'] = z_ref[rows, Z_RQ:Z_RQ + RET_W]
        d['rpm'] = [(lax.dot_general(stack_pair(d['qr'][:, p * LANES:(p + 1) * LANES].astype(BF16)),
                                     kr[:, p * LANES:(p + 1) * LANES].astype(BF16), _NT,
                                     preferred_element_type=F32) * rdec_ref[p]).astype(BF16)
                    for p in range(PAIRS)]
        kr_t = (kr.T * rwend_ref[...]).astype(BF16)
        d['rupd'] = [_dot(kr_t[p * LANES:(p + 1) * LANES], d['rvb'][:, p * LANES:(p + 1) * LANES])
                     for p in range(PAIRS)]
        return d

    def first_dots(ci, pre, lr, d):
        rows = slice(ci * c, (ci + 1) * c)
        b = pre['bg'][lr]
        bmid = b[c // 2 - 1:c // 2, :]
        bl = b[c - 1:c, :]
        gq = z_ref[rows, Z_GQ:Z_GQ + GLA_KW] * (GLA_DK ** -0.5)
        gk = z_ref[rows, Z_GK:Z_GK + GLA_KW]
        eq = jnp.exp2(b - bmid)
        ek = jnp.exp2(bmid - b)
        qh = gq * eq
        kh = gk * ek
        fac = jnp.maximum(jnp.maximum(jnp.abs(qh), jnp.abs(kh)), jnp.maximum(eq, ek))
        d['fac'] = functools.reduce(jnp.maximum, [fac[i:i + SUBLANES] for i in range(0, c, SUBLANES)])
        qh = qh.astype(BF16)
        d['qe'] = (gq * jnp.exp2(b)).astype(BF16)
        kl_t = (gk * jnp.exp2(bl - b)).T
        lhs = jnp.concatenate([qh * gla_mask[h] for h in range(GLA_HEADS)], axis=0)
        a = lax.dot_general(lhs, kh.astype(BF16), _NT, preferred_element_type=F32)
        d['a'] = jnp.where(causal4, a, 0.0).astype(BF16)
        d['gvb'] = z_ref[rows, Z_GV:Z_GV + GLA_W].astype(BF16)
        d['gupd'] = _dot(kl_t, d['gvb'])
        d['bl_col'] = b[c - SUBLANES:c, :].T[:, SUBLANES - 1:SUBLANES]
        d['mvb'] = z_ref[rows, Z_MV:Z_MV + ML_W].astype(BF16)
        km = pre['km'][lr]
        d['km_t'] = km.T
        d['ms2'] = [lax.dot_general(stack_pair(pre['qm'][lr, p * LANES:(p + 1) * LANES].astype(BF16)),
                                    km[:, p * LANES:(p + 1) * LANES].astype(BF16), _NT,
                                    preferred_element_type=F32) for p in range(PAIRS)]
        dg_t = pre['dg'][lr].T
        d['dg_t'] = dg_t
        d['mrow'] = [jnp.max(jnp.where(causal, pre['bx'][lr, h * LANES:h * LANES + c]
                                       + dg_t[SM_IG + h:SM_IG + h + 1, :], neg_inf), axis=1, keepdims=True)
                     for h in range(ML_HEADS)]
        return d

    def second_dots(ci, pre, lr, d):
        rows = slice(ci * c, (ci + 1) * c)
        raw = []
        a = d['a']
        o = jnp.dot(jnp.concatenate([a[h * c:(h + 1) * c] for h in range(GLA_HEADS)] + [d['qe']], axis=1),
                    jnp.concatenate([d['gvb'] * gv_mask[h] for h in range(GLA_HEADS)]
                                    + [gst_ref[...].astype(BF16)], axis=0),
                    preferred_element_type=F32)
        gst_ref[...] = gst_ref[...] * jnp.exp2(d['bl_col']) + bm_gk * d['gupd']
        for p in range(PAIRS):
            lanes = slice(p * LANES, (p + 1) * LANES)
            vb = d['rvb'][:, lanes]
            pm = d['rpm'][p]
            raw.append(jnp.dot(
                jnp.concatenate([pm[:c], pm[c:], (d['qr'][:, lanes] * rint_ref[:, lanes]).astype(BF16)], axis=1),
                jnp.concatenate([vb * pair_mask[0], vb * pair_mask[1], sst_ref[p].astype(BF16)], axis=0),
                preferred_element_type=F32))
            sst_ref[p] = sst_ref[p] * rgam_ref[:, lanes] + bm_pair * d['rupd'][p]
        dg_t = d['dg_t']
        wls, w0ls = [], []
        for p in range(PAIRS):
            lanes = slice(p * LANES, (p + 1) * LANES)
            wd, w0, mt = [], [], []
            for a_ in range(2):
                h = 2 * p + a_
                colr = pre['bx'][lr, h * LANES:(h + 1) * LANES]
                logd = jnp.where(causal, colr[:, :c] + dg_t[SM_IG + h:SM_IG + h + 1, :], neg_inf)
                log0 = colr + mst_ref[:, h * LANES:(h + 1) * LANES]
                m_t = jnp.maximum(log0, d['mrow'][h])
                wd.append(jnp.exp2(logd - m_t[:, :c]))
                w0.append(jnp.exp2(log0 - m_t))
                mt.append(m_t)
                mst_ref[:, h * LANES:(h + 1) * LANES] = m_t[c - 1:c, :]
            qp = pre['qm'][lr, lanes]
            vb = d['mvb'][:, lanes]
            a0 = (d['ms2'][p][:c] * wd[0]).astype(BF16)
            a1 = (d['ms2'][p][c:] * wd[1]).astype(BF16)
            w0p = jnp.where(lane_lo, w0[0], w0[1])
            mtp = jnp.where(lane_lo, mt[0], mt[1])
            nd = jnp.dot(
                jnp.concatenate([a0, a1, (qp * w0p).astype(BF16)], axis=1),
                jnp.concatenate([jnp.concatenate([vb * pair_mask[0], pair_mask[0]], axis=1),
                                 jnp.concatenate([vb * pair_mask[1], pair_mask[1]], axis=1),
                                 cst_ref[p].astype(BF16)], axis=0),
                preferred_element_type=F32)
            hh = nd[:, :LANES] / jnp.maximum(jnp.abs(nd[:, LANES:]), jnp.exp2(-mtp))
            raw.append(hh * _sigmoid(z_ref[rows, Z_MO + p * LANES:Z_MO + (p + 1) * LANES]))
            wls.append(jnp.where(sub_lo, wd[0][c - 1:c, :], wd[1][c - 1:c, :]))
            w0l = w0p[c - 1:c, :]
            w0ls.append(jnp.concatenate([w0l, w0l], axis=1))
        for p in range(PAIRS):
            lanes = slice(p * LANES, (p + 1) * LANES)
            upd = _dot(d['km_t'][lanes] * wls[p], jnp.concatenate([d['mvb'][:, lanes], ones_b], axis=1))
            cst_ref[p] = cst_ref[p] * w0ls[p] + bm_aug * upd
        return raw + [o]

    preps, raws, firsts, pending = {}, {}, {}, []

    def drain(gen):
        for _ in gen:
            pass

    def tick():
        pending[:] = [gen for gen in pending if next(gen, pending) is not pending]

    fac_max = jnp.zeros((SUBLANES, LANES), F32)
    place = lambda ci: (ci // per_group, slice((ci % per_group) * c, (ci % per_group + 1) * c))
    rets = {}
    for _ in prep(0):
        if len(rets) < min(per_group, n_chunks):
            rets[len(rets)] = first_ret(len(rets))
    firsts[0] = first_dots(0, preps[0], place(0)[1], rets.pop(0))
    for ci in range(n_chunks):
        g, lr = place(ci)
        if ci % per_group == 0:
            if g > 0:
                pending.append(finish(slice((g - 1) * group, g * group), raws.pop(g - 1)))
            if (g + 1) * group < block:
                pending.append(prep(g + 1))
        tick()
        if ci + 1 < n_chunks:
            g1, lr1 = place(ci + 1)
            while g1 not in preps:
                tick()
            firsts[ci + 1] = first_dots(ci + 1, preps[g1], lr1, rets.pop(ci + 1, None) or first_ret(ci + 1))
        tick()
        d = firsts.pop(ci)
        fac_max = jnp.maximum(fac_max, d['fac'])
        raws.setdefault(g, []).append(second_dots(ci, preps[g], lr, d))
        tick()
    while pending:
        tick()
    last_g = block // group - 1
    drain(finish(slice(last_g * group, block), raws.pop(last_g)))

    @pl.when(jnp.logical_not(jnp.max(fac_max) < GLA_FACTOR_LIMIT))
    def _gla_per_token():
        gst_ref[...] = gprev_ref[...]

        def tokens8(g, carry):
            rows8 = pl.ds(pl.multiple_of(g * SUBLANES, SUBLANES), SUBLANES)
            alpha_t = jnp.exp2(la_ref[rows8, :]).T
            k_t = z_ref[rows8, Z_GK:Z_GK + GLA_KW].T
            q_t = (z_ref[rows8, Z_GQ:Z_GQ + GLA_KW] * (GLA_DK ** -0.5)).T
            v8 = z_ref[rows8, Z_GV:Z_GV + GLA_W]
            s = gst_ref[...]
            outs = []
            for j in range(SUBLANES):
                s = s * alpha_t[:, j:j + 1] + bm_gk * (k_t[:, j:j + 1] * v8[j:j + 1, :])
                outs.append(jnp.sum(q_t[:, j:j + 1] * s, axis=0, keepdims=True))
            gst_ref[...] = s
            mix_ref[rows8, 0:GLA_W] = jnp.concatenate(outs, axis=0)
            return carry

        lax.fori_loop(0, block // SUBLANES, tokens8, 0)
        mix_ref[:, 0:GLA_W] = gla_norm(mix_ref[:, 0:GLA_W], slice(0, block))

    @pl.when(step == last)
    def _finish():
        for h in range(GLA_HEADS):
            gout_ref[h] = gst_ref[h * GLA_DK:(h + 1) * GLA_DK, h * GLA_DV:(h + 1) * GLA_DV]
        for h in range(RET_HEADS):
            p, a = divmod(h, 2)
            rout_ref[h] = sst_ref[p, a * hd:(a + 1) * hd, a * hd:(a + 1) * hd]
            cout_ref[h] = cst_ref[p, a * hd:(a + 1) * hd, a * hd:(a + 1) * hd]
            nout_ref[h] = cst_ref[p, a * hd:(a + 1) * hd, LANES + a * hd:LANES + (a + 1) * hd]
        mout_ref[...] = mst_ref[...] * (1.0 / LOG2E)


def _mixer_constants(block, chunk):
    t = np.arange(block)
    tril = ((t[:, None] // chunk == t[None, :] // chunk) & (t[None, :] <= t[:, None])).astype(np.float32)
    r = np.arange(2 * LANES)[:, None] % LANES
    col = np.arange(ML_HEADS * LANES)[None, :] // LANES
    lane = np.arange(LANES)[None, :]
    ind = np.concatenate([r == col + SM_FG, (r == lane + ML_HEADS) & (lane >= SM_IG) & (lane < SM_FG)],
                         axis=1).astype(np.float32)
    i256 = np.arange(GLA_W) // GLA_DV
    on256 = (i256[:, None] == i256[None, :]).astype(np.float32) / GLA_DV
    return tuple(jnp.asarray(m, BF16) for m in (tril, ind, on256))


def _mixer(z3, tabs, wts, init, layer):
    bsz, seq, _ = z3.shape
    block = min(MIXER_BLOCK, seq)
    chunk = min(MIXER_CHUNK, block)
    has_init = init is not None
    rdec, rint, rwend, rgam = tabs
    consts = _mixer_constants(min(MIXER_GROUP, block), chunk)

    def const(shape):
        nd = len(shape)
        return pl.BlockSpec(shape, lambda b, s: (0,) * nd)

    def per_layer(shape):
        nd = len(shape)
        return pl.BlockSpec((None,) + shape, lambda b, s: (layer,) + (0,) * nd)

    def per_batch(shape, with_layer):
        nd = len(shape)
        if with_layer:
            return pl.BlockSpec((None, None) + shape, lambda b, s: (layer, b) + (0,) * nd)
        return pl.BlockSpec((None,) + shape, lambda b, s: (b,) + (0,) * nd)

    state_shapes = [(GLA_HEADS, GLA_DK, GLA_DV), (RET_HEADS, RET_D, RET_D), (ML_HEADS, ML_D, ML_D),
                    (ML_HEADS, ML_D, ML_D), (1, ML_HEADS * LANES)]
    in_specs = [
        pl.BlockSpec((None, block, Z_W), lambda b, s: (b, s, 0)),
        const(rdec.shape), const(rint.shape), const(rwend.shape), const(rgam.shape),
    ] + [const(m.shape) for m in consts] + [
        per_layer((LANES, GLA_KW)), per_layer((1, GLA_KW)), per_layer((1, LANES)),
        per_layer((1, GLA_W)), per_layer((1, RET_W)), per_layer((1, ML_W)),
        per_layer((PAIRS, LANES, 2 * LANES)),
    ]
    args = [z3, rdec, rint, rwend, rgam] + list(consts) + list(wts)
    if has_init:
        in_specs += [per_batch(s, True) for s in state_shapes]
        args += list(init)
    out_specs = [pl.BlockSpec((None, block, D_MODEL), lambda b, s: (b, s, 0))]
    out_specs += [per_batch(s, False) for s in state_shapes]
    out_shape = [jax.ShapeDtypeStruct((bsz, seq, D_MODEL), F32)]
    out_shape += [jax.ShapeDtypeStruct((bsz,) + s, F32) for s in state_shapes]
    scratch = [
        pltpu.VMEM((GLA_KW, GLA_W), F32), pltpu.VMEM((PAIRS, LANES, LANES), F32),
        pltpu.VMEM((PAIRS, LANES, 2 * LANES), F32), pltpu.VMEM((1, ML_HEADS * LANES), F32),
        pltpu.VMEM((block, LANES), F32),
        pltpu.VMEM((GLA_KW, GLA_W), F32),
    ]
    return pl.pallas_call(
        functools.partial(_mixer_kernel, block=block, chunk=chunk, has_init=has_init),
        grid=(bsz, seq // block),
        in_specs=in_specs,
        out_specs=out_specs,
        out_shape=out_shape,
        scratch_shapes=scratch,
        compiler_params=pltpu.CompilerParams(dimension_semantics=("parallel", "arbitrary"),
                                             vmem_limit_bytes=VMEM_LIMIT),
        name="mixer",
    )(*args)


def _ffn_kernel(*refs, final):
    if final:
        x_ref, mix_ref, wo_ref, g_ref, wg_ref, wu_ref, wd_ref, gf_ref, out_ref, y_ref = refs
    else:
        x_ref, mix_ref, wo_ref, g_ref, wg_ref, wu_ref, wd_ref, out_ref = refs
    x1 = x_ref[...] + jnp.dot(mix_ref[...].astype(BF16), wo_ref[...], preferred_element_type=F32)
    r = lax.rsqrt(jnp.mean(x1 * x1, axis=-1, keepdims=True) + EPS)
    hf = (x1 * r * g_ref[...]).astype(BF16)
    acc = x1
    for c in range(D_FF // FF_CHUNK):
        cols = slice(c * FF_CHUNK, (c + 1) * FF_CHUNK)
        gate = jnp.dot(hf, wg_ref[:, cols], preferred_element_type=F32)
        up = jnp.dot(hf, wu_ref[:, cols], preferred_element_type=F32)
        act = (_silu(gate) * up).astype(BF16)
        acc = acc + jnp.dot(act, wd_ref[cols, :], preferred_element_type=F32)
    out_ref[...] = acc
    if final:
        rf = lax.rsqrt(jnp.mean(acc * acc, axis=-1, keepdims=True) + EPS)
        y_ref[...] = acc * rf * gf_ref[...]


def _ffn(x2d, mix2d, wo, norm3, wg, wu, wd, layer, norm_final2):
    n = x2d.shape[0]
    tm = min(TOKEN_BLOCK, n)
    final = norm_final2 is not None

    def weight(shape):
        return pl.BlockSpec((None,) + shape, lambda i: (layer, 0, 0), pipeline_mode=pl.Buffered(1))

    row_spec = pl.BlockSpec((tm, D_MODEL), lambda i: (i, 0))
    in_specs = [row_spec, row_spec, weight((D_MODEL, D_MODEL)),
                pl.BlockSpec((None, 1, D_MODEL), lambda i: (layer, 0, 0)),
                weight((D_MODEL, D_FF)), weight((D_MODEL, D_FF)), weight((D_FF, D_MODEL))]
    args = [x2d, mix2d, wo, norm3, wg, wu, wd]
    out_specs = [row_spec]
    out_shape = [jax.ShapeDtypeStruct((n, D_MODEL), F32)]
    if final:
        in_specs.append(pl.BlockSpec((1, D_MODEL), lambda i: (0, 0)))
        args.append(norm_final2)
        out_specs.append(row_spec)
        out_shape.append(jax.ShapeDtypeStruct((n, D_MODEL), F32))
    return pl.pallas_call(
        functools.partial(_ffn_kernel, final=final),
        grid=(n // tm,),
        in_specs=in_specs,
        out_specs=out_specs,
        out_shape=out_shape,
        compiler_params=pltpu.CompilerParams(dimension_semantics=("parallel",),
                                             vmem_limit_bytes=VMEM_LIMIT),
        name="ffn",
    )(*args)


def _reorder_w_in(w_in):
    offs = [0]
    for sz in SPLIT_SIZES:
        offs.append(offs[-1] + sz)
    ga0, ga1, gates0 = offs[3], offs[4], offs[12]
    pad = jnp.zeros(w_in.shape[:-1] + (LANES - GLA_RANK - 2 * ML_HEADS,), w_in.dtype)
    w = jnp.concatenate([w_in[..., :ga0], w_in[..., ga1:gates0], w_in[..., ga0:ga1], w_in[..., gates0:], pad],
                        axis=-1)
    scale = np.ones((Z_W,), np.float32)
    scale[Z_RK:Z_RV] = RET_D ** -0.5
    return (w * scale).astype(BF16)


def _block_diag_pairs(w):
    depth, h, d, e = w.shape
    eye = jnp.eye(2, dtype=w.dtype)
    return jnp.einsum('lpade,ab->lpadbe', w.reshape(depth, h // 2, 2, d, e), eye).reshape(
        depth, h // 2, 2 * d, 2 * e)


def _rope_tables(pos):
    inv = ROPE_BASE ** (-jnp.arange(0, RET_D, 2, dtype=F32) / RET_D)
    ang = pos.astype(F32)[:, None] * inv[None, :]
    cos, sin = jnp.cos(ang), jnp.sin(ang)
    cos_h = jnp.concatenate([cos, cos], axis=-1)
    sin_h = jnp.concatenate([-sin, sin], axis=-1)
    return jnp.tile(cos_h, (1, RET_HEADS)), jnp.tile(sin_h, (1, RET_HEADS))


def _retention_tables(chunk):
    lg = jnp.log1p(-jnp.exp2(-5.0 - jnp.arange(RET_HEADS, dtype=F32)))
    t = jnp.arange(chunk, dtype=F32)
    diff = t[:, None] - t[None, :]
    decay = jnp.where((diff >= 0)[..., None], jnp.exp(jnp.maximum(diff, 0.0)[..., None] * lg), 0.0)
    decay = jnp.transpose(decay, (2, 0, 1)).reshape(PAIRS, 2 * chunk, chunk)
    inter = jnp.repeat(jnp.exp((t[:, None] + 1.0) * lg), RET_D, axis=1)
    w_end = jnp.exp((chunk - 1.0 - t)[None, :] * lg[:, None])
    w_end = jnp.repeat(w_end, RET_D, axis=0)
    gam = jnp.repeat(jnp.exp(chunk * lg), RET_D)[None, :]
    return decay, inter, w_end, gam


def kernel(x_prompt, x_sample, state_gla, state_ret, state_mlstm_c, state_mlstm_n, state_mlstm_m,
           cache_mlstm_conv, norm_mix, w_in, gla_w_a2, gla_b_a, gla_norm, ret_norm, ml_conv_w,
           ml_conv_b, ml_wq, ml_wk, ml_b_if, ml_norm, w_out, norm_ffn, w_gate, w_up, w_down, norm_final):
    depth = w_in.shape[0]

    w_in_p = _reorder_w_in(w_in)
    wo, wg, wu, wd = (w.astype(BF16) for w in (w_out, w_gate, w_up, w_down))
    wa2 = jnp.zeros((depth, LANES, GLA_KW), F32).at[:, :GLA_RANK, :].set(gla_w_a2).astype(BF16)
    bif = jnp.zeros((depth, 1, LANES), F32).at[:, 0, SM_IG:SM_IG + 2 * ML_HEADS].set(ml_b_if)
    wqk = jnp.concatenate([_block_diag_pairs(ml_wq), _block_diag_pairs(ml_wk) * (ML_D ** -0.5)],
                          axis=-1).astype(BF16)
    row = lambda a: a[:, None, :]
    wts = (wa2, row(gla_b_a), bif, row(gla_norm), row(ret_norm), row(ml_norm), wqk)
    conv_b3 = row(ml_conv_b)
    norm_mix3, norm_ffn3, norm_final2 = row(norm_mix), row(norm_ffn), norm_final[None, :]

    init_sample = (state_gla, state_ret, state_mlstm_c,
                   jnp.broadcast_to(state_mlstm_n[..., None], state_mlstm_n.shape + (ML_D,)),
                   jnp.repeat(state_mlstm_m, LANES, axis=-1)[:, :, None, :])

    def run(x, pos, init, conv_cache):
        bsz, seq, _ = x.shape
        chunk = min(MIXER_CHUNK, MIXER_BLOCK, seq)
        assert seq % chunk == 0 and chunk & (chunk - 1) == 0 and chunk >= 2 * 8
        rope = _rope_tables(pos)
        tabs = _retention_tables(chunk)
        x2d = x.reshape(bsz * seq, D_MODEL)
        states = []
        y2d = None
        for i in range(depth):
            z, cv = _inproj(x2d.reshape(bsz, seq, D_MODEL), norm_mix3, w_in_p, rope, ml_conv_w, conv_b3,
                            conv_cache, i)
            mixed, *st = _mixer(z, tabs, wts, init, i)
            outs = _ffn(x2d, mixed.reshape(bsz * seq, D_MODEL), wo, norm_ffn3, wg, wu, wd, i,
                        norm_final2 if i == depth - 1 else None)
            x2d = outs[0]
            if i == depth - 1:
                y2d = outs[1]
            states.append(st + [cv])
        g, r, c, n, m, cv = (jnp.stack([st[j] for st in states]) for j in range(6))
        return (y2d.reshape(bsz, seq, D_MODEL), g, r, c,
                n[..., 0], m[:, :, 0, ::LANES], cv)

    pos_p = jnp.arange(x_prompt.shape[1], dtype=jnp.int32)
    pos_s = PAST_LEN + jnp.arange(x_sample.shape[1], dtype=jnp.int32)
    yp, gp, rp, cp, np_, mp, cvp = run(x_prompt, pos_p, None, None)
    ys, gs, rs, cs, ns, ms, cvs = run(x_sample, pos_s, init_sample, cache_mlstm_conv)
    return (yp, ys, gp, gs, rp, rs, cp, cs, np_, ns, mp, ms, cvp, cvs)
```

```python
import functools

import numpy as np
import jax
import jax.numpy as jnp
from jax import lax
from jax.experimental import pallas as pl
from jax.experimental.pallas import tpu as pltpu

F32 = jnp.float32
BF16 = jnp.bfloat16

D_MODEL = 1024
EPS = 1e-6
GLA_HEADS, GLA_DK, GLA_DV, GLA_RANK, GLA_TAU = 4, 32, 64, 16, 16.0
RET_HEADS, RET_D = 6, 64
ML_HEADS, ML_D = 6, 64
CONV_W = 4
ROPE_BASE = 10000.0
PAST_LEN = 2048
GLA_KW = GLA_HEADS * GLA_DK
GLA_W = GLA_HEADS * GLA_DV
RET_W = RET_HEADS * RET_D
ML_W = ML_HEADS * ML_D
D_FF = 2816
SPLIT_SIZES = (GLA_KW, GLA_KW, GLA_W, GLA_RANK, GLA_W, RET_W, RET_W, RET_W, RET_W, ML_W, ML_W, ML_W,
               ML_HEADS, ML_HEADS)

LANES = 128
SUBLANES = 8
PAIRS = RET_W // LANES
Z_GQ = 0
Z_GK = Z_GQ + GLA_KW
Z_GV = Z_GK + GLA_KW
Z_GG = Z_GV + GLA_W
Z_RQ = Z_GG + GLA_W
Z_RK = Z_RQ + RET_W
Z_RV = Z_RK + RET_W
Z_RG = Z_RV + RET_W
Z_MU = Z_RG + RET_W
Z_MV = Z_MU + ML_W
Z_MO = Z_MV + ML_W
Z_SM = Z_MO + ML_W
Z_W = Z_SM + LANES
SM_IG, SM_FG = GLA_RANK, GLA_RANK + ML_HEADS
SHIFT_DK, SHIFT_D = GLA_DK.bit_length() - 1, RET_D.bit_length() - 1
MIX_RET = GLA_W

TOKEN_BLOCK = 512
INPROJ_BLOCK = 1024
MIXER_BLOCK = 1024
MIXER_CHUNK = 128
MIXER_GROUP = 256
FF_CHUNK = 1408
LOG2E = 1.4426950408889634
GLA_FACTOR_LIMIT = 1e18
VMEM_LIMIT = 56 * 1024 * 1024

_NT = (((1,), (1,)), ((), ()))


def _dot(a, b):
    return jnp.dot(a.astype(BF16), b.astype(BF16), preferred_element_type=F32)


def _split2(x):
    hi = x.astype(BF16)
    return [hi, (x - hi.astype(F32)).astype(BF16)]


def _log_sigmoid(x):
    return jnp.minimum(x, 0.0) - jnp.log(1.0 + jnp.exp(-jnp.abs(x)))


def _sigmoid(x):
    return 0.5 + 0.5 * jnp.tanh(0.5 * x)


def _silu(x):
    h = 0.5 * x
    return h + h * jnp.tanh(h)


def _iota(shape, dim):
    return lax.broadcasted_iota(jnp.int32, shape, dim)


def _inproj_kernel(*refs, tm, has_init):
    if has_init:
        x_ref, g_ref, w_ref, cos_ref, sin_ref, cw_ref, cb_ref, cv0_ref, z_ref, cvout_ref, xp_ref = refs
    else:
        x_ref, g_ref, w_ref, cos_ref, sin_ref, cw_ref, cb_ref, z_ref, cvout_ref, xp_ref = refs
    conv_lo = SUBLANES - (CONV_W - 1)

    @pl.when(pl.program_id(1) == 0)
    def _start_of_sequence():
        if has_init:
            xp_ref[conv_lo:SUBLANES, :] = cv0_ref[...]
        else:
            xp_ref[conv_lo:SUBLANES, :] = jnp.zeros((CONV_W - 1, ML_W), F32)

    x = x_ref[...]
    r = lax.rsqrt(jnp.mean(x * x, axis=-1, keepdims=True) + EPS)
    h = (x * r * g_ref[...]).astype(BF16)

    def proj(lo, hi):
        return jnp.dot(h, w_ref[:, lo:hi], preferred_element_type=F32)

    z_ref[:, Z_GQ:Z_RQ] = proj(Z_GQ, Z_RQ)
    seg = proj(Z_RQ, Z_RV)
    lo32 = (_iota((tm, LANES), 1) & (RET_D // 2)) == 0
    for p in range(2 * PAIRS):
        lanes = slice((p % PAIRS) * LANES, (p % PAIRS + 1) * LANES)
        t = seg[:, p * LANES:(p + 1) * LANES]
        sw = jnp.where(lo32, pltpu.roll(t, LANES - 32, 1), pltpu.roll(t, 32, 1))
        z_ref[:, Z_RQ + p * LANES:Z_RQ + (p + 1) * LANES] = t * cos_ref[:, lanes] + sw * sin_ref[:, lanes]
    z_ref[:, Z_RV:Z_MU] = proj(Z_RV, Z_MU)
    seg = proj(Z_MU, Z_MO)
    z_ref[:, Z_MV:Z_MO] = seg[:, ML_W:]
    xp_ref[SUBLANES:SUBLANES + tm, :] = seg[:, :ML_W]
    cw = cw_ref[...]
    conv = cb_ref[...]
    for j in range(CONV_W):
        conv = conv + xp_ref[conv_lo + j:conv_lo + j + tm, :] * cw[j:j + 1, :]
    z_ref[:, Z_MU:Z_MV] = _silu(conv)
    conv_tail = xp_ref[tm + conv_lo:tm + SUBLANES, :]
    xp_ref[conv_lo:SUBLANES, :] = conv_tail
    cvout_ref[...] = conv_tail
    z_ref[:, Z_MO:Z_W] = proj(Z_MO, Z_W)


def _inproj(x3, norm3, w_in_p, rope, cw, cb, cv0, layer):
    bsz, seq, _ = x3.shape
    tm = min(INPROJ_BLOCK, seq)
    has_init = cv0 is not None
    cos, sin = rope
    in_specs = [
        pl.BlockSpec((None, tm, D_MODEL), lambda b, s: (b, s, 0)),
        pl.BlockSpec((None, 1, D_MODEL), lambda b, s: (layer, 0, 0)),
        pl.BlockSpec((None, D_MODEL, Z_W), lambda b, s: (layer, 0, 0), pipeline_mode=pl.Buffered(1)),
        pl.BlockSpec((tm, RET_W), lambda b, s: (s, 0)),
        pl.BlockSpec((tm, RET_W), lambda b, s: (s, 0)),
        pl.BlockSpec((None, CONV_W, ML_W), lambda b, s: (layer, 0, 0)),
        pl.BlockSpec((None, 1, ML_W), lambda b, s: (layer, 0, 0)),
    ]
    args = [x3, norm3, w_in_p, cos, sin, cw, cb]
    if has_init:
        in_specs.append(pl.BlockSpec((None, None, CONV_W - 1, ML_W), lambda b, s: (layer, b, 0, 0)))
        args.append(cv0)
    return pl.pallas_call(
        functools.partial(_inproj_kernel, tm=tm, has_init=has_init),
        grid=(bsz, seq // tm),
        in_specs=in_specs,
        out_specs=[pl.BlockSpec((None, tm, Z_W), lambda b, s: (b, s, 0)),
                   pl.BlockSpec((None, CONV_W - 1, ML_W), lambda b, s: (b, 0, 0))],
        out_shape=[jax.ShapeDtypeStruct((bsz, seq, Z_W), F32),
                   jax.ShapeDtypeStruct((bsz, CONV_W - 1, ML_W), F32)],
        scratch_shapes=[pltpu.VMEM((tm + SUBLANES, ML_W), F32)],
        compiler_params=pltpu.CompilerParams(dimension_semantics=("parallel", "arbitrary"),
                                             vmem_limit_bytes=VMEM_LIMIT),
        name="inproj",
    )(*args)


def _mixer_kernel(*refs, block, chunk, has_init):
    it = iter(refs)
    z_ref, rdec_ref, rint_ref, rwend_ref, rgam_ref, tril_ref, ind_ref, on256_ref = (next(it) for _ in range(8))
    wa2_ref, ba_ref, bif_ref, gnorm_ref, rnorm_ref, mnorm_ref, wqk_ref = (next(it) for _ in range(7))
    if has_init:
        g0_ref, r0_ref, c0_ref, n0_ref, m0_ref = (next(it) for _ in range(5))
    mix_ref, gout_ref, rout_ref, cout_ref, nout_ref, mout_ref = (next(it) for _ in range(6))
    gst_ref, sst_ref, cst_ref, mst_ref, la_ref, gprev_ref = (next(it) for _ in range(6))

    step = pl.program_id(1)
    last = pl.num_programs(1) - 1
    c = chunk
    hd = RET_D
    group = min(MIXER_GROUP, block)

    @pl.when(step == 0)
    def _init():
        gst_ref[...] = jnp.zeros_like(gst_ref)
        sst_ref[...] = jnp.zeros_like(sst_ref)
        cst_ref[...] = jnp.zeros_like(cst_ref)
        if has_init:
            for h in range(GLA_HEADS):
                gst_ref[h * GLA_DK:(h + 1) * GLA_DK, h * GLA_DV:(h + 1) * GLA_DV] = g0_ref[h]
            for h in range(RET_HEADS):
                p, a = divmod(h, 2)
                sst_ref[p, a * hd:(a + 1) * hd, a * hd:(a + 1) * hd] = r0_ref[h]
                cst_ref[p, a * hd:(a + 1) * hd, a * hd:(a + 1) * hd] = c0_ref[h]
                cst_ref[p, a * hd:(a + 1) * hd, LANES + a * hd:LANES + (a + 1) * hd] = n0_ref[h]
            mst_ref[...] = m0_ref[...] * LOG2E
        else:
            mst_ref[...] = jnp.zeros_like(mst_ref)

    lane_c = _iota((c, LANES), 1)
    lane_lo = (lane_c & hd) == 0
    pair_mask = [jnp.where(lane_lo, 1.0, 0.0).astype(BF16), jnp.where(lane_lo, 0.0, 1.0).astype(BF16)]
    gla_mask = [jnp.where((lane_c >> SHIFT_DK) == h, 1.0, 0.0).astype(BF16) for h in range(GLA_HEADS)]
    gv_mask = [jnp.where((_iota((c, GLA_W), 1) >> SHIFT_D) == h, 1.0, 0.0).astype(BF16) for h in range(GLA_HEADS)]
    ones_b = jnp.ones((c, LANES), BF16)
    neg_inf = jnp.float32(-jnp.inf)
    causal = _iota((c, c), 1) <= _iota((c, c), 0)
    causal4 = _iota((GLA_HEADS * c, c), 1) <= (_iota((GLA_HEADS * c, c), 0) & (c - 1))
    bm_gk = jnp.where((_iota((GLA_KW, GLA_W), 0) >> SHIFT_DK) == (_iota((GLA_KW, GLA_W), 1) >> SHIFT_D), 1.0, 0.0)
    bm_pair = jnp.where((_iota((LANES, LANES), 0) >> SHIFT_D) == (_iota((LANES, LANES), 1) >> SHIFT_D), 1.0, 0.0)
    bm_aug = jnp.concatenate([bm_pair, bm_pair], axis=1)
    sub_lo = _iota((LANES, c), 0) < hd

    def stack_pair(xb):
        return jnp.concatenate([xb * pair_mask[0], xb * pair_mask[1]], axis=0)

    gprev_ref[...] = gst_ref[...]
    on256 = on256_ref[...]

    def prep(g):
        rs = slice(g * group, (g + 1) * group)
        qk = [_dot(z_ref[rs, Z_MU + p * LANES:Z_MU + (p + 1) * LANES], wqk_ref[p]) for p in range(PAIRS)]
        out = {'qm': jnp.concatenate([t[:, :LANES] for t in qk], axis=1),
               'km': jnp.concatenate([t[:, LANES:] for t in qk], axis=1)}
        small = z_ref[rs, Z_SM:Z_SM + LANES]
        la = _log_sigmoid(_dot(small, wa2_ref[...]) + ba_ref[...]) * (LOG2E / GLA_TAU)
        gates = small + bif_ref[...]
        lf = _log_sigmoid(gates) * LOG2E
        yield
        cum = jnp.dot(tril_ref[...], jnp.concatenate(_split2(la) + _split2(lf), axis=1),
                      preferred_element_type=F32)
        out['bg'] = cum[:, :LANES] + cum[:, LANES:2 * LANES]
        la_ref[rs, :] = la
        bt = cum[:, 2 * LANES:3 * LANES] + cum[:, 3 * LANES:]
        yield
        bxe = jnp.dot(jnp.concatenate(_split2(bt), axis=1), ind_ref[...], preferred_element_type=F32)
        out['bx'] = bxe[:, :ML_HEADS * LANES]
        out['dg'] = gates * LOG2E - bxe[:, ML_HEADS * LANES:]
        preps[g] = out

    def gla_norm(o, rs):
        return (o * lax.rsqrt(_dot(o * o, on256) + EPS) * gnorm_ref[...]
                * _silu(z_ref[rs, Z_GG:Z_GG + GLA_W]))

    def finish(rs, parts):
        raw = [jnp.concatenate(t, axis=0) for t in zip(*parts)]
        n_col = (RET_W + ML_W) // GLA_W
        os_ = [jnp.concatenate(raw[2 * k:2 * k + 2], axis=1) for k in range(n_col)]
        ocs = [o - _dot(o, on256) for o in os_]
        mix_ref[rs, 0:GLA_W] = gla_norm(raw[2 * n_col], rs)
        yield
        ons = [oc * lax.rsqrt(_dot(oc * oc, on256) + EPS) for oc in ocs]
        yield
        for k in range(n_col):
            on = ons[k]
            for j in range(GLA_W // LANES):
                t = (GLA_W // LANES) * k + j
                tile = on[:, j * LANES:(j + 1) * LANES]
                if t < PAIRS:
                    tile = (tile * rnorm_ref[:, t * LANES:(t + 1) * LANES]
                            * _silu(z_ref[rs, Z_RG + t * LANES:Z_RG + (t + 1) * LANES]))
                else:
                    tile = tile * mnorm_ref[:, (t - PAIRS) * LANES:(t - PAIRS + 1) * LANES]
                mix_ref[rs, MIX_RET + t * LANES:MIX_RET + (t + 1) * LANES] = tile

    per_group = group // c
    n_chunks = block // c

    def first_ret(ci):
        rows = slice(ci * c, (ci + 1) * c)
        d = {}
        d['rvb'] = z_ref[rows, Z_RV:Z_RV + RET_W].astype(BF16)
        kr = z_ref[rows, Z_RK:Z_RK + RET_W]
        d['qr'] = z_ref[rows, Z_RQ:Z_RQ + RET_W]
        d['rpm'] = [(lax.dot_general(stack_pair(d['qr'][:, p * LANES:(p + 1) * LANES].astype(BF16)),
                                     kr[:, p * LANES:(p + 1) * LANES].astype(BF16), _NT,
                                     preferred_element_type=F32) * rdec_ref[p]).astype(BF16)
                    for p in range(PAIRS)]
        kr_t = (kr.T * rwend_ref[...]).astype(BF16)
        d['rupd'] = [_dot(kr_t[p * LANES:(p + 1) * LANES], d['rvb'][:, p * LANES:(p + 1) * LANES])
                     for p in range(PAIRS)]
        return d

    def first_dots(ci, pre, lr, d):
        rows = slice(ci * c, (ci + 1) * c)
        b = pre['bg'][lr]
        bmid = b[c // 2 - 1:c // 2, :]
        bl = b[c - 1:c, :]
        gq = z_ref[rows, Z_GQ:Z_GQ + GLA_KW] * (GLA_DK ** -0.5)
        gk = z_ref[rows, Z_GK:Z_GK + GLA_KW]
        eq = jnp.exp2(b - bmid)
        ek = jnp.exp2(bmid - b)
        qh = gq * eq
        kh = gk * ek
        fac = jnp.maximum(jnp.maximum(jnp.abs(qh), jnp.abs(kh)), jnp.maximum(eq, ek))
        d['fac'] = functools.reduce(jnp.maximum, [fac[i:i + SUBLANES] for i in range(0, c, SUBLANES)])
        qh = qh.astype(BF16)
        d['qe'] = (gq * jnp.exp2(b)).astype(BF16)
        kl_t = (gk * jnp.exp2(bl - b)).T
        lhs = jnp.concatenate([qh * gla_mask[h] for h in range(GLA_HEADS)], axis=0)
        a = lax.dot_general(lhs, kh.astype(BF16), _NT, preferred_element_type=F32)
        d['a'] = jnp.where(causal4, a, 0.0).astype(BF16)
        d['gvb'] = z_ref[rows, Z_GV:Z_GV + GLA_W].astype(BF16)
        d['gupd'] = _dot(kl_t, d['gvb'])
        d['bl_col'] = b[c - SUBLANES:c, :].T[:, SUBLANES - 1:SUBLANES]
        d['mvb'] = z_ref[rows, Z_MV:Z_MV + ML_W].astype(BF16)
        km = pre['km'][lr]
        d['km_t'] = km.T
        d['ms2'] = [lax.dot_general(stack_pair(pre['qm'][lr, p * LANES:(p + 1) * LANES].astype(BF16)),
                                    km[:, p * LANES:(p + 1) * LANES].astype(BF16), _NT,
                                    preferred_element_type=F32) for p in range(PAIRS)]
        dg_t = pre['dg'][lr].T
        d['dg_t'] = dg_t
        d['mrow'] = [jnp.max(jnp.where(causal, pre['bx'][lr, h * LANES:h * LANES + c]
                                       + dg_t[SM_IG + h:SM_IG + h + 1, :], neg_inf), axis=1, keepdims=True)
                     for h in range(ML_HEADS)]
        return d

    def second_dots(ci, pre, lr, d):
        rows = slice(ci * c, (ci + 1) * c)
        raw = []
        a = d['a']
        o = jnp.dot(jnp.concatenate([a[h * c:(h + 1) * c] for h in range(GLA_HEADS)] + [d['qe']], axis=1),
                    jnp.concatenate([d['gvb'] * gv_mask[h] for h in range(GLA_HEADS)]
                                    + [gst_ref[...].astype(BF16)], axis=0),
                    preferred_element_type=F32)
        gst_ref[...] = gst_ref[...] * jnp.exp2(d['bl_col']) + bm_gk * d['gupd']
        for p in range(PAIRS):
            lanes = slice(p * LANES, (p + 1) * LANES)
            vb = d['rvb'][:, lanes]
            pm = d['rpm'][p]
            raw.append(jnp.dot(
                jnp.concatenate([pm[:c], pm[c:], (d['qr'][:, lanes] * rint_ref[:, lanes]).astype(BF16)], axis=1),
                jnp.concatenate([vb * pair_mask[0], vb * pair_mask[1], sst_ref[p].astype(BF16)], axis=0),
                preferred_element_type=F32))
            sst_ref[p] = sst_ref[p] * rgam_ref[:, lanes] + bm_pair * d['rupd'][p]
        dg_t = d['dg_t']
        wls, w0ls = [], []
        for p in range(PAIRS):
            lanes = slice(p * LANES, (p + 1) * LANES)
            wd, w0, mt = [], [], []
            for a_ in range(2):
                h = 2 * p + a_
                colr = pre['bx'][lr, h * LANES:(h + 1) * LANES]
                logd = jnp.where(causal, colr[:, :c] + dg_t[SM_IG + h:SM_IG + h + 1, :], neg_inf)
                log0 = colr + mst_ref[:, h * LANES:(h + 1) * LANES]
                m_t = jnp.maximum(log0, d['mrow'][h])
                wd.append(jnp.exp2(logd - m_t[:, :c]))
                w0.append(jnp.exp2(log0 - m_t))
                mt.append(m_t)
                mst_ref[:, h * LANES:(h + 1) * LANES] = m_t[c - 1:c, :]
            qp = pre['qm'][lr, lanes]
            vb = d['mvb'][:, lanes]
            a0 = (d['ms2'][p][:c] * wd[0]).astype(BF16)
            a1 = (d['ms2'][p][c:] * wd[1]).astype(BF16)
            w0p = jnp.where(lane_lo, w0[0], w0[1])
            mtp = jnp.where(lane_lo, mt[0], mt[1])
            nd = jnp.dot(
                jnp.concatenate([a0, a1, (qp * w0p).astype(BF16)], axis=1),
                jnp.concatenate([jnp.concatenate([vb * pair_mask[0], pair_mask[0]], axis=1),
                                 jnp.concatenate([vb * pair_mask[1], pair_mask[1]], axis=1),
                                 cst_ref[p].astype(BF16)], axis=0),
                preferred_element_type=F32)
            hh = nd[:, :LANES] / jnp.maximum(jnp.abs(nd[:, LANES:]), jnp.exp2(-mtp))
            raw.append(hh * _sigmoid(z_ref[rows, Z_MO + p * LANES:Z_MO + (p + 1) * LANES]))
            wls.append(jnp.where(sub_lo, wd[0][c - 1:c, :], wd[1][c - 1:c, :]))
            w0l = w0p[c - 1:c, :]
            w0ls.append(jnp.concatenate([w0l, w0l], axis=1))
        for p in range(PAIRS):
            lanes = slice(p * LANES, (p + 1) * LANES)
            upd = _dot(d['km_t'][lanes] * wls[p], jnp.concatenate([d['mvb'][:, lanes], ones_b], axis=1))
            cst_ref[p] = cst_ref[p] * w0ls[p] + bm_aug * upd
        return raw + [o]

    preps, raws, firsts, pending = {}, {}, {}, []

    def drain(gen):
        for _ in gen:
            pass

    def tick():
        pending[:] = [gen for gen in pending if next(gen, pending) is not pending]

    fac_max = jnp.zeros((SUBLANES, LANES), F32)
    place = lambda ci: (ci // per_group, slice((ci % per_group) * c, (ci % per_group + 1) * c))
    rets = {}
    for _ in prep(0):
        if len(rets) < min(per_group, n_chunks):
            rets[len(rets)] = first_ret(len(rets))
    firsts[0] = first_dots(0, preps[0], place(0)[1], rets.pop(0))
    for ci in range(n_chunks):
        g, lr = place(ci)
        if ci % per_group == 0:
            if g > 0:
                pending.append(finish(slice((g - 1) * group, g * group), raws.pop(g - 1)))
            if (g + 1) * group < block:
                pending.append(prep(g + 1))
        tick()
        if ci + 1 < n_chunks:
            g1, lr1 = place(ci + 1)
            while g1 not in preps:
                tick()
            firsts[ci + 1] = first_dots(ci + 1, preps[g1], lr1, rets.pop(ci + 1, None) or first_ret(ci + 1))
        tick()
        d = firsts.pop(ci)
        fac_max = jnp.maximum(fac_max, d['fac'])
        raws.setdefault(g, []).append(second_dots(ci, preps[g], lr, d))
        tick()
    while pending:
        tick()
    last_g = block // group - 1
    drain(finish(slice(last_g * group, block), raws.pop(last_g)))

    @pl.when(jnp.logical_not(jnp.max(fac_max) < GLA_FACTOR_LIMIT))
    def _gla_per_token():
        gst_ref[...] = gprev_ref[...]

        def tokens8(g, carry):
            rows8 = pl.ds(pl.multiple_of(g * SUBLANES, SUBLANES), SUBLANES)
            alpha_t = jnp.exp2(la_ref[rows8, :]).T
            k_t = z_ref[rows8, Z_GK:Z_GK + GLA_KW].T
            q_t = (z_ref[rows8, Z_GQ:Z_GQ + GLA_KW] * (GLA_DK ** -0.5)).T
            v8 = z_ref[rows8, Z_GV:Z_GV + GLA_W]
            s = gst_ref[...]
            outs = []
            for j in range(SUBLANES):
                s = s * alpha_t[:, j:j + 1] + bm_gk * (k_t[:, j:j + 1] * v8[j:j + 1, :])
                outs.append(jnp.sum(q_t[:, j:j + 1] * s, axis=0, keepdims=True))
            gst_ref[...] = s
            mix_ref[rows8, 0:GLA_W] = jnp.concatenate(outs, axis=0)
            return carry

        lax.fori_loop(0, block // SUBLANES, tokens8, 0)
        mix_ref[:, 0:GLA_W] = gla_norm(mix_ref[:, 0:GLA_W], slice(0, block))

    @pl.when(step == last)
    def _finish():
        for h in range(GLA_HEADS):
            gout_ref[h] = gst_ref[h * GLA_DK:(h + 1) * GLA_DK, h * GLA_DV:(h + 1) * GLA_DV]
        for h in range(RET_HEADS):
            p, a = divmod(h, 2)
            rout_ref[h] = sst_ref[p, a * hd:(a + 1) * hd, a * hd:(a + 1) * hd]
            cout_ref[h] = cst_ref[p, a * hd:(a + 1) * hd, a * hd:(a + 1) * hd]
            nout_ref[h] = cst_ref[p, a * hd:(a + 1) * hd, LANES + a * hd:LANES + (a + 1) * hd]
        mout_ref[...] = mst_ref[...] * (1.0 / LOG2E)


def _mixer_constants(block, chunk):
    t = np.arange(block)
    tril = ((t[:, None] // chunk == t[None, :] // chunk) & (t[None, :] <= t[:, None])).astype(np.float32)
    r = np.arange(2 * LANES)[:, None] % LANES
    col = np.arange(ML_HEADS * LANES)[None, :] // LANES
    lane = np.arange(LANES)[None, :]
    ind = np.concatenate([r == col + SM_FG, (r == lane + ML_HEADS) & (lane >= SM_IG) & (lane < SM_FG)],
                         axis=1).astype(np.float32)
    i256 = np.arange(GLA_W) // GLA_DV
    on256 = (i256[:, None] == i256[None, :]).astype(np.float32) / GLA_DV
    return tuple(jnp.asarray(m, BF16) for m in (tril, ind, on256))


def _mixer(z3, tabs, wts, init, layer):
    bsz, seq, _ = z3.shape
    block = min(MIXER_BLOCK, seq)
    chunk = min(MIXER_CHUNK, block)
    has_init = init is not None
    rdec, rint, rwend, rgam = tabs
    consts = _mixer_constants(min(MIXER_GROUP, block), chunk)

    def const(shape):
        nd = len(shape)
        return pl.BlockSpec(shape, lambda b, s: (0,) * nd)

    def per_layer(shape):
        nd = len(shape)
        return pl.BlockSpec((None,) + shape, lambda b, s: (layer,) + (0,) * nd)

    def per_batch(shape, with_layer):
        nd = len(shape)
        if with_layer:
            return pl.BlockSpec((None, None) + shape, lambda b, s: (layer, b) + (0,) * nd)
        return pl.BlockSpec((None,) + shape, lambda b, s: (b,) + (0,) * nd)

    state_shapes = [(GLA_HEADS, GLA_DK, GLA_DV), (RET_HEADS, RET_D, RET_D), (ML_HEADS, ML_D, ML_D),
                    (ML_HEADS, ML_D, ML_D), (1, ML_HEADS * LANES)]
    in_specs = [
        pl.BlockSpec((None, block, Z_W), lambda b, s: (b, s, 0)),
        const(rdec.shape), const(rint.shape), const(rwend.shape), const(rgam.shape),
    ] + [const(m.shape) for m in consts] + [
        per_layer((LANES, GLA_KW)), per_layer((1, GLA_KW)), per_layer((1, LANES)),
        per_layer((1, GLA_W)), per_layer((1, RET_W)), per_layer((1, ML_W)),
        per_layer((PAIRS, LANES, 2 * LANES)),
    ]
    args = [z3, rdec, rint, rwend, rgam] + list(consts) + list(wts)
    if has_init:
        in_specs += [per_batch(s, True) for s in state_shapes]
        args += list(init)
    out_specs = [pl.BlockSpec((None, block, D_MODEL), lambda b, s: (b, s, 0))]
    out_specs += [per_batch(s, False) for s in state_shapes]
    out_shape = [jax.ShapeDtypeStruct((bsz, seq, D_MODEL), F32)]
    out_shape += [jax.ShapeDtypeStruct((bsz,) + s, F32) for s in state_shapes]
    scratch = [
        pltpu.VMEM((GLA_KW, GLA_W), F32), pltpu.VMEM((PAIRS, LANES, LANES), F32),
        pltpu.VMEM((PAIRS, LANES, 2 * LANES), F32), pltpu.VMEM((1, ML_HEADS * LANES), F32),
        pltpu.VMEM((block, LANES), F32),
        pltpu.VMEM((GLA_KW, GLA_W), F32),
    ]
    return pl.pallas_call(
        functools.partial(_mixer_kernel, block=block, chunk=chunk, has_init=has_init),
        grid=(bsz, seq // block),
        in_specs=in_specs,
        out_specs=out_specs,
        out_shape=out_shape,
        scratch_shapes=scratch,
        compiler_params=pltpu.CompilerParams(dimension_semantics=("parallel", "arbitrary"),
                                             vmem_limit_bytes=VMEM_LIMIT),
        name="mixer",
    )(*args)


def _ffn_kernel(*refs, final):
    if final:
        x_ref, mix_ref, wo_ref, g_ref, wg_ref, wu_ref, wd_ref, gf_ref, out_ref, y_ref = refs
    else:
        x_ref, mix_ref, wo_ref, g_ref, wg_ref, wu_ref, wd_ref, out_ref = refs
    x1 = x_ref[...] + jnp.dot(mix_ref[...].astype(BF16), wo_ref[...], preferred_element_type=F32)
    r = lax.rsqrt(jnp.mean(x1 * x1, axis=-1, keepdims=True) + EPS)
    hf = (x1 * r * g_ref[...]).astype(BF16)
    acc = x1
    for c in range(D_FF // FF_CHUNK):
        cols = slice(c * FF_CHUNK, (c + 1) * FF_CHUNK)
        gate = jnp.dot(hf, wg_ref[:, cols], preferred_element_type=F32)
        up = jnp.dot(hf, wu_ref[:, cols], preferred_element_type=F32)
        act = (_silu(gate) * up).astype(BF16)
        acc = acc + jnp.dot(act, wd_ref[cols, :], preferred_element_type=F32)
    out_ref[...] = acc
    if final:
        rf = lax.rsqrt(jnp.mean(acc * acc, axis=-1, keepdims=True) + EPS)
        y_ref[...] = acc * rf * gf_ref[...]


def _ffn(x2d, mix2d, wo, norm3, wg, wu, wd, layer, norm_final2):
    n = x2d.shape[0]
    tm = min(TOKEN_BLOCK, n)
    final = norm_final2 is not None

    def weight(shape):
        return pl.BlockSpec((None,) + shape, lambda i: (layer, 0, 0), pipeline_mode=pl.Buffered(1))

    row_spec = pl.BlockSpec((tm, D_MODEL), lambda i: (i, 0))
    in_specs = [row_spec, row_spec, weight((D_MODEL, D_MODEL)),
                pl.BlockSpec((None, 1, D_MODEL), lambda i: (layer, 0, 0)),
                weight((D_MODEL, D_FF)), weight((D_MODEL, D_FF)), weight((D_FF, D_MODEL))]
    args = [x2d, mix2d, wo, norm3, wg, wu, wd]
    out_specs = [row_spec]
    out_shape = [jax.ShapeDtypeStruct((n, D_MODEL), F32)]
    if final:
        in_specs.append(pl.BlockSpec((1, D_MODEL), lambda i: (0, 0)))
        args.append(norm_final2)
        out_specs.append(row_spec)
        out_shape.append(jax.ShapeDtypeStruct((n, D_MODEL), F32))
    return pl.pallas_call(
        functools.partial(_ffn_kernel, final=final),
        grid=(n // tm,),
        in_specs=in_specs,
        out_specs=out_specs,
        out_shape=out_shape,
        compiler_params=pltpu.CompilerParams(dimension_semantics=("parallel",),
                                             vmem_limit_bytes=VMEM_LIMIT),
        name="ffn",
    )(*args)


def _reorder_w_in(w_in):
    offs = [0]
    for sz in SPLIT_SIZES:
        offs.append(offs[-1] + sz)
    ga0, ga1, gates0 = offs[3], offs[4], offs[12]
    pad = jnp.zeros(w_in.shape[:-1] + (LANES - GLA_RANK - 2 * ML_HEADS,), w_in.dtype)
    w = jnp.concatenate([w_in[..., :ga0], w_in[..., ga1:gates0], w_in[..., ga0:ga1], w_in[..., gates0:], pad],
                        axis=-1)
    scale = np.ones((Z_W,), np.float32)
    scale[Z_RK:Z_RV] = RET_D ** -0.5
    return (w * scale).astype(BF16)


def _block_diag_pairs(w):
    depth, h, d, e = w.shape
    eye = jnp.eye(2, dtype=w.dtype)
    return jnp.einsum('lpade,ab->lpadbe', w.reshape(depth, h // 2, 2, d, e), eye).reshape(
        depth, h // 2, 2 * d, 2 * e)


def _rope_tables(pos):
    inv = ROPE_BASE ** (-jnp.arange(0, RET_D, 2, dtype=F32) / RET_D)
    ang = pos.astype(F32)[:, None] * inv[None, :]
    cos, sin = jnp.cos(ang), jnp.sin(ang)
    cos_h = jnp.concatenate([cos, cos], axis=-1)
    sin_h = jnp.concatenate([-sin, sin], axis=-1)
    return jnp.tile(cos_h, (1, RET_HEADS)), jnp.tile(sin_h, (1, RET_HEADS))


def _retention_tables(chunk):
    lg = jnp.log1p(-jnp.exp2(-5.0 - jnp.arange(RET_HEADS, dtype=F32)))
    t = jnp.arange(chunk, dtype=F32)
    diff = t[:, None] - t[None, :]
    decay = jnp.where((diff >= 0)[..., None], jnp.exp(jnp.maximum(diff, 0.0)[..., None] * lg), 0.0)
    decay = jnp.transpose(decay, (2, 0, 1)).reshape(PAIRS, 2 * chunk, chunk)
    inter = jnp.repeat(jnp.exp((t[:, None] + 1.0) * lg), RET_D, axis=1)
    w_end = jnp.exp((chunk - 1.0 - t)[None, :] * lg[:, None])
    w_end = jnp.repeat(w_end, RET_D, axis=0)
    gam = jnp.repeat(jnp.exp(chunk * lg), RET_D)[None, :]
    return decay, inter, w_end, gam


def kernel(x_prompt, x_sample, state_gla, state_ret, state_mlstm_c, state_mlstm_n, state_mlstm_m,
           cache_mlstm_conv, norm_mix, w_in, gla_w_a2, gla_b_a, gla_norm, ret_norm, ml_conv_w,
           ml_conv_b, ml_wq, ml_wk, ml_b_if, ml_norm, w_out, norm_ffn, w_gate, w_up, w_down, norm_final):
    depth = w_in.shape[0]

    w_in_p = _reorder_w_in(w_in)
    wo, wg, wu, wd = (w.astype(BF16) for w in (w_out, w_gate, w_up, w_down))
    wa2 = jnp.zeros((depth, LANES, GLA_KW), F32).at[:, :GLA_RANK, :].set(gla_w_a2).astype(BF16)
    bif = jnp.zeros((depth, 1, LANES), F32).at[:, 0, SM_IG:SM_IG + 2 * ML_HEADS].set(ml_b_if)
    wqk = jnp.concatenate([_block_diag_pairs(ml_wq), _block_diag_pairs(ml_wk) * (ML_D ** -0.5)],
                          axis=-1).astype(BF16)
    row = lambda a: a[:, None, :]
    wts = (wa2, row(gla_b_a), bif, row(gla_norm), row(ret_norm), row(ml_norm), wqk)
    conv_b3 = row(ml_conv_b)
    norm_mix3, norm_ffn3, norm_final2 = row(norm_mix), row(norm_ffn), norm_final[None, :]

    init_sample = (state_gla, state_ret, state_mlstm_c,
                   jnp.broadcast_to(state_mlstm_n[..., None], state_mlstm_n.shape + (ML_D,)),
                   jnp.repeat(state_mlstm_m, LANES, axis=-1)[:, :, None, :])

    def run(x, pos, init, conv_cache):
        bsz, seq, _ = x.shape
        chunk = min(MIXER_CHUNK, MIXER_BLOCK, seq)
        assert seq % chunk == 0 and chunk & (chunk - 1) == 0 and chunk >= 2 * 8
        rope = _rope_tables(pos)
        tabs = _retention_tables(chunk)
        x2d = x.reshape(bsz * seq, D_MODEL)
        states = []
        y2d = None
        for i in range(depth):
            z, cv = _inproj(x2d.reshape(bsz, seq, D_MODEL), norm_mix3, w_in_p, rope, ml_conv_w, conv_b3,
                            conv_cache, i)
            mixed, *st = _mixer(z, tabs, wts, init, i)
            outs = _ffn(x2d, mixed.reshape(bsz * seq, D_MODEL), wo, norm_ffn3, wg, wu, wd, i,
                        norm_final2 if i == depth - 1 else None)
            x2d = outs[0]
            if i == depth - 1:
                y2d = outs[1]
            states.append(st + [cv])
        g, r, c, n, m, cv = (jnp.stack([st[j] for st in states]) for j in range(6))
        return (y2d.reshape(bsz, seq, D_MODEL), g, r, c,
                n[..., 0], m[:, :, 0, ::LANES], cv)

    pos_p = jnp.arange(x_prompt.shape[1], dtype=jnp.int32)
    pos_s = PAST_LEN + jnp.arange(x_sample.shape[1], dtype=jnp.int32)
    yp, gp, rp, cp, np_, mp, cvp = run(x_prompt, pos_p, None, None)
    ys, gs, rs, cs, ns, ms, cvs = run(x_sample, pos_s, init_sample, cache_mlstm_conv)
    return (yp, ys, gp, gs, rp, rs, cp, cs, np_, ns, mp, ms, cvp, cvs)
```

```python
import functools

import numpy as np
import jax
import jax.numpy as jnp
from jax import lax
from jax.experimental import pallas as pl
from jax.experimental.pallas import tpu as pltpu

F32 = jnp.float32
BF16 = jnp.bfloat16

D_MODEL = 1024
EPS = 1e-6
GLA_HEADS, GLA_DK, GLA_DV, GLA_RANK, GLA_TAU = 4, 32, 64, 16, 16.0
RET_HEADS, RET_D = 6, 64
ML_HEADS, ML_D = 6, 64
CONV_W = 4
ROPE_BASE = 10000.0
PAST_LEN = 2048
GLA_KW = GLA_HEADS * GLA_DK
GLA_W = GLA_HEADS * GLA_DV
RET_W = RET_HEADS * RET_D
ML_W = ML_HEADS * ML_D
D_FF = 2816
SPLIT_SIZES = (GLA_KW, GLA_KW, GLA_W, GLA_RANK, GLA_W, RET_W, RET_W, RET_W, RET_W, ML_W, ML_W, ML_W,
               ML_HEADS, ML_HEADS)

LANES = 128
SUBLANES = 8
PAIRS = RET_W // LANES
Z_GQ = 0
Z_GK = Z_GQ + GLA_KW
Z_GV = Z_GK + GLA_KW
Z_GG = Z_GV + GLA_W
Z_RQ = Z_GG + GLA_W
Z_RK = Z_RQ + RET_W
Z_RV = Z_RK + RET_W
Z_RG = Z_RV + RET_W
Z_MU = Z_RG + RET_W
Z_MV = Z_MU + ML_W
Z_MO = Z_MV + ML_W
Z_SM = Z_MO + ML_W
Z_W = Z_SM + LANES
SM_IG, SM_FG = GLA_RANK, GLA_RANK + ML_HEADS
SHIFT_DK, SHIFT_D = GLA_DK.bit_length() - 1, RET_D.bit_length() - 1
MIX_RET = GLA_W

TOKEN_BLOCK = 512
INPROJ_BLOCK = 1024
MIXER_BLOCK = 1024
MIXER_CHUNK = 128
MIXER_GROUP = 256
FF_CHUNK = 1408
LOG2E = 1.4426950408889634
GLA_FACTOR_LIMIT = 1e18
VMEM_LIMIT = 56 * 1024 * 1024

_NT = (((1,), (1,)), ((), ()))


def _dot(a, b):
    return jnp.dot(a.astype(BF16), b.astype(BF16), preferred_element_type=F32)


def _split2(x):
    hi = x.astype(BF16)
    return [hi, (x - hi.astype(F32)).astype(BF16)]


def _log_sigmoid(x):
    return jnp.minimum(x, 0.0) - jnp.log(1.0 + jnp.exp(-jnp.abs(x)))


def _sigmoid(x):
    return 0.5 + 0.5 * jnp.tanh(0.5 * x)


def _silu(x):
    h = 0.5 * x
    return h + h * jnp.tanh(h)


def _iota(shape, dim):
    return lax.broadcasted_iota(jnp.int32, shape, dim)


def _inproj_kernel(*refs, tm, has_init):
    if has_init:
        x_ref, g_ref, w_ref, cos_ref, sin_ref, cw_ref, cb_ref, cv0_ref, z_ref, cvout_ref, xp_ref = refs
    else:
        x_ref, g_ref, w_ref, cos_ref, sin_ref, cw_ref, cb_ref, z_ref, cvout_ref, xp_ref = refs
    conv_lo = SUBLANES - (CONV_W - 1)

    @pl.when(pl.program_id(1) == 0)
    def _start_of_sequence():
        if has_init:
            xp_ref[conv_lo:SUBLANES, :] = cv0_ref[...]
        else:
            xp_ref[conv_lo:SUBLANES, :] = jnp.zeros((CONV_W - 1, ML_W), F32)

    x = x_ref[...]
    r = lax.rsqrt(jnp.mean(x * x, axis=-1, keepdims=True) + EPS)
    h = (x * r * g_ref[...]).astype(BF16)

    def proj(lo, hi):
        return jnp.dot(h, w_ref[:, lo:hi], preferred_element_type=F32)

    z_ref[:, Z_GQ:Z_RQ] = proj(Z_GQ, Z_RQ)
    seg = proj(Z_RQ, Z_RV)
    lo32 = (_iota((tm, LANES), 1) & (RET_D // 2)) == 0
    for p in range(2 * PAIRS):
        lanes = slice((p % PAIRS) * LANES, (p % PAIRS + 1) * LANES)
        t = seg[:, p * LANES:(p + 1) * LANES]
        sw = jnp.where(lo32, pltpu.roll(t, LANES - 32, 1), pltpu.roll(t, 32, 1))
        z_ref[:, Z_RQ + p * LANES:Z_RQ + (p + 1) * LANES] = t * cos_ref[:, lanes] + sw * sin_ref[:, lanes]
    z_ref[:, Z_RV:Z_MU] = proj(Z_RV, Z_MU)
    seg = proj(Z_MU, Z_MO)
    z_ref[:, Z_MV:Z_MO] = seg[:, ML_W:]
    xp_ref[SUBLANES:SUBLANES + tm, :] = seg[:, :ML_W]
    cw = cw_ref[...]
    conv = cb_ref[...]
    for j in range(CONV_W):
        conv = conv + xp_ref[conv_lo + j:conv_lo + j + tm, :] * cw[j:j + 1, :]
    z_ref[:, Z_MU:Z_MV] = _silu(conv)
    conv_tail = xp_ref[tm + conv_lo:tm + SUBLANES, :]
    xp_ref[conv_lo:SUBLANES, :] = conv_tail
    cvout_ref[...] = conv_tail
    z_ref[:, Z_MO:Z_W] = proj(Z_MO, Z_W)


def _inproj(x3, norm3, w_in_p, rope, cw, cb, cv0, layer):
    bsz, seq, _ = x3.shape
    tm = min(INPROJ_BLOCK, seq)
    has_init = cv0 is not None
    cos, sin = rope
    in_specs = [
        pl.BlockSpec((None, tm, D_MODEL), lambda b, s: (b, s, 0)),
        pl.BlockSpec((None, 1, D_MODEL), lambda b, s: (layer, 0, 0)),
        pl.BlockSpec((None, D_MODEL, Z_W), lambda b, s: (layer, 0, 0), pipeline_mode=pl.Buffered(1)),
        pl.BlockSpec((tm, RET_W), lambda b, s: (s, 0)),
        pl.BlockSpec((tm, RET_W), lambda b, s: (s, 0)),
        pl.BlockSpec((None, CONV_W, ML_W), lambda b, s: (layer, 0, 0)),
        pl.BlockSpec((None, 1, ML_W), lambda b, s: (layer, 0, 0)),
    ]
    args = [x3, norm3, w_in_p, cos, sin, cw, cb]
    if has_init:
        in_specs.append(pl.BlockSpec((None, None, CONV_W - 1, ML_W), lambda b, s: (layer, b, 0, 0)))
        args.append(cv0)
    return pl.pallas_call(
        functools.partial(_inproj_kernel, tm=tm, has_init=has_init),
        grid=(bsz, seq // tm),
        in_specs=in_specs,
        out_specs=[pl.BlockSpec((None, tm, Z_W), lambda b, s: (b, s, 0)),
                   pl.BlockSpec((None, CONV_W - 1, ML_W), lambda b, s: (b, 0, 0))],
        out_shape=[jax.ShapeDtypeStruct((bsz, seq, Z_W), F32),
                   jax.ShapeDtypeStruct((bsz, CONV_W - 1, ML_W), F32)],
        scratch_shapes=[pltpu.VMEM((tm + SUBLANES, ML_W), F32)],
        compiler_params=pltpu.CompilerParams(dimension_semantics=("parallel", "arbitrary"),
                                             vmem_limit_bytes=VMEM_LIMIT),
        name="inproj",
    )(*args)


def _mixer_kernel(*refs, block, chunk, has_init):
    it = iter(refs)
    z_ref, rdec_ref, rint_ref, rwend_ref, rgam_ref, tril_ref, ind_ref, on256_ref = (next(it) for _ in range(8))
    wa2_ref, ba_ref, bif_ref, gnorm_ref, rnorm_ref, mnorm_ref, wqk_ref = (next(it) for _ in range(7))
    if has_init:
        g0_ref, r0_ref, c0_ref, n0_ref, m0_ref = (next(it) for _ in range(5))
    mix_ref, gout_ref, rout_ref, cout_ref, nout_ref, mout_ref = (next(it) for _ in range(6))
    gst_ref, sst_ref, cst_ref, mst_ref, la_ref, gprev_ref, graw_ref = (next(it) for _ in range(7))

    step = pl.program_id(1)
    last = pl.num_programs(1) - 1
    c = chunk
    hd = RET_D
    group = min(MIXER_GROUP, block)

    @pl.when(step == 0)
    def _init():
        gst_ref[...] = jnp.zeros_like(gst_ref)
        sst_ref[...] = jnp.zeros_like(sst_ref)
        cst_ref[...] = jnp.zeros_like(cst_ref)
        if has_init:
            for h in range(GLA_HEADS):
                gst_ref[h * GLA_DK:(h + 1) * GLA_DK, h * GLA_DV:(h + 1) * GLA_DV] = g0_ref[h]
            for h in range(RET_HEADS):
                p, a = divmod(h, 2)
                sst_ref[p, a * hd:(a + 1) * hd, a * hd:(a + 1) * hd] = r0_ref[h]
                cst_ref[p, a * hd:(a + 1) * hd, a * hd:(a + 1) * hd] = c0_ref[h]
                cst_ref[p, a * hd:(a + 1) * hd, LANES + a * hd:LANES + (a + 1) * hd] = n0_ref[h]
            mst_ref[...] = m0_ref[...] * LOG2E
        else:
            mst_ref[...] = jnp.zeros_like(mst_ref)

    lane_c = _iota((c, LANES), 1)
    lane_lo = (lane_c & hd) == 0
    pair_mask = [jnp.where(lane_lo, 1.0, 0.0).astype(BF16), jnp.where(lane_lo, 0.0, 1.0).astype(BF16)]
    gla_mask = [jnp.where((lane_c >> SHIFT_DK) == h, 1.0, 0.0).astype(BF16) for h in range(GLA_HEADS)]
    gv_mask = [jnp.where((_iota((c, GLA_W), 1) >> SHIFT_D) == h, 1.0, 0.0).astype(BF16) for h in range(GLA_HEADS)]
    ones_b = jnp.ones((c, LANES), BF16)
    neg_inf = jnp.float32(-jnp.inf)
    causal = _iota((c, c), 1) <= _iota((c, c), 0)
    causal4 = _iota((GLA_HEADS * c, c), 1) <= (_iota((GLA_HEADS * c, c), 0) & (c - 1))
    bm_gk = jnp.where((_iota((GLA_KW, GLA_W), 0) >> SHIFT_DK) == (_iota((GLA_KW, GLA_W), 1) >> SHIFT_D), 1.0, 0.0)
    bm_pair = jnp.where((_iota((LANES, LANES), 0) >> SHIFT_D) == (_iota((LANES, LANES), 1) >> SHIFT_D), 1.0, 0.0)
    bm_aug = jnp.concatenate([bm_pair, bm_pair], axis=1)
    sub_lo = _iota((LANES, c), 0) < hd

    def stack_pair(xb):
        return jnp.concatenate([xb * pair_mask[0], xb * pair_mask[1]], axis=0)

    gprev_ref[...] = gst_ref[...]
    on256 = on256_ref[...]

    def prep(g):
        rs = slice(g * group, (g + 1) * group)
        qk = [_dot(z_ref[rs, Z_MU + p * LANES:Z_MU + (p + 1) * LANES], wqk_ref[p]) for p in range(PAIRS)]
        out = {'qm': jnp.concatenate([t[:, :LANES] for t in qk], axis=1),
               'km': jnp.concatenate([t[:, LANES:] for t in qk], axis=1)}
        small = z_ref[rs, Z_SM:Z_SM + LANES]
        la = _log_sigmoid(_dot(small, wa2_ref[...]) + ba_ref[...]) * (LOG2E / GLA_TAU)
        gates = small + bif_ref[...]
        lf = _log_sigmoid(gates) * LOG2E
        yield
        cum = jnp.dot(tril_ref[...], jnp.concatenate(_split2(la) + _split2(lf), axis=1),
                      preferred_element_type=F32)
        out['bg'] = cum[:, :LANES] + cum[:, LANES:2 * LANES]
        la_ref[rs, :] = la
        bt = cum[:, 2 * LANES:3 * LANES] + cum[:, 3 * LANES:]
        yield
        bxe = jnp.dot(jnp.concatenate(_split2(bt), axis=1), ind_ref[...], preferred_element_type=F32)
        out['bx'] = bxe[:, :ML_HEADS * LANES]
        out['dg'] = gates * LOG2E - bxe[:, ML_HEADS * LANES:]
        preps[g] = out

    def gla_norm(o, rs):
        return (o * lax.rsqrt(_dot(o * o, on256) + EPS) * gnorm_ref[...]
                * _silu(z_ref[rs, Z_GG:Z_GG + GLA_W]))

    def finish(rs, parts):
        raw = [jnp.concatenate(t, axis=0) for t in zip(*parts)]
        n_col = (RET_W + ML_W) // GLA_W
        os_ = [jnp.concatenate(raw[2 * k:2 * k + 2], axis=1) for k in range(n_col)]
        ocs = [o - _dot(o, on256) for o in os_]
        mix_ref[rs, 0:GLA_W] = gla_norm(raw[2 * n_col], rs).astype(BF16)
        yield
        ons = [oc * lax.rsqrt(_dot(oc * oc, on256) + EPS) for oc in ocs]
        yield
        for k in range(n_col):
            on = ons[k]
            for j in range(GLA_W // LANES):
                t = (GLA_W // LANES) * k + j
                tile = on[:, j * LANES:(j + 1) * LANES]
                if t < PAIRS:
                    tile = (tile * rnorm_ref[:, t * LANES:(t + 1) * LANES]
                            * _silu(z_ref[rs, Z_RG + t * LANES:Z_RG + (t + 1) * LANES]))
                else:
                    tile = tile * mnorm_ref[:, (t - PAIRS) * LANES:(t - PAIRS + 1) * LANES]
                mix_ref[rs, MIX_RET + t * LANES:MIX_RET + (t + 1) * LANES] = tile.astype(BF16)

    per_group = group // c
    n_chunks = block // c

    def first_ret(ci):
        rows = slice(ci * c, (ci + 1) * c)
        d = {}
        d['rvb'] = z_ref[rows, Z_RV:Z_RV + RET_W].astype(BF16)
        kr = z_ref[rows, Z_RK:Z_RK + RET_W]
        d['qr'] = z_ref[rows, Z_RQ:Z_RQ + RET_W]
        d['rpm'] = [(lax.dot_general(stack_pair(d['qr'][:, p * LANES:(p + 1) * LANES].astype(BF16)),
                                     kr[:, p * LANES:(p + 1) * LANES].astype(BF16), _NT,
                                     preferred_element_type=F32) * rdec_ref[p]).astype(BF16)
                    for p in range(PAIRS)]
        kr_t = (kr.T * rwend_ref[...]).astype(BF16)
        d['rupd'] = [_dot(kr_t[p * LANES:(p + 1) * LANES], d['rvb'][:, p * LANES:(p + 1) * LANES])
                     for p in range(PAIRS)]
        return d

    def first_dots(ci, pre, lr, d):
        rows = slice(ci * c, (ci + 1) * c)
        b = pre['bg'][lr]
        bmid = b[c // 2 - 1:c // 2, :]
        bl = b[c - 1:c, :]
        gq = z_ref[rows, Z_GQ:Z_GQ + GLA_KW] * (GLA_DK ** -0.5)
        gk = z_ref[rows, Z_GK:Z_GK + GLA_KW]
        eq = jnp.exp2(b - bmid)
        ek = jnp.exp2(bmid - b)
        qh = gq * eq
        kh = gk * ek
        fac = jnp.maximum(jnp.maximum(jnp.abs(qh), jnp.abs(kh)), jnp.maximum(eq, ek))
        d['fac'] = functools.reduce(jnp.maximum, [fac[i:i + SUBLANES] for i in range(0, c, SUBLANES)])
        qh = qh.astype(BF16)
        d['qe'] = (gq * jnp.exp2(b)).astype(BF16)
        kl_t = (gk * jnp.exp2(bl - b)).T
        lhs = jnp.concatenate([qh * gla_mask[h] for h in range(GLA_HEADS)], axis=0)
        a = lax.dot_general(lhs, kh.astype(BF16), _NT, preferred_element_type=F32)
        d['a'] = jnp.where(causal4, a, 0.0).astype(BF16)
        d['gvb'] = z_ref[rows, Z_GV:Z_GV + GLA_W].astype(BF16)
        d['gupd'] = _dot(kl_t, d['gvb'])
        d['bl_col'] = b[c - SUBLANES:c, :].T[:, SUBLANES - 1:SUBLANES]
        d['mvb'] = z_ref[rows, Z_MV:Z_MV + ML_W].astype(BF16)
        km = pre['km'][lr]
        d['km_t'] = km.T
        d['ms2'] = [lax.dot_general(stack_pair(pre['qm'][lr, p * LANES:(p + 1) * LANES].astype(BF16)),
                                    km[:, p * LANES:(p + 1) * LANES].astype(BF16), _NT,
                                    preferred_element_type=F32) for p in range(PAIRS)]
        dg_t = pre['dg'][lr].T
        d['dg_t'] = dg_t
        d['mrow'] = [jnp.max(jnp.where(causal, pre['bx'][lr, h * LANES:h * LANES + c]
                                       + dg_t[SM_IG + h:SM_IG + h + 1, :], neg_inf), axis=1, keepdims=True)
                     for h in range(ML_HEADS)]
        return d

    def second_dots(ci, pre, lr, d):
        rows = slice(ci * c, (ci + 1) * c)
        raw = []
        a = d['a']
        o = jnp.dot(jnp.concatenate([a[h * c:(h + 1) * c] for h in range(GLA_HEADS)] + [d['qe']], axis=1),
                    jnp.concatenate([d['gvb'] * gv_mask[h] for h in range(GLA_HEADS)]
                                    + [gst_ref[...].astype(BF16)], axis=0),
                    preferred_element_type=F32)
        gst_ref[...] = gst_ref[...] * jnp.exp2(d['bl_col']) + bm_gk * d['gupd']
        for p in range(PAIRS):
            lanes = slice(p * LANES, (p + 1) * LANES)
            vb = d['rvb'][:, lanes]
            pm = d['rpm'][p]
            raw.append(jnp.dot(
                jnp.concatenate([pm[:c], pm[c:], (d['qr'][:, lanes] * rint_ref[:, lanes]).astype(BF16)], axis=1),
                jnp.concatenate([vb * pair_mask[0], vb * pair_mask[1], sst_ref[p].astype(BF16)], axis=0),
                preferred_element_type=F32))
            sst_ref[p] = sst_ref[p] * rgam_ref[:, lanes] + bm_pair * d['rupd'][p]
        dg_t = d['dg_t']
        wls, w0ls = [], []
        for p in range(PAIRS):
            lanes = slice(p * LANES, (p + 1) * LANES)
            wd, w0, mt = [], [], []
            for a_ in range(2):
                h = 2 * p + a_
                colr = pre['bx'][lr, h * LANES:(h + 1) * LANES]
                logd = jnp.where(causal, colr[:, :c] + dg_t[SM_IG + h:SM_IG + h + 1, :], neg_inf)
                log0 = colr + mst_ref[:, h * LANES:(h + 1) * LANES]
                m_t = jnp.maximum(log0, d['mrow'][h])
                wd.append(jnp.exp2(logd - m_t[:, :c]))
                w0.append(jnp.exp2(log0 - m_t))
                mt.append(m_t)
                mst_ref[:, h * LANES:(h + 1) * LANES] = m_t[c - 1:c, :]
            qp = pre['qm'][lr, lanes]
            vb = d['mvb'][:, lanes]
            a0 = (d['ms2'][p][:c] * wd[0]).astype(BF16)
            a1 = (d['ms2'][p][c:] * wd[1]).astype(BF16)
            w0p = jnp.where(lane_lo, w0[0], w0[1])
            mtp = jnp.where(lane_lo, mt[0], mt[1])
            nd = jnp.dot(
                jnp.concatenate([a0, a1, (qp * w0p).astype(BF16)], axis=1),
                jnp.concatenate([jnp.concatenate([vb * pair_mask[0], pair_mask[0]], axis=1),
                                 jnp.concatenate([vb * pair_mask[1], pair_mask[1]], axis=1),
                                 cst_ref[p].astype(BF16)], axis=0),
                preferred_element_type=F32)
            hh = nd[:, :LANES] / jnp.maximum(jnp.abs(nd[:, LANES:]), jnp.exp2(-mtp))
            raw.append(hh * _sigmoid(z_ref[rows, Z_MO + p * LANES:Z_MO + (p + 1) * LANES]))
            wls.append(jnp.where(sub_lo, wd[0][c - 1:c, :], wd[1][c - 1:c, :]))
            w0l = w0p[c - 1:c, :]
            w0ls.append(jnp.concatenate([w0l, w0l], axis=1))
        for p in range(PAIRS):
            lanes = slice(p * LANES, (p + 1) * LANES)
            upd = _dot(d['km_t'][lanes] * wls[p], jnp.concatenate([d['mvb'][:, lanes], ones_b], axis=1))
            cst_ref[p] = cst_ref[p] * w0ls[p] + bm_aug * upd
        return raw + [o]

    preps, raws, firsts, pending = {}, {}, {}, []

    def drain(gen):
        for _ in gen:
            pass

    def tick():
        pending[:] = [gen for gen in pending if next(gen, pending) is not pending]

    fac_max = jnp.zeros((SUBLANES, LANES), F32)
    place = lambda ci: (ci // per_group, slice((ci % per_group) * c, (ci % per_group + 1) * c))
    rets = {}
    for _ in prep(0):
        if len(rets) < min(per_group, n_chunks):
            rets[len(rets)] = first_ret(len(rets))
    firsts[0] = first_dots(0, preps[0], place(0)[1], rets.pop(0))
    for ci in range(n_chunks):
        g, lr = place(ci)
        if ci % per_group == 0:
            if g > 0:
                pending.append(finish(slice((g - 1) * group, g * group), raws.pop(g - 1)))
            if (g + 1) * group < block:
                pending.append(prep(g + 1))
        tick()
        if ci + 1 < n_chunks:
            g1, lr1 = place(ci + 1)
            while g1 not in preps:
                tick()
            firsts[ci + 1] = first_dots(ci + 1, preps[g1], lr1, rets.pop(ci + 1, None) or first_ret(ci + 1))
        tick()
        d = firsts.pop(ci)
        fac_max = jnp.maximum(fac_max, d['fac'])
        raws.setdefault(g, []).append(second_dots(ci, preps[g], lr, d))
        tick()
    while pending:
        tick()
    last_g = block // group - 1
    drain(finish(slice(last_g * group, block), raws.pop(last_g)))

    @pl.when(jnp.logical_not(jnp.max(fac_max) < GLA_FACTOR_LIMIT))
    def _gla_per_token():
        gst_ref[...] = gprev_ref[...]

        def tokens8(g, carry):
            rows8 = pl.ds(pl.multiple_of(g * SUBLANES, SUBLANES), SUBLANES)
            alpha_t = jnp.exp2(la_ref[rows8, :]).T
            k_t = z_ref[rows8, Z_GK:Z_GK + GLA_KW].T
            q_t = (z_ref[rows8, Z_GQ:Z_GQ + GLA_KW] * (GLA_DK ** -0.5)).T
            v8 = z_ref[rows8, Z_GV:Z_GV + GLA_W]
            s = gst_ref[...]
            outs = []
            for j in range(SUBLANES):
                s = s * alpha_t[:, j:j + 1] + bm_gk * (k_t[:, j:j + 1] * v8[j:j + 1, :])
                outs.append(jnp.sum(q_t[:, j:j + 1] * s, axis=0, keepdims=True))
            gst_ref[...] = s
            graw_ref[rows8, :] = jnp.concatenate(outs, axis=0)
            return carry

        lax.fori_loop(0, block // SUBLANES, tokens8, 0)
        mix_ref[:, 0:GLA_W] = gla_norm(graw_ref[...], slice(0, block)).astype(BF16)

    @pl.when(step == last)
    def _finish():
        for h in range(GLA_HEADS):
            gout_ref[h] = gst_ref[h * GLA_DK:(h + 1) * GLA_DK, h * GLA_DV:(h + 1) * GLA_DV]
        for h in range(RET_HEADS):
            p, a = divmod(h, 2)
            rout_ref[h] = sst_ref[p, a * hd:(a + 1) * hd, a * hd:(a + 1) * hd]
            cout_ref[h] = cst_ref[p, a * hd:(a + 1) * hd, a * hd:(a + 1) * hd]
            nout_ref[h] = cst_ref[p, a * hd:(a + 1) * hd, LANES + a * hd:LANES + (a + 1) * hd]
        mout_ref[...] = mst_ref[...] * (1.0 / LOG2E)


def _mixer_constants(block, chunk):
    t = np.arange(block)
    tril = ((t[:, None] // chunk == t[None, :] // chunk) & (t[None, :] <= t[:, None])).astype(np.float32)
    r = np.arange(2 * LANES)[:, None] % LANES
    col = np.arange(ML_HEADS * LANES)[None, :] // LANES
    lane = np.arange(LANES)[None, :]
    ind = np.concatenate([r == col + SM_FG, (r == lane + ML_HEADS) & (lane >= SM_IG) & (lane < SM_FG)],
                         axis=1).astype(np.float32)
    i256 = np.arange(GLA_W) // GLA_DV
    on256 = (i256[:, None] == i256[None, :]).astype(np.float32) / GLA_DV
    return tuple(jnp.asarray(m, BF16) for m in (tril, ind, on256))


def _mixer(z3, tabs, wts, init, layer):
    bsz, seq, _ = z3.shape
    block = min(MIXER_BLOCK, seq)
    chunk = min(MIXER_CHUNK, block)
    has_init = init is not None
    rdec, rint, rwend, rgam = tabs
    consts = _mixer_constants(min(MIXER_GROUP, block), chunk)

    def const(shape):
        nd = len(shape)
        return pl.BlockSpec(shape, lambda b, s: (0,) * nd)

    def per_layer(shape):
        nd = len(shape)
        return pl.BlockSpec((None,) + shape, lambda b, s: (layer,) + (0,) * nd)

    def per_batch(shape, with_layer):
        nd = len(shape)
        if with_layer:
            return pl.BlockSpec((None, None) + shape, lambda b, s: (layer, b) + (0,) * nd)
        return pl.BlockSpec((None,) + shape, lambda b, s: (b,) + (0,) * nd)

    state_shapes = [(GLA_HEADS, GLA_DK, GLA_DV), (RET_HEADS, RET_D, RET_D), (ML_HEADS, ML_D, ML_D),
                    (ML_HEADS, ML_D, ML_D), (1, ML_HEADS * LANES)]
    in_specs = [
        pl.BlockSpec((None, block, Z_W), lambda b, s: (b, s, 0)),
        const(rdec.shape), const(rint.shape), const(rwend.shape), const(rgam.shape),
    ] + [const(m.shape) for m in consts] + [
        per_layer((LANES, GLA_KW)), per_layer((1, GLA_KW)), per_layer((1, LANES)),
        per_layer((1, GLA_W)), per_layer((1, RET_W)), per_layer((1, ML_W)),
        per_layer((PAIRS, LANES, 2 * LANES)),
    ]
    args = [z3, rdec, rint, rwend, rgam] + list(consts) + list(wts)
    if has_init:
        in_specs += [per_batch(s, True) for s in state_shapes]
        args += list(init)
    out_specs = [pl.BlockSpec((None, block, D_MODEL), lambda b, s: (b, s, 0))]
    out_specs += [per_batch(s, False) for s in state_shapes]
    out_shape = [jax.ShapeDtypeStruct((bsz, seq, D_MODEL), BF16)]
    out_shape += [jax.ShapeDtypeStruct((bsz,) + s, F32) for s in state_shapes]
    scratch = [
        pltpu.VMEM((GLA_KW, GLA_W), F32), pltpu.VMEM((PAIRS, LANES, LANES), F32),
        pltpu.VMEM((PAIRS, LANES, 2 * LANES), F32), pltpu.VMEM((1, ML_HEADS * LANES), F32),
        pltpu.VMEM((block, LANES), F32),
        pltpu.VMEM((GLA_KW, GLA_W), F32),
        pltpu.VMEM((block, GLA_W), F32),
    ]
    return pl.pallas_call(
        functools.partial(_mixer_kernel, block=block, chunk=chunk, has_init=has_init),
        grid=(bsz, seq // block),
        in_specs=in_specs,
        out_specs=out_specs,
        out_shape=out_shape,
        scratch_shapes=scratch,
        compiler_params=pltpu.CompilerParams(dimension_semantics=("parallel", "arbitrary"),
                                             vmem_limit_bytes=VMEM_LIMIT),
        name="mixer",
    )(*args)


def _ffn_kernel(*refs, final):
    if final:
        x_ref, mix_ref, wo_ref, g_ref, wg_ref, wu_ref, wd_ref, gf_ref, out_ref, y_ref = refs
    else:
        x_ref, mix_ref, wo_ref, g_ref, wg_ref, wu_ref, wd_ref, out_ref = refs
    x1 = x_ref[...] + jnp.dot(mix_ref[...].astype(BF16), wo_ref[...], preferred_element_type=F32)
    r = lax.rsqrt(jnp.mean(x1 * x1, axis=-1, keepdims=True) + EPS)
    hf = (x1 * r * g_ref[...]).astype(BF16)
    acc = x1
    for c in range(D_FF // FF_CHUNK):
        cols = slice(c * FF_CHUNK, (c + 1) * FF_CHUNK)
        gate = jnp.dot(hf, wg_ref[:, cols], preferred_element_type=F32)
        up = jnp.dot(hf, wu_ref[:, cols], preferred_element_type=F32)
        act = (_silu(gate) * up).astype(BF16)
        acc = acc + jnp.dot(act, wd_ref[cols, :], preferred_element_type=F32)
    out_ref[...] = acc
    if final:
        rf = lax.rsqrt(jnp.mean(acc * acc, axis=-1, keepdims=True) + EPS)
        y_ref[...] = acc * rf * gf_ref[...]


def _ffn(x2d, mix2d, wo, norm3, wg, wu, wd, layer, norm_final2):
    n = x2d.shape[0]
    tm = min(TOKEN_BLOCK, n)
    final = norm_final2 is not None

    def weight(shape):
        return pl.BlockSpec((None,) + shape, lambda i: (layer, 0, 0), pipeline_mode=pl.Buffered(1))

    row_spec = pl.BlockSpec((tm, D_MODEL), lambda i: (i, 0))
    in_specs = [row_spec, row_spec, weight((D_MODEL, D_MODEL)),
                pl.BlockSpec((None, 1, D_MODEL), lambda i: (layer, 0, 0)),
                weight((D_MODEL, D_FF)), weight((D_MODEL, D_FF)), weight((D_FF, D_MODEL))]
    args = [x2d, mix2d, wo, norm3, wg, wu, wd]
    out_specs = [row_spec]
    out_shape = [jax.ShapeDtypeStruct((n, D_MODEL), F32)]
    if final:
        in_specs.append(pl.BlockSpec((1, D_MODEL), lambda i: (0, 0)))
        args.append(norm_final2)
        out_specs.append(row_spec)
        out_shape.append(jax.ShapeDtypeStruct((n, D_MODEL), F32))
    return pl.pallas_call(
        functools.partial(_ffn_kernel, final=final),
        grid=(n // tm,),
        in_specs=in_specs,
        out_specs=out_specs,
        out_shape=out_shape,
        compiler_params=pltpu.CompilerParams(dimension_semantics=("parallel",),
                                             vmem_limit_bytes=VMEM_LIMIT),
        name="ffn",
    )(*args)


def _reorder_w_in(w_in):
    offs = [0]
    for sz in SPLIT_SIZES:
        offs.append(offs[-1] + sz)
    ga0, ga1, gates0 = offs[3], offs[4], offs[12]
    wb = w_in.astype(BF16)
    pad = jnp.zeros(wb.shape[:-1] + (LANES - GLA_RANK - 2 * ML_HEADS,), BF16)
    w = jnp.concatenate([wb[..., :ga0], wb[..., ga1:gates0], wb[..., ga0:ga1], wb[..., gates0:], pad], axis=-1)
    scale = np.ones((Z_W,), np.float32)
    scale[Z_RK:Z_RV] = RET_D ** -0.5
    return w * jnp.asarray(scale, BF16)


def _block_diag_pairs(w):
    depth, h, d, e = w.shape
    eye = jnp.eye(2, dtype=w.dtype)
    return jnp.einsum('lpade,ab->lpadbe', w.reshape(depth, h // 2, 2, d, e), eye).reshape(
        depth, h // 2, 2 * d, 2 * e)


def _rope_tables(pos):
    inv = ROPE_BASE ** (-jnp.arange(0, RET_D, 2, dtype=F32) / RET_D)
    ang = pos.astype(F32)[:, None] * inv[None, :]
    cos, sin = jnp.cos(ang), jnp.sin(ang)
    cos_h = jnp.concatenate([cos, cos], axis=-1)
    sin_h = jnp.concatenate([-sin, sin], axis=-1)
    return jnp.tile(cos_h, (1, RET_HEADS)), jnp.tile(sin_h, (1, RET_HEADS))


def _retention_tables(chunk):
    lg = jnp.log1p(-jnp.exp2(-5.0 - jnp.arange(RET_HEADS, dtype=F32)))
    t = jnp.arange(chunk, dtype=F32)
    diff = t[:, None] - t[None, :]
    decay = jnp.where((diff >= 0)[..., None], jnp.exp(jnp.maximum(diff, 0.0)[..., None] * lg), 0.0)
    decay = jnp.transpose(decay, (2, 0, 1)).reshape(PAIRS, 2 * chunk, chunk)
    inter = jnp.repeat(jnp.exp((t[:, None] + 1.0) * lg), RET_D, axis=1)
    w_end = jnp.exp((chunk - 1.0 - t)[None, :] * lg[:, None])
    w_end = jnp.repeat(w_end, RET_D, axis=0)
    gam = jnp.repeat(jnp.exp(chunk * lg), RET_D)[None, :]
    return decay, inter, w_end, gam


def kernel(x_prompt, x_sample, state_gla, state_ret, state_mlstm_c, state_mlstm_n, state_mlstm_m,
           cache_mlstm_conv, norm_mix, w_in, gla_w_a2, gla_b_a, gla_norm, ret_norm, ml_conv_w,
           ml_conv_b, ml_wq, ml_wk, ml_b_if, ml_norm, w_out, norm_ffn, w_gate, w_up, w_down, norm_final):
    depth = w_in.shape[0]

    w_in_p = _reorder_w_in(w_in)
    wo, wg, wu, wd = (w.astype(BF16) for w in (w_out, w_gate, w_up, w_down))
    wa2 = jnp.zeros((depth, LANES, GLA_KW), F32).at[:, :GLA_RANK, :].set(gla_w_a2).astype(BF16)
    bif = jnp.zeros((depth, 1, LANES), F32).at[:, 0, SM_IG:SM_IG + 2 * ML_HEADS].set(ml_b_if)
    wqk = jnp.concatenate([_block_diag_pairs(ml_wq), _block_diag_pairs(ml_wk) * (ML_D ** -0.5)],
                          axis=-1).astype(BF16)
    row = lambda a: a[:, None, :]
    wts = (wa2, row(gla_b_a), bif, row(gla_norm), row(ret_norm), row(ml_norm), wqk)
    conv_b3 = row(ml_conv_b)
    norm_mix3, norm_ffn3, norm_final2 = row(norm_mix), row(norm_ffn), norm_final[None, :]

    init_sample = (state_gla, state_ret, state_mlstm_c,
                   jnp.broadcast_to(state_mlstm_n[..., None], state_mlstm_n.shape + (ML_D,)),
                   jnp.repeat(state_mlstm_m, LANES, axis=-1)[:, :, None, :])

    def run(x, pos, init, conv_cache):
        bsz, seq, _ = x.shape
        chunk = min(MIXER_CHUNK, MIXER_BLOCK, seq)
        assert seq % chunk == 0 and chunk & (chunk - 1) == 0 and chunk >= 2 * SUBLANES
        rope = _rope_tables(pos)
        tabs = _retention_tables(chunk)
        x2d = x.reshape(bsz * seq, D_MODEL)
        states = []
        y2d = None
        for i in range(depth):
            z, cv = _inproj(x2d.reshape(bsz, seq, D_MODEL), norm_mix3, w_in_p, rope, ml_conv_w, conv_b3,
                            conv_cache, i)
            mixed, *st = _mixer(z, tabs, wts, init, i)
            outs = _ffn(x2d, mixed.reshape(bsz * seq, D_MODEL), wo, norm_ffn3, wg, wu, wd, i,
                        norm_final2 if i == depth - 1 else None)
            x2d = outs[0]
            if i == depth - 1:
                y2d = outs[1]
            states.append(st + [cv])
        g, r, c, n, m, cv = (jnp.stack([st[j] for st in states]) for j in range(6))
        return (y2d.reshape(bsz, seq, D_MODEL), g, r, c,
                n[..., 0], m[:, :, 0, ::LANES], cv)

    pos_p = jnp.arange(x_prompt.shape[1], dtype=jnp.int32)
    pos_s = PAST_LEN + jnp.arange(x_sample.shape[1], dtype=jnp.int32)
    yp, gp, rp, cp, np_, mp, cvp = run(x_prompt, pos_p, None, None)
    ys, gs, rs, cs, ns, ms, cvs = run(x_sample, pos_s, init_sample, cache_mlstm_conv)
    return (yp, ys, gp, gs, rp, rs, cp, cs, np_, ns, mp, ms, cvp, cvs)
```
